```python
import jax, jax.numpy as jnp
from jax import lax
import numpy as np


D_MODEL = 1024
BATCH = 8
SEQ = 4096
DEPTH = 4

ROPE_THETA = 500000.0
Q_BLOCK = 128
LN_EPS = 1e-5
RMS_EPS = 1e-6
BRANCH_WIDTH = D_MODEL // 2
N_BRANCH = 3

A_HEADS = 8
A_HEAD_DIM = BRANCH_WIDTH // A_HEADS
A_ROT_DIM = A_HEAD_DIM // 4
IDX_HEADS = 8
IDX_HEAD_DIM = 64
IDX_TOPK_MAX = 256

B_GROUPS = 8
B_GROUP_DIM = BRANCH_WIDTH // B_GROUPS
B_CHUNK = 128

C_HEADS = 8
C_NOPE_DIM = 64
C_ROPE_DIM = 32
C_V_DIM = 64
C_Q_RANK = 256
C_KV_RANK = 128

N_EXPERTS = 32
TOP_K = 4
D_FF = D_MODEL
SWIGLU_LIMIT = 7.0
SWIGLU_ALPHA = 1.702

DEEPNORM_ALPHA = (2 * DEPTH) ** 0.25
DEEPNORM_BETA = (8 * DEPTH) ** -0.25

IN_SPLITS = (
    A_HEADS * A_HEAD_DIM,
    A_HEADS * A_HEAD_DIM,
    A_HEADS * A_HEAD_DIM,
    IDX_HEADS * IDX_HEAD_DIM,
    IDX_HEAD_DIM,
    IDX_HEADS,
    2 * BRANCH_WIDTH,
    C_Q_RANK,
    C_KV_RANK,
    C_ROPE_DIM,
    N_BRANCH * D_MODEL,
)
N_IN = sum(IN_SPLITS)

kernel_name = 'hybrid_dsa_gmlp_mla_moe_deepnorm'


def layer_norm(x, g, b):
    xf = x.astype(jnp.float32)
    mu = jnp.mean(xf, axis=-1, keepdims=True)
    var = jnp.mean(jnp.square(xf - mu), axis=-1, keepdims=True)
    y = (xf - mu) * lax.rsqrt(var + LN_EPS) * g.astype(jnp.float32) + b.astype(jnp.float32)
    return y.astype(x.dtype)


def rms_norm(x, g):
    xf = x.astype(jnp.float32)
    y = xf * lax.rsqrt(jnp.mean(jnp.square(xf), axis=-1, keepdims=True) + RMS_EPS) * g.astype(jnp.float32)
    return y.astype(x.dtype)


def rope_tables(positions, rot_dim):
    half = rot_dim // 2
    inv_freq = ROPE_THETA ** (-jnp.arange(half, dtype=jnp.float32) / half)
    ang = positions.astype(jnp.float32)[..., None] * inv_freq
    return jnp.cos(ang), jnp.sin(ang)


def apply_rope(x, cos, sin):
    half = cos.shape[-1]
    if x.ndim == 4:
        cos = cos[:, :, None]
        sin = sin[:, :, None]
    cos = cos.astype(x.dtype)
    sin = sin.astype(x.dtype)
    x1 = x[..., :half]
    x2 = x[..., half:2 * half]
    return jnp.concatenate([x1 * cos - x2 * sin, x2 * cos + x1 * sin, x[..., 2 * half:]], axis=-1)


def to_blocks(a, n_blocks):
    return a.reshape((a.shape[0], n_blocks, Q_BLOCK) + a.shape[2:]).swapaxes(0, 1)


def dsa_attention(q, k, v, iq, ik, iw):
    B_, S_ = q.shape[0], q.shape[1]
    n_sel = min(IDX_TOPK_MAX, S_ // 4)
    nb = S_ // Q_BLOCK
    key_pos = jnp.arange(S_)
    ik32 = ik.astype(jnp.float32)

    def one_block(args):
        qb, iqb, iwb, qpos = args
        s = jnp.einsum('bqhd,bsd->bqhs', iqb.astype(jnp.float32), ik32) * (IDX_HEAD_DIM ** -0.5)
        score = jnp.einsum('bqhs,bqh->bqs', jax.nn.relu(s), iwb.astype(jnp.float32))
        causal = key_pos[None, None, :] <= qpos[None, :, None]
        score = jnp.where(causal, score, -jnp.inf)
        _, sel = lax.top_k(score, n_sel)
        k_sel = jax.vmap(lambda kk, ii: kk[ii])(k, sel)
        v_sel = jax.vmap(lambda vv, ii: vv[ii])(v, sel)
        logits = jnp.einsum('bqhd,bqkhd->bhqk', qb, k_sel).astype(jnp.float32) * (A_HEAD_DIM ** -0.5)
        valid = sel <= qpos[None, :, None]
        logits = jnp.where(valid[:, None], logits, -jnp.inf)
        p = jax.nn.softmax(logits, axis=-1).astype(v.dtype)
        return jnp.einsum('bhqk,bqkhd->bqhd', p, v_sel)

    qpos_blocks = jnp.arange(S_).reshape(nb, Q_BLOCK)
    out = lax.map(one_block, (to_blocks(q, nb), to_blocks(iq, nb), to_blocks(iw, nb), qpos_blocks))
    return out.swapaxes(0, 1).reshape(B_, S_, -1)


def causal_block_attention(q, k, v):
    B_, S_ = q.shape[0], q.shape[1]
    nb = S_ // Q_BLOCK
    key_pos = jnp.arange(S_)
    scale = q.shape[-1] ** -0.5

    def one_block(args):
        qb, qpos = args
        logits = jnp.einsum('bqhd,bshd->bhqs', qb, k).astype(jnp.float32) * scale
        causal = key_pos[None, :] <= qpos[:, None]
        logits = jnp.where(causal, logits, -jnp.inf)
        p = jax.nn.softmax(logits, axis=-1).astype(v.dtype)
        return jnp.einsum('bhqs,bshd->bqhd', p, v)

    qpos_blocks = jnp.arange(S_).reshape(nb, Q_BLOCK)
    out = lax.map(one_block, (to_blocks(q, nb), qpos_blocks))
    return out.swapaxes(0, 1).reshape(B_, S_, -1)


def chunked_spatial_gating(u, v, w_s, b_s):
    B_, S_ = u.shape[0], u.shape[1]
    nc = S_ // B_CHUNK
    mask = jnp.tril(jnp.ones((B_CHUNK, B_CHUNK), dtype=bool))
    w = jnp.where(mask[None], w_s, 0)
    vc = v.reshape(B_, nc, B_CHUNK, B_GROUPS, B_GROUP_DIM)
    s = jnp.einsum('gij,bcjgd->bcigd', w, vc) + b_s.T[None, None, :, :, None]
    return u * s.reshape(B_, S_, -1)


def clamped_swiglu(h):
    glu, lin = jnp.split(h, 2, axis=-1)
    glu = jnp.minimum(glu, SWIGLU_LIMIT)
    lin = jnp.clip(lin, -SWIGLU_LIMIT, SWIGLU_LIMIT)
    return (lin + 1) * (glu * jax.nn.sigmoid(SWIGLU_ALPHA * glu))


def moe(x, rw, rb, wgu, bgu, wdn, bdn):
    B_, S_, D = x.shape
    xf = x.reshape(-1, D)
    logits = (xf @ rw + rb).astype(jnp.float32)
    top_v, top_i = lax.top_k(logits, TOP_K)
    top_w = jax.nn.softmax(top_v, axis=-1)
    gates = jnp.einsum('nk,nke->ne', top_w, jax.nn.one_hot(top_i, N_EXPERTS, dtype=jnp.float32)).astype(x.dtype)
    out = jnp.zeros_like(xf)
    for e in range(N_EXPERTS):
        h = clamped_swiglu(xf @ wgu[e] + bgu[e])
        out = out + gates[:, e:e + 1] * (h @ wdn[e] + bdn[e])
    return out.reshape(B_, S_, D)


def setup_inputs(seed: int = 0) -> dict:
    key = jax.random.key(seed)
    ks = jax.random.split(key, 24)
    L, D, W = DEPTH, D_MODEL, BRANCH_WIDTH
    nrm = lambda k, shape, scale: jax.random.normal(k, shape, jnp.float32) * scale
    return {
        'x': nrm(ks[0], (BATCH, SEQ, D), 1.0),
        'positions': jnp.broadcast_to(jnp.arange(SEQ, dtype=jnp.int32), (BATCH, SEQ)),
        'w_in': nrm(ks[1], (L, D, N_IN), D ** -0.5),
        'gm_norm_g': 1.0 + nrm(ks[2], (L, W), 0.02),
        'gm_norm_b': nrm(ks[3], (L, W), 0.02),
        'gm_w_s': nrm(ks[4], (L, B_GROUPS, B_CHUNK, B_CHUNK), B_CHUNK ** -0.5),
        'gm_b_s': 1.0 + nrm(ks[5], (L, B_GROUPS, B_CHUNK), 0.02),
        'mla_q_norm': 1.0 + nrm(ks[6], (L, C_Q_RANK), 0.02),
        'mla_kv_norm': 1.0 + nrm(ks[7], (L, C_KV_RANK), 0.02),
        'mla_w_uq': nrm(ks[8], (L, C_Q_RANK, C_HEADS * (C_NOPE_DIM + C_ROPE_DIM)), C_Q_RANK ** -0.5),
        'mla_w_ukv': nrm(ks[9], (L, C_KV_RANK, C_HEADS * (C_NOPE_DIM + C_V_DIM)), C_KV_RANK ** -0.5),
        'w_branch': nrm(ks[10], (L, N_BRANCH, W, D), W ** -0.5),
        'w_out': nrm(ks[11], (L, D, D), D ** -0.5 * DEEPNORM_BETA),
        'ln1_g': 1.0 + nrm(ks[12], (L, D), 0.02),
        'ln1_b': nrm(ks[13], (L, D), 0.02),
        'router_w': nrm(ks[14], (L, D, N_EXPERTS), D ** -0.5),
        'router_b': nrm(ks[15], (L, N_EXPERTS), 0.01),
        'exp_w_gu': nrm(ks[16], (L, N_EXPERTS, D, 2 * D_FF), D ** -0.5),
        'exp_b_gu': nrm(ks[17], (L, N_EXPERTS, 2 * D_FF), 0.02),
        'exp_w_dn': nrm(ks[18], (L, N_EXPERTS, D_FF, D), D_FF ** -0.5 * DEEPNORM_BETA),
        'exp_b_dn': nrm(ks[19], (L, N_EXPERTS, D), 0.02),
        'ln2_g': 1.0 + nrm(ks[20], (L, D), 0.02),
        'ln2_b': nrm(ks[21], (L, D), 0.02),
    }


def reference(x, positions, w_in, gm_norm_g, gm_norm_b, gm_w_s, gm_b_s, mla_q_norm, mla_kv_norm,
              mla_w_uq, mla_w_ukv, w_branch, w_out, ln1_g, ln1_b, router_w, router_b,
              exp_w_gu, exp_b_gu, exp_w_dn, exp_b_dn, ln2_g, ln2_b):
    B_, S_, _ = x.shape
    split_pts = [int(p) for p in np.cumsum(IN_SPLITS)[:-1]]
    cos_p, sin_p = rope_tables(positions, A_ROT_DIM)
    cos_c, sin_c = rope_tables(positions, C_ROPE_DIM)
    for l in range(DEPTH):
        h = x
        proj = h @ w_in[l]
        aq, ak, av, iq, ik, iw, buv, cdq, cdkv, ckr, gate_logits = jnp.split(proj, split_pts, axis=-1)

        aq = apply_rope(aq.reshape(B_, S_, A_HEADS, A_HEAD_DIM), cos_p, sin_p)
        ak = apply_rope(ak.reshape(B_, S_, A_HEADS, A_HEAD_DIM), cos_p, sin_p)
        av = av.reshape(B_, S_, A_HEADS, A_HEAD_DIM)
        iq = apply_rope(iq.reshape(B_, S_, IDX_HEADS, IDX_HEAD_DIM), cos_p, sin_p)
        ik = apply_rope(ik, cos_p, sin_p)
        iw = iw * (IDX_HEADS ** -0.5)
        o_a = dsa_attention(aq, ak, av, iq, ik, iw)

        u, v = jnp.split(jax.nn.gelu(buv, approximate=False), 2, axis=-1)
        v = layer_norm(v, gm_norm_g[l], gm_norm_b[l])
        o_b = chunked_spatial_gating(u, v, gm_w_s[l], gm_b_s[l])

        cq = rms_norm(cdq, mla_q_norm[l])
        q = (cq @ mla_w_uq[l]).reshape(B_, S_, C_HEADS, C_NOPE_DIM + C_ROPE_DIM)
        q = jnp.concatenate([q[..., :C_NOPE_DIM], apply_rope(q[..., C_NOPE_DIM:], cos_c, sin_c)], axis=-1)
        ckv = rms_norm(cdkv, mla_kv_norm[l])
        kv = (ckv @ mla_w_ukv[l]).reshape(B_, S_, C_HEADS, C_NOPE_DIM + C_V_DIM)
        k_rope = apply_rope(ckr, cos_c, sin_c)
        k = jnp.concatenate([kv[..., :C_NOPE_DIM],
                             jnp.broadcast_to(k_rope[:, :, None], (B_, S_, C_HEADS, C_ROPE_DIM))], axis=-1)
        o_c = causal_block_attention(q, k, kv[..., C_NOPE_DIM:])

        branches = jnp.stack([o_a, o_b, o_c], axis=2)
        br = jnp.einsum('bsnw,nwd->bsnd', branches, w_branch[l])
        gates = jax.nn.sigmoid(gate_logits.reshape(B_, S_, N_BRANCH, D_MODEL))
        merged = jnp.sum(gates * br, axis=2)
        y = merged @ w_out[l]
        x = layer_norm(DEEPNORM_ALPHA * h + y, ln1_g[l], ln1_b[l])

        f = moe(x, router_w[l], router_b[l], exp_w_gu[l], exp_b_gu[l], exp_w_dn[l], exp_b_dn[l])
        x = layer_norm(DEEPNORM_ALPHA * x + f, ln2_g[l], ln2_b[l])
    return x
```

```python
import functools
import math

import jax
import jax.numpy as jnp
from jax import lax
from jax.experimental import pallas as pl
from jax.experimental.pallas import tpu as pltpu

F32 = jnp.float32
BF16 = jnp.bfloat16
I32 = jnp.int32

D_MODEL = 1024
BRANCH_WIDTH = 512
N_BRANCH = 3
ROPE_THETA = 500000.0
LN_EPS = 1e-5
RMS_EPS = 1e-6

A_HEADS = 8
A_HEAD_DIM = 64
A_ROT_DIM = 16
IDX_HEADS = 8
IDX_HEAD_DIM = 64
IDX_TOPK_MAX = 256

B_GROUPS = 8
B_GROUP_DIM = 64
B_CHUNK = 128

C_HEADS = 8
C_NOPE_DIM = 64
C_ROPE_DIM = 32
C_V_DIM = 64
C_Q_RANK = 256
C_KV_RANK = 128

N_EXPERTS = 32
TOP_K = 4
D_FF = 1024
SWIGLU_LIMIT = 7.0
SWIGLU_ALPHA = 1.702

LANES = 128
ATT_T = 128
PROJ_TM = 512
MERGE_TM = 256
DISPATCH_TM = 256
COMBINE_TM = 128
EXPERT_TG = 256
VMEM_LIMIT = 56 * 1024 * 1024

NEG = -1e30
INT_MIN = -(2 ** 31)

_C_AQ, _C_AK, _C_IQ, _C_AV = 0, 512, 1024, 1536
_C_S1, _C_S2, _C_B, _C_C, _C_END = 2048, 2176, 2304, 3328, 3712


def _dot(a, b):
    return jnp.dot(a, b, preferred_element_type=F32)


def _dot_nt(a, b):
    return lax.dot_general(a, b, (((1,), (1,)), ((), ())), preferred_element_type=F32)


def _rot(y, a, bm, bp, r):
    w = y.shape[1]
    return y * a + pltpu.roll(y, w - r, 1) * bm + pltpu.roll(y, r, 1) * bp


def _layer_norm(x, g, b):
    mu = jnp.mean(x, axis=-1, keepdims=True)
    xc = x - mu
    var = jnp.mean(xc * xc, axis=-1, keepdims=True)
    return xc * lax.rsqrt(var + LN_EPS) * g + b


def _rms_norm(x, g):
    return x * lax.rsqrt(jnp.mean(x * x, axis=-1, keepdims=True) + RMS_EPS) * g


def _const_spec(shape):
    nd = len(shape)
    return pl.BlockSpec(shape, lambda *_: (0,) * nd)


def _proj_kernel(x_ref, wp_ref, wuq_ref, wke_ref, wv_ref, ws_ref, gmg_ref, gmb_ref, bsf_ref,
                 qn_ref, kvn_ref, t64_ref, t128_ref, ts2_ref,
                 aq_o, ak_o, iq_o, av_o, ik2_o, sm2_o, ob_o, q_o, k_o, v_o):
    tm = x_ref.shape[0]
    xb = x_ref[...].astype(BF16)

    def mm(c0, c1):
        return _dot(xb, wp_ref[:, c0:c1])

    a64, bm64, bp64 = t64_ref[0], t64_ref[1], t64_ref[2]
    a4 = jnp.concatenate([a64] * 4, axis=1)
    bm4 = jnp.concatenate([bm64] * 4, axis=1)
    bp4 = jnp.concatenate([bp64] * 4, axis=1)
    for c0, out in ((_C_AQ, aq_o), (_C_AK, ak_o), (_C_IQ, iq_o)):
        out[...] = _rot(mm(c0, c0 + 512), a4, bm4, bp4, A_ROT_DIM // 2).astype(BF16)
    av_o[...] = mm(_C_AV, _C_AV + 512).astype(BF16)
    ik2_o[...] = _rot(mm(_C_S1, _C_S1 + LANES), a64, bm64, bp64, A_ROT_DIM // 2).astype(BF16)
    sm2 = _rot(mm(_C_S2, _C_S2 + LANES), ts2_ref[0], ts2_ref[1], ts2_ref[2], C_ROPE_DIM // 2)
    sm2_o[...] = sm2

    buv = mm(_C_B, _C_B + 2 * BRANCH_WIDTH)
    buv = 0.5 * buv * (1.0 + lax.erf(buv * (1.0 / math.sqrt(2.0))))
    u = buv[:, :BRANCH_WIDTH]
    v = _layer_norm(buv[:, BRANCH_WIDTH:], gmg_ref[...], gmb_ref[...]).astype(BF16)
    lane = lax.broadcasted_iota(I32, (B_CHUNK, LANES), 1)
    for c in range(tm // B_CHUNK):
        rows = slice(c * B_CHUNK, (c + 1) * B_CHUNK)
        for pr in range(B_GROUPS // 2):
            cols = slice(pr * LANES, (pr + 1) * LANES)
            vp = v[rows, cols]
            s = jnp.where(lane < B_GROUP_DIM, _dot(ws_ref[2 * pr], vp), _dot(ws_ref[2 * pr + 1], vp))
            ob_o[rows, cols] = (u[rows, cols] * (s + bsf_ref[:, cols])).astype(BF16)

    cc = mm(_C_C, _C_END)
    cq = _rms_norm(cc[:, :C_Q_RANK], qn_ref[...]).astype(BF16)
    a8 = jnp.concatenate([t128_ref[0]] * C_HEADS, axis=1)
    bm8 = jnp.concatenate([t128_ref[1]] * C_HEADS, axis=1)
    bp8 = jnp.concatenate([t128_ref[2]] * C_HEADS, axis=1)
    q_o[...] = _rot(_dot(cq, wuq_ref[...]), a8, bm8, bp8, C_ROPE_DIM // 2).astype(BF16)
    ckv = _rms_norm(cc[:, C_Q_RANK:], kvn_ref[...]).astype(BF16)
    kin = jnp.concatenate([ckv, sm2.astype(BF16)], axis=1)
    k_o[...] = _dot(kin, wke_ref[...]).astype(BF16)
    v_o[...] = _dot(ckv, wv_ref[...]).astype(BF16)


def _proj(x2, wts, tabs):
    n = x2.shape[0]
    tm = min(PROJ_TM, n)
    assert n % tm == 0 and tm % B_CHUNK == 0
    row = lambda w: pl.BlockSpec((tm, w), lambda i: (i, 0))
    tab = pl.BlockSpec((3, tm, LANES), lambda i: (0, i, 0))
    in_specs = [
        row(D_MODEL),
        _const_spec((D_MODEL, _C_END)),
        _const_spec((C_Q_RANK, C_HEADS * LANES)),
        _const_spec((C_KV_RANK + LANES, C_HEADS * LANES)),
        _const_spec((C_KV_RANK, C_HEADS * C_V_DIM)),
        _const_spec((B_GROUPS, B_CHUNK, B_CHUNK)),
        _const_spec((1, BRANCH_WIDTH)),
        _const_spec((1, BRANCH_WIDTH)),
        _const_spec((B_CHUNK, BRANCH_WIDTH)),
        _const_spec((1, C_Q_RANK)),
        _const_spec((1, C_KV_RANK)),
        tab, tab, tab,
    ]
    out_shape = [
        jax.ShapeDtypeStruct((n, 512), BF16),
        jax.ShapeDtypeStruct((n, 512), BF16),
        jax.ShapeDtypeStruct((n, 512), BF16),
        jax.ShapeDtypeStruct((n, 512), BF16),
        jax.ShapeDtypeStruct((n, LANES), BF16),
        jax.ShapeDtypeStruct((n, LANES), F32),
        jax.ShapeDtypeStruct((n, 512), BF16),
        jax.ShapeDtypeStruct((n, C_HEADS * LANES), BF16),
        jax.ShapeDtypeStruct((n, C_HEADS * LANES), BF16),
        jax.ShapeDtypeStruct((n, C_HEADS * C_V_DIM), BF16),
    ]
    out_specs = [row(s.shape[1]) for s in out_shape]
    return pl.pallas_call(
        _proj_kernel,
        out_shape=out_shape,
        grid=(n // tm,),
        in_specs=in_specs,
        out_specs=out_specs,
        compiler_params=pltpu.CompilerParams(dimension_semantics=("arbitrary",),
                                             vmem_limit_bytes=VMEM_LIMIT),
        name="proj",
    )(x2, wts["wp"], wts["wuq"], wts["wke"], wts["wv"], wts["ws"], wts["gmg"], wts["gmb"],
      wts["bsf"], wts["qn"], wts["kvn"], tabs["t64"], tabs["t128"], tabs["ts2"])


def _softmax_step(s, vp, m, l, a, c2):
    mn = jnp.maximum(m, jnp.max(s, axis=1, keepdims=True))
    al = jnp.exp2((m - mn) * c2)
    p = jnp.exp2((s - mn) * c2)
    l = al * l + jnp.sum(p, axis=1, keepdims=True)
    a = al * a + _dot(p.astype(BF16), vp)
    return mn, l, a


def _att_init(t):
    return (jnp.full((t, 1), NEG, F32), jnp.zeros((t, 1), F32), jnp.zeros((t, LANES), F32))


def _dsa_kernel(iq_ref, sm2_ref, aq_ref, ik2_ref, ak_ref, av_ref, o_ref,
                key_ref, bias_ref, iqm_ref, wb_ref, thr_ref, *, n_sel):
    t = ATT_T
    i = pl.program_id(1)
    nchunk = i + 1
    lane = lax.broadcasted_iota(I32, (t, LANES), 1)
    lo = lane < IDX_HEAD_DIM
    row = lax.broadcasted_iota(I32, (t, t), 0)
    col = lax.broadcasted_iota(I32, (t, t), 1)

    for h in range(IDX_HEADS):
        pair = iq_ref[:, (h // 2) * LANES:(h // 2 + 1) * LANES]
        keep = lo if h % 2 == 0 else jnp.logical_not(lo)
        iqm_ref[h] = jnp.where(keep, pair, jnp.zeros_like(pair))
        wb_ref[h] = jnp.broadcast_to(sm2_ref[:, h:h + 1], (t, t))

    def causal_at(c):
        return (col + c * t) <= (row + i * t)

    def idx_body(c, carry):
        off = pl.multiple_of(c * t, t)
        kc = ik2_ref[pl.ds(off, t), :]
        acc = jnp.zeros((t, t), F32)
        for h in range(IDX_HEADS):
            acc = acc + jnp.maximum(_dot_nt(iqm_ref[h], kc), 0.0) * wb_ref[h]
        bits = lax.bitcast_convert_type(acc, I32)
        key = bits ^ ((bits >> 31) & 0x7FFFFFFF)
        key_ref[:, pl.ds(off, t)] = jnp.where(causal_at(c), key, INT_MIN)
        return carry

    lax.fori_loop(0, nchunk, idx_body, 0)

    thr_ref[...] = jnp.full((t, LANES), INT_MIN, I32)

    @pl.when(nchunk * t > n_sel)
    def _():
        def bit_body(it, prefix):
            trial_u = prefix | jnp.left_shift(jnp.int32(1), 31 - it)
            trial_s = trial_u ^ INT_MIN

            def cnt_body(c, acc):
                kc = key_ref[:, pl.ds(pl.multiple_of(c * t, t), t)]
                return acc + jnp.where(kc >= trial_s, 1, 0)

            acc = lax.fori_loop(0, nchunk, cnt_body, jnp.zeros((t, LANES), I32))
            cnt = jnp.sum(acc, axis=1, keepdims=True)
            return jnp.where(cnt >= n_sel, trial_u, prefix)

        prefix = lax.fori_loop(0, 32, bit_body, jnp.zeros((t, LANES), I32))
        thr_ref[...] = prefix ^ INT_MIN

    thr = thr_ref[...]

    def cnt2_body(c, carry):
        kc = key_ref[:, pl.ds(pl.multiple_of(c * t, t), t)]
        return (carry[0] + jnp.where(kc >= thr, 1, 0), carry[1] + jnp.where(kc > thr, 1, 0))

    z = jnp.zeros((t, LANES), I32)
    a_ge, a_gt = lax.fori_loop(0, nchunk, cnt2_body, (z, z))
    cnt_ge = jnp.sum(a_ge, axis=1, keepdims=True)
    need = (n_sel - jnp.sum(a_gt, axis=1, keepdims=True)).astype(F32)

    def plain_mask():
        def body(c, carry):
            off = pl.multiple_of(c * t, t)
            kc = key_ref[:, pl.ds(off, t)]
            keep = jnp.where(kc >= thr, 0.0, NEG)
            bias_ref[:, pl.ds(off, t)] = jnp.where(causal_at(c), keep, NEG)
            return carry
        lax.fori_loop(0, nchunk, body, 0)

    def tie_mask():
        upper = jnp.where(row < col, 1.0, 0.0).astype(BF16)

        def body(c, seen):
            off = pl.multiple_of(c * t, t)
            kc = key_ref[:, pl.ds(off, t)]
            eq = jnp.where(kc == thr, 1.0, 0.0)
            rank = seen + _dot(eq.astype(BF16), upper)
            tied = jnp.where(kc == thr, jnp.where(rank < need, 0.0, NEG), NEG)
            keep = jnp.where(kc > thr, 0.0, tied)
            bias_ref[:, pl.ds(off, t)] = jnp.where(causal_at(c), keep, NEG)
            return seen + jnp.sum(eq, axis=1, keepdims=True)
        lax.fori_loop(0, nchunk, body, jnp.zeros((t, 1), F32))

    lax.cond(jnp.max(cnt_ge) > n_sel, tie_mask, plain_mask)

    log2e = math.log2(math.e)
    for pr in range(A_HEADS // 2):
        cols = slice(pr * LANES, (pr + 1) * LANES)
        qp = aq_ref[:, cols]
        q0 = jnp.where(lo, qp, jnp.zeros_like(qp))
        q1 = jnp.where(lo, jnp.zeros_like(qp), qp)

        def body(c, carry):
            off = pl.multiple_of(c * t, t)
            kp = ak_ref[pl.ds(off, t), cols]
            vp = av_ref[pl.ds(off, t), cols]
            b = bias_ref[:, pl.ds(off, t)]
            s0 = _dot_nt(q0, kp) + b
            s1 = _dot_nt(q1, kp) + b
            return _softmax_step(s0, vp, *carry[:3], log2e) + _softmax_step(s1, vp, *carry[3:], log2e)

        m0, l0, a0, m1, l1, a1 = lax.fori_loop(0, nchunk, body, _att_init(t) + _att_init(t))
        o_ref[:, cols] = jnp.where(lo, a0 / l0, a1 / l1).astype(BF16)


def _dsa(iq, sm2, aq, ik2, ak, av, batch, seq):
    t = ATT_T
    nq = seq // t
    n_sel = min(IDX_TOPK_MAX, seq // 4)
    assert seq % t == 0 and n_sel % t == 0
    qspec = lambda w: pl.BlockSpec((t, w), lambda b, i: (b * nq + i, 0))
    kspec = lambda w: pl.BlockSpec((seq, w), lambda b, i: (b, 0))
    return pl.pallas_call(
        functools.partial(_dsa_kernel, n_sel=n_sel),
        out_shape=jax.ShapeDtypeStruct((batch * seq, 512), BF16),
        grid=(batch, nq),
        in_specs=[qspec(512), qspec(LANES), qspec(512), kspec(LANES), kspec(512), kspec(512)],
        out_specs=qspec(512),
        scratch_shapes=[
            pltpu.VMEM((t, seq), I32),
            pltpu.VMEM((t, seq), F32),
            pltpu.VMEM((IDX_HEADS, t, LANES), BF16),
            pltpu.VMEM((IDX_HEADS, t, t), F32),
            pltpu.VMEM((t, LANES), I32),
        ],
        compiler_params=pltpu.CompilerParams(dimension_semantics=("arbitrary", "arbitrary"),
                                             vmem_limit_bytes=VMEM_LIMIT),
        name="dsa",
    )(iq, sm2, aq, ik2, ak, av)


def _mla_kernel(q_ref, k_ref, v_ref, o_ref):
    t = ATT_T
    i = pl.program_id(1)
    lane = lax.broadcasted_iota(I32, (t, LANES), 1)
    lo = lane < C_V_DIM
    row = lax.broadcasted_iota(I32, (t, t), 0)
    col = lax.broadcasted_iota(I32, (t, t), 1)
    c2 = (C_NOPE_DIM + C_ROPE_DIM) ** -0.5 * math.log2(math.e)
    for pr in range(C_HEADS // 2):
        vcols = slice(pr * LANES, (pr + 1) * LANES)
        h0 = slice(2 * pr * LANES, (2 * pr + 1) * LANES)
        h1 = slice((2 * pr + 1) * LANES, (2 * pr + 2) * LANES)
        q0 = q_ref[:, h0]
        q1 = q_ref[:, h1]

        def step(c, carry, masked):
            off = pl.multiple_of(c * t, t)
            vp = v_ref[pl.ds(off, t), vcols]
            s0 = _dot_nt(q0, k_ref[pl.ds(off, t), h0])
            s1 = _dot_nt(q1, k_ref[pl.ds(off, t), h1])
            if masked:
                s0 = jnp.where(col <= row, s0, NEG)
                s1 = jnp.where(col <= row, s1, NEG)
            return _softmax_step(s0, vp, *carry[:3], c2) + _softmax_step(s1, vp, *carry[3:], c2)

        carry = lax.fori_loop(0, i, lambda c, cr: step(c, cr, False), _att_init(t) + _att_init(t))
        m0, l0, a0, m1, l1, a1 = step(i, carry, True)
        o_ref[:, vcols] = jnp.where(lo, a0 / l0, a1 / l1).astype(BF16)


def _mla(q, k, v, batch, seq):
    t = ATT_T
    nq = seq // t
    return pl.pallas_call(
        _mla_kernel,
        out_shape=jax.ShapeDtypeStruct((batch * seq, C_HEADS * C_V_DIM), BF16),
        grid=(batch, nq),
        in_specs=[pl.BlockSpec((t, C_HEADS * LANES), lambda b, i: (b * nq + i, 0)),
                  pl.BlockSpec((seq, C_HEADS * LANES), lambda b, i: (b, 0)),
                  pl.BlockSpec((seq, C_HEADS * C_V_DIM), lambda b, i: (b, 0))],
        out_specs=pl.BlockSpec((t, C_HEADS * C_V_DIM), lambda b, i: (b * nq + i, 0)),
        compiler_params=pltpu.CompilerParams(dimension_semantics=("arbitrary", "arbitrary"),
                                             vmem_limit_bytes=VMEM_LIMIT),
        name="mla",
    )(q, k, v)


def _merge_kernel(x_ref, oa_ref, ob_ref, oc_ref, wg_ref, wbr_ref, wo_ref, g1_ref, b1_ref,
                  rw_ref, rb_ref, x1_o, ids_o, rk_o, gt_o, cnt_o, run_ref, *, alpha):
    tm = x_ref.shape[0]

    @pl.when(pl.program_id(0) == 0)
    def _():
        run_ref[...] = jnp.zeros_like(run_ref)

    x = x_ref[...]
    xb = x.astype(BF16)
    merged = jnp.zeros((tm, D_MODEL), F32)
    for n, o_ref in enumerate((oa_ref, ob_ref, oc_ref)):
        z = _dot(xb, wg_ref[:, n * D_MODEL:(n + 1) * D_MODEL])
        merged = merged + (1.0 / (1.0 + jnp.exp(-z))) * _dot(o_ref[...], wbr_ref[n])
    y = _dot(merged.astype(BF16), wo_ref[...])
    x1 = _layer_norm(alpha * x + y, g1_ref[...], b1_ref[...])
    x1_o[...] = x1

    lane = lax.broadcasted_iota(I32, (tm, LANES), 1)
    logits = jnp.dot(x1, rw_ref[...], precision=lax.Precision.HIGHEST,
                     preferred_element_type=F32) + rb_ref[...]
    lg = jnp.where(lane < N_EXPERTS, logits, -jnp.inf)
    ids, vals = [], []
    for _ in range(TOP_K):
        mx = jnp.max(lg, axis=1, keepdims=True)
        idx = jnp.min(jnp.where(lg == mx, lane, LANES), axis=1, keepdims=True)
        ids.append(idx)
        vals.append(mx)
        lg = jnp.where(lane == idx, -jnp.inf, lg)
    es = [jnp.exp(v - vals[0]) for v in vals]
    den = es[0] + es[1] + es[2] + es[3]

    hot = jnp.zeros((tm, LANES), F32)
    for idx in ids:
        hot = hot + jnp.where(lane == idx, 1.0, 0.0)
    r2 = lax.broadcasted_iota(I32, (tm, tm), 0)
    c2 = lax.broadcasted_iota(I32, (tm, tm), 1)
    lower = jnp.where(c2 < r2, 1.0, 0.0).astype(BF16)
    base = run_ref[0:1, :] + _dot(lower, hot.astype(BF16))
    ids_v = jnp.zeros((tm, LANES), I32)
    rk_v = jnp.zeros((tm, LANES), I32)
    gt_v = jnp.zeros((tm, LANES), F32)
    for j in range(TOP_K):
        rank = jnp.sum(jnp.where(lane == ids[j], base, 0.0), axis=1, keepdims=True)
        ids_v = jnp.where(lane == j, ids[j], ids_v)
        rk_v = jnp.where(lane == j, rank.astype(I32), rk_v)
        gt_v = jnp.where(lane == j, es[j] / den, gt_v)
    ids_o[...] = ids_v
    rk_o[...] = rk_v
    gt_o[...] = gt_v
    run = run_ref[0:1, :] + jnp.sum(hot, axis=0, keepdims=True)
    run_ref[...] = jnp.broadcast_to(run, run_ref.shape)
    cnt_o[...] = jnp.broadcast_to(run, cnt_o.shape)


def _merge(x2, oa, ob, oc, wts, alpha):
    n = x2.shape[0]
    tm = min(MERGE_TM, n)
    assert n % tm == 0
    row = lambda w: pl.BlockSpec((tm, w), lambda i: (i, 0))
    in_specs = [row(D_MODEL), row(512), row(512), row(512),
                _const_spec((D_MODEL, N_BRANCH * D_MODEL)),
                _const_spec((N_BRANCH, BRANCH_WIDTH, D_MODEL)),
                _const_spec((D_MODEL, D_MODEL)),
                _const_spec((1, D_MODEL)), _const_spec((1, D_MODEL)),
                _const_spec((D_MODEL, LANES)), _const_spec((1, LANES))]
    out_shape = [jax.ShapeDtypeStruct((n, D_MODEL), F32),
                 jax.ShapeDtypeStruct((n, LANES), I32),
                 jax.ShapeDtypeStruct((n, LANES), I32),
                 jax.ShapeDtypeStruct((n, LANES), F32),
                 jax.ShapeDtypeStruct((8, LANES), F32)]
    out_specs = [row(D_MODEL), row(LANES), row(LANES), row(LANES), _const_spec((8, LANES))]
    return pl.pallas_call(
        functools.partial(_merge_kernel, alpha=alpha),
        out_shape=out_shape,
        grid=(n // tm,),
        in_specs=in_specs,
        out_specs=out_specs,
        scratch_shapes=[pltpu.VMEM((8, LANES), F32)],
        compiler_params=pltpu.CompilerParams(dimension_semantics=("arbitrary",),
                                             vmem_limit_bytes=VMEM_LIMIT),
        name="merge",
    )(x2, oa, ob, oc, wts["wg"], wts["wbr"], wts["wo"], wts["g1"], wts["b1"], wts["rw"], wts["rb"])


def _row_copy(src_ref, src_row, dst_ref, dst_row, sem):
    return pltpu.make_async_copy(src_ref.at[pl.ds(src_row, 1), :], dst_ref.at[pl.ds(dst_row, 1), :], sem)


def _dispatch_kernel(off_ref, ids_ref, rk_ref, x_ref, xs_in_ref, xs_ref, sem):
    del xs_in_ref
    tm = x_ref.shape[0]

    def issue(tok, carry):
        for j in range(TOP_K):
            a = tok * TOP_K + j
            _row_copy(x_ref, tok, xs_ref, off_ref[ids_ref[a]] + rk_ref[a], sem).start()
        return carry

    lax.fori_loop(0, tm, issue, 0)

    def drain(a, carry):
        _row_copy(x_ref, 0, xs_ref, 0, sem).wait()
        return carry

    lax.fori_loop(0, tm * TOP_K, drain, 0)


def _dispatch(offsets, ids_flat, rk_flat, x1, xs_init):
    n = x1.shape[0]
    tm = min(DISPATCH_TM, n)
    assert n % tm == 0
    smem = pl.BlockSpec((tm * TOP_K,), lambda i, off: (i,), memory_space=pltpu.SMEM)
    grid_spec = pltpu.PrefetchScalarGridSpec(
        num_scalar_prefetch=1,
        grid=(n // tm,),
        in_specs=[smem, smem,
                  pl.BlockSpec((tm, D_MODEL), lambda i, off: (i, 0)),
                  pl.BlockSpec(memory_space=pl.ANY)],
        out_specs=pl.BlockSpec(memory_space=pl.ANY),
        scratch_shapes=[pltpu.SemaphoreType.DMA],
    )
    return pl.pallas_call(
        _dispatch_kernel,
        out_shape=jax.ShapeDtypeStruct(xs_init.shape, xs_init.dtype),
        grid_spec=grid_spec,
        input_output_aliases={4: 0},
        compiler_params=pltpu.CompilerParams(dimension_semantics=("arbitrary",),
                                             vmem_limit_bytes=VMEM_LIMIT),
        name="dispatch",
    )(offsets, ids_flat, rk_flat, x1, xs_init)


def _expert_kernel(te_ref, nu_ref, xs_ref, wgu_ref, bgu_ref, wdn_ref, bdn_ref, y_ref,
                   wgu_b, wdn_b):
    t = pl.program_id(0)
    e = te_ref[t]
    valid = t < nu_ref[0]
    fresh = jnp.logical_or(t == 0, te_ref[jnp.maximum(t - 1, 0)] != e)

    @pl.when(jnp.logical_and(valid, fresh))
    def _():
        for r in range(0, D_MODEL, LANES):
            wgu_b[r:r + LANES, :] = wgu_ref[r:r + LANES, :].astype(BF16)
        for r in range(0, D_FF, LANES):
            wdn_b[r:r + LANES, :] = wdn_ref[r:r + LANES, :].astype(BF16)

    @pl.when(valid)
    def _():
        h = _dot(xs_ref[...].astype(BF16), wgu_b[...]) + bgu_ref[...]
        glu = jnp.minimum(h[:, :D_FF], SWIGLU_LIMIT)
        lin = jnp.clip(h[:, D_FF:], -SWIGLU_LIMIT, SWIGLU_LIMIT)
        act = (lin + 1.0) * (glu * (1.0 / (1.0 + jnp.exp(-SWIGLU_ALPHA * glu))))
        y_ref[...] = _dot(act.astype(BF16), wdn_b[...]) + bdn_ref[...]

    @pl.when(jnp.logical_not(valid))
    def _():
        y_ref[...] = jnp.zeros_like(y_ref)


def _experts(tile_e, n_used, xs, wgu, bgu, wdn, bdn, layer):
    p = xs.shape[0]
    tg = EXPERT_TG
    n_tiles = p // tg
    wspec = lambda r, c: pl.BlockSpec((None, None, r, c), lambda t, te, nu: (layer, te[t], 0, 0))
    grid_spec = pltpu.PrefetchScalarGridSpec(
        num_scalar_prefetch=2,
        grid=(n_tiles,),
        in_specs=[pl.BlockSpec((tg, D_MODEL), lambda t, te, nu: (t, 0)),
                  wspec(D_MODEL, 2 * D_FF), wspec(1, 2 * D_FF), wspec(D_FF, D_MODEL), wspec(1, D_MODEL)],
        out_specs=pl.BlockSpec((tg, D_MODEL), lambda t, te, nu: (t, 0)),
        scratch_shapes=[pltpu.VMEM((D_MODEL, 2 * D_FF), BF16), pltpu.VMEM((D_FF, D_MODEL), BF16)],
    )
    return pl.pallas_call(
        _expert_kernel,
        out_shape=jax.ShapeDtypeStruct((p, D_MODEL), F32),
        grid_spec=grid_spec,
        compiler_params=pltpu.CompilerParams(dimension_semantics=("arbitrary",),
                                             vmem_limit_bytes=VMEM_LIMIT),
        name="experts",
    )(tile_e, n_used, xs, wgu, bgu, wdn, bdn)


def _combine_kernel(off_ref, ids_ref, rk_ref, x1_ref, gt_ref, g2_ref, b2_ref, y_ref, o_ref,
                    buf, sem, *, alpha):
    tm = x1_ref.shape[0]

    def issue(tok, carry):
        for j in range(TOP_K):
            a = tok * TOP_K + j
            _row_copy(y_ref, off_ref[ids_ref[a]] + rk_ref[a], buf.at[j], tok, sem).start()
        return carry

    lax.fori_loop(0, tm, issue, 0)

    def drain(a, carry):
        _row_copy(y_ref, 0, buf.at[0], 0, sem).wait()
        return carry

    lax.fori_loop(0, tm * TOP_K, drain, 0)

    f = jnp.zeros((tm, D_MODEL), F32)
    for j in range(TOP_K):
        f = f + gt_ref[:, j:j + 1] * buf[j]
    o_ref[...] = _layer_norm(alpha * x1_ref[...] + f, g2_ref[...], b2_ref[...])


def _combine(offsets, ids_flat, rk_flat, x1, gates, g2, b2, y, alpha):
    n = x1.shape[0]
    tm = min(COMBINE_TM, n)
    assert n % tm == 0
    smem = pl.BlockSpec((tm * TOP_K,), lambda i, off: (i,), memory_space=pltpu.SMEM)
    grid_spec = pltpu.PrefetchScalarGridSpec(
        num_scalar_prefetch=1,
        grid=(n // tm,),
        in_specs=[smem, smem,
                  pl.BlockSpec((tm, D_MODEL), lambda i, off: (i, 0)),
                  pl.BlockSpec((tm, LANES), lambda i, off: (i, 0)),
                  pl.BlockSpec((1, D_MODEL), lambda i, off: (0, 0)),
                  pl.BlockSpec((1, D_MODEL), lambda i, off: (0, 0)),
                  pl.BlockSpec(memory_space=pl.ANY)],
        out_specs=pl.BlockSpec((tm, D_MODEL), lambda i, off: (i, 0)),
        scratch_shapes=[pltpu.VMEM((TOP_K, tm, D_MODEL), F32), pltpu.SemaphoreType.DMA],
    )
    return pl.pallas_call(
        functools.partial(_combine_kernel, alpha=alpha),
        out_shape=jax.ShapeDtypeStruct((n, D_MODEL), F32),
        grid_spec=grid_spec,
        compiler_params=pltpu.CompilerParams(dimension_semantics=("arbitrary",),
                                             vmem_limit_bytes=VMEM_LIMIT),
        name="combine",
    )(offsets, ids_flat, rk_flat, x1, gates, g2, b2, y)


def _rope_tables(positions):
    pos = positions.reshape(-1).astype(F32)[:, None]
    n = pos.shape[0]

    def cs(rot_dim):
        half = rot_dim // 2
        inv_freq = ROPE_THETA ** (-jnp.arange(half, dtype=F32) / half)
        ang = pos * inv_freq
        return jnp.cos(ang), jnp.sin(ang)

    cos_p, sin_p = cs(A_ROT_DIM)
    cos_c, sin_c = cs(C_ROPE_DIM)
    hp, hc = A_ROT_DIM // 2, C_ROPE_DIM // 2
    one = lambda w: jnp.ones((n, w), F32)
    zero = lambda w: jnp.zeros((n, w), F32)

    a64 = jnp.concatenate([cos_p, cos_p, one(64 - 2 * hp)] * 2, axis=1)
    bm64 = jnp.concatenate([-sin_p, zero(64 - hp)] * 2, axis=1)
    bp64 = jnp.concatenate([zero(hp), sin_p, zero(64 - 2 * hp)] * 2, axis=1)
    a128 = jnp.concatenate([one(64), cos_c, cos_c, one(32)], axis=1)
    bm128 = jnp.concatenate([zero(64), -sin_c, zero(64 - hc)], axis=1)
    bp128 = jnp.concatenate([zero(64 + hc), sin_c, zero(32)], axis=1)
    iw_scale = jnp.full((n, IDX_HEADS), IDX_HEADS ** -0.5, F32)
    as2 = jnp.concatenate([iw_scale, zero(64 - IDX_HEADS), cos_c, cos_c, zero(32)], axis=1)
    return {"t64": jnp.stack([a64, bm64, bp64]),
            "t128": jnp.stack([a128, bm128, bp128]),
            "ts2": jnp.stack([as2, bm128, bp128])}


def _layer_weights(l, w_in, gm_norm_g, gm_norm_b, gm_w_s, gm_b_s, mla_q_norm, mla_kv_norm,
                   mla_w_uq, mla_w_ukv, w_branch, w_out, ln1_g, ln1_b, router_w, router_b):
    w = w_in[l]
    o = 0
    cols = {}
    for name, width in (("aq", 512), ("ak", 512), ("av", 512), ("iq", 512), ("ik", 64), ("iw", 8),
                        ("buv", 1024), ("cdq", 256), ("cdkv", 128), ("ckr", 32), ("gate", 3072)):
        cols[name] = w[:, o:o + width]
        o += width
    zc = lambda width: jnp.zeros((D_MODEL, width), F32)
    s2 = jnp.concatenate([cols["iw"], zc(64 - IDX_HEADS), cols["ckr"], zc(32)], axis=1)
    wp = jnp.concatenate([cols["aq"] * (A_HEAD_DIM ** -0.5), cols["ak"],
                          cols["iq"] * (IDX_HEAD_DIM ** -0.5), cols["av"],
                          cols["ik"], cols["ik"], s2, cols["buv"], cols["cdq"], cols["cdkv"]], axis=1)

    uq = mla_w_uq[l].reshape(C_Q_RANK, C_HEADS, C_NOPE_DIM + C_ROPE_DIM)
    wuq = jnp.pad(uq, ((0, 0), (0, 0), (0, LANES - C_NOPE_DIM - C_ROPE_DIM))).reshape(C_Q_RANK, -1)
    ukv = mla_w_ukv[l].reshape(C_KV_RANK, C_HEADS, C_NOPE_DIM + C_V_DIM)
    wk = jnp.pad(ukv[:, :, :C_NOPE_DIM], ((0, 0), (0, 0), (0, LANES - C_NOPE_DIM))).reshape(C_KV_RANK, -1)
    wv = ukv[:, :, C_NOPE_DIM:].reshape(C_KV_RANK, -1)
    src = jnp.arange(LANES)[:, None]
    dst = jnp.arange(C_HEADS * LANES)[None, :]
    place = ((dst % LANES == src) & (src >= C_NOPE_DIM) & (src < C_NOPE_DIM + C_ROPE_DIM)).astype(F32)
    wke = jnp.concatenate([wk, place], axis=0)

    tril = jnp.tril(jnp.ones((B_CHUNK, B_CHUNK), dtype=bool))
    ws = jnp.where(tril[None], gm_w_s[l], 0)
    bsf = jnp.repeat(gm_b_s[l].T, B_GROUP_DIM, axis=1)
    rw = jnp.pad(router_w[l], ((0, 0), (0, LANES - N_EXPERTS)))
    rb = jnp.pad(router_b[l], (0, LANES - N_EXPERTS))[None, :]
    return {
        "wp": wp.astype(BF16), "wuq": wuq.astype(BF16), "wke": wke.astype(BF16), "wv": wv.astype(BF16),
        "ws": ws.astype(BF16), "gmg": gm_norm_g[l][None, :], "gmb": gm_norm_b[l][None, :], "bsf": bsf,
        "qn": mla_q_norm[l][None, :], "kvn": mla_kv_norm[l][None, :],
        "wg": cols["gate"].astype(BF16), "wbr": w_branch[l].astype(BF16), "wo": w_out[l].astype(BF16),
        "g1": ln1_g[l][None, :], "b1": ln1_b[l][None, :], "rw": rw, "rb": rb,
    }


def _expert_tiles(counts, n_tiles):
    tg = EXPERT_TG
    cnt = counts[0, :N_EXPERTS].astype(I32)
    tiles_per = (cnt + tg - 1) // tg
    tile_end = jnp.cumsum(tiles_per)
    offsets = (tile_end - tiles_per) * tg
    n_used = tile_end[-1]
    tid = jnp.minimum(jnp.arange(n_tiles, dtype=I32), n_used - 1)
    tile_e = jnp.sum(tid[:, None] >= tile_end[None, :], axis=1).astype(I32)
    off128 = jnp.pad(offsets, (0, LANES - N_EXPERTS)).astype(I32)
    return off128, tile_e, n_used.reshape(1).astype(I32)


def _layer(l, x2, tabs, batch, seq, depth, p):
    alpha = (2 * depth) ** 0.25
    n = x2.shape[0]
    wts = _layer_weights(l, p["w_in"], p["gm_norm_g"], p["gm_norm_b"], p["gm_w_s"], p["gm_b_s"],
                         p["mla_q_norm"], p["mla_kv_norm"], p["mla_w_uq"], p["mla_w_ukv"],
                         p["w_branch"], p["w_out"], p["ln1_g"], p["ln1_b"], p["router_w"], p["router_b"])
    aq, ak, iq, av, ik2, sm2, o_b, q, k, v = _proj(x2, wts, tabs)
    o_a = _dsa(iq, sm2, aq, ik2, ak, av, batch, seq)
    o_c = _mla(q, k, v, batch, seq)
    x1, ids, ranks, gates, counts = _merge(x2, o_a, o_b, o_c, wts, alpha)

    n_tiles = n * TOP_K // EXPERT_TG + N_EXPERTS
    offsets, tile_e, n_used = _expert_tiles(counts, n_tiles)
    ids_flat = ids[:, :TOP_K].reshape(-1)
    rk_flat = ranks[:, :TOP_K].reshape(-1)
    xs = _dispatch(offsets, ids_flat, rk_flat, x1, jnp.zeros((n_tiles * EXPERT_TG, D_MODEL), F32))
    y = _experts(tile_e, n_used, xs, p["exp_w_gu"], p["exp_b_gu"][:, :, None, :],
                 p["exp_w_dn"], p["exp_b_dn"][:, :, None, :], l)
    return _combine(offsets, ids_flat, rk_flat, x1, gates, p["ln2_g"][l][None, :],
                    p["ln2_b"][l][None, :], y, alpha)


def kernel(x, positions, w_in, gm_norm_g, gm_norm_b, gm_w_s, gm_b_s, mla_q_norm, mla_kv_norm, mla_w_uq, mla_w_ukv, w_branch, w_out, ln1_g, ln1_b, router_w, router_b, exp_w_gu, exp_b_gu, exp_w_dn, exp_b_dn, ln2_g, ln2_b):
    batch, seq, _ = x.shape
    depth = w_in.shape[0]
    p = dict(w_in=w_in, gm_norm_g=gm_norm_g, gm_norm_b=gm_norm_b, gm_w_s=gm_w_s, gm_b_s=gm_b_s,
             mla_q_norm=mla_q_norm, mla_kv_norm=mla_kv_norm, mla_w_uq=mla_w_uq, mla_w_ukv=mla_w_ukv,
             w_branch=w_branch, w_out=w_out, ln1_g=ln1_g, ln1_b=ln1_b, router_w=router_w,
             router_b=router_b, exp_w_gu=exp_w_gu, exp_b_gu=exp_b_gu, exp_w_dn=exp_w_dn,
             exp_b_dn=exp_b_dn, ln2_g=ln2_g, ln2_b=ln2_b)
    tabs = _rope_tables(positions)
    x2 = x.reshape(batch * seq, D_MODEL)
    for l in range(depth):
        x2 = _layer(l, x2, tabs, batch, seq, depth, p)
    return x2.reshape(batch, seq, D_MODEL)
```

```python
import functools
import math

import jax
import jax.numpy as jnp
from jax import lax
from jax.experimental import pallas as pl
from jax.experimental.pallas import tpu as pltpu

F32 = jnp.float32
BF16 = jnp.bfloat16
I32 = jnp.int32

D_MODEL = 1024
BRANCH_WIDTH = 512
N_BRANCH = 3
ROPE_THETA = 500000.0
LN_EPS = 1e-5
RMS_EPS = 1e-6

A_HEADS = 8
A_HEAD_DIM = 64
A_ROT_DIM = 16
IDX_HEADS = 8
IDX_HEAD_DIM = 64
IDX_TOPK_MAX = 256

B_GROUPS = 8
B_GROUP_DIM = 64
B_CHUNK = 128

C_HEADS = 8
C_NOPE_DIM = 64
C_ROPE_DIM = 32
C_V_DIM = 64
C_Q_RANK = 256
C_KV_RANK = 128

N_EXPERTS = 32
TOP_K = 4
D_FF = 1024
SWIGLU_LIMIT = 7.0
SWIGLU_ALPHA = 1.702

LANES = 128
ATT_T = 512
PROJ_TM = 512
MERGE_TM = 256
DISPATCH_TM = 256
COMBINE_TM = 128
EXPERT_TG = 256
VMEM_LIMIT = 56 * 1024 * 1024

FLASH_LOOKAHEAD = 2
NEG = -1e30
INT_MIN = -(2 ** 31)

_C_AQ, _C_AK, _C_IQ, _C_AV = 0, 512, 1024, 1536
_C_S1, _C_S2, _C_B, _C_C, _C_END = 2048, 2176, 2304, 3328, 3712


def _dot(a, b):
    return jnp.dot(a, b, preferred_element_type=F32)


def _dot_nt(a, b):
    return lax.dot_general(a, b, (((1,), (1,)), ((), ())), preferred_element_type=F32)


def _rot(y, a, bm, bp, r):
    w = y.shape[1]
    return y * a + pltpu.roll(y, w - r, 1) * bm + pltpu.roll(y, r, 1) * bp


def _layer_norm(x, g, b):
    mu = jnp.mean(x, axis=-1, keepdims=True)
    xc = x - mu
    var = jnp.mean(xc * xc, axis=-1, keepdims=True)
    return xc * lax.rsqrt(var + LN_EPS) * g + b


def _rms_norm(x, g):
    return x * lax.rsqrt(jnp.mean(x * x, axis=-1, keepdims=True) + RMS_EPS) * g


def _const_spec(shape):
    nd = len(shape)
    return pl.BlockSpec(shape, lambda *_: (0,) * nd)


def _proj_kernel(x_ref, wp_ref, wuq_ref, wke_ref, wv_ref, ws_ref, gmg_ref, gmb_ref, bsf_ref,
                 qn_ref, kvn_ref, t64_ref, t128_ref, ts2_ref,
                 aq_o, ak_o, iq_o, avt_o, ik2_o, iwt_o, ob_o, q_o, k_o, vt_o):
    tm = x_ref.shape[0]
    xb = x_ref[...].astype(BF16)

    def mm(c0, c1):
        return _dot(xb, wp_ref[:, c0:c1])

    a64, bm64, bp64 = t64_ref[0], t64_ref[1], t64_ref[2]
    a4 = jnp.concatenate([a64] * 4, axis=1)
    bm4 = jnp.concatenate([bm64] * 4, axis=1)
    bp4 = jnp.concatenate([bp64] * 4, axis=1)
    for c0, out in ((_C_AQ, aq_o), (_C_AK, ak_o), (_C_IQ, iq_o)):
        out[...] = _rot(mm(c0, c0 + 512), a4, bm4, bp4, A_ROT_DIM // 2).astype(BF16)
    avt_o[...] = mm(_C_AV, _C_AV + 512).T.astype(BF16)
    ik2_o[...] = _rot(mm(_C_S1, _C_S1 + LANES), a64, bm64, bp64, A_ROT_DIM // 2).astype(BF16)
    sm2 = _rot(mm(_C_S2, _C_S2 + LANES), ts2_ref[0], ts2_ref[1], ts2_ref[2], C_ROPE_DIM // 2)
    iwt_o[...] = sm2.T[:IDX_HEADS, :]

    buv = mm(_C_B, _C_B + 2 * BRANCH_WIDTH)
    buv = 0.5 * buv * (1.0 + lax.erf(buv * (1.0 / math.sqrt(2.0))))
    u = buv[:, :BRANCH_WIDTH]
    v = _layer_norm(buv[:, BRANCH_WIDTH:], gmg_ref[...], gmb_ref[...]).astype(BF16)
    lane = lax.broadcasted_iota(I32, (B_CHUNK, LANES), 1)
    for c in range(tm // B_CHUNK):
        rows = slice(c * B_CHUNK, (c + 1) * B_CHUNK)
        for pr in range(B_GROUPS // 2):
            cols = slice(pr * LANES, (pr + 1) * LANES)
            vp = v[rows, cols]
            s = jnp.where(lane < B_GROUP_DIM, _dot(ws_ref[2 * pr], vp), _dot(ws_ref[2 * pr + 1], vp))
            ob_o[rows, cols] = (u[rows, cols] * (s + bsf_ref[:, cols])).astype(BF16)

    cc = mm(_C_C, _C_END)
    cq = _rms_norm(cc[:, :C_Q_RANK], qn_ref[...]).astype(BF16)
    a8 = jnp.concatenate([t128_ref[0]] * C_HEADS, axis=1)
    bm8 = jnp.concatenate([t128_ref[1]] * C_HEADS, axis=1)
    bp8 = jnp.concatenate([t128_ref[2]] * C_HEADS, axis=1)
    q_o[...] = _rot(_dot(cq, wuq_ref[...]), a8, bm8, bp8, C_ROPE_DIM // 2).astype(BF16)
    ckv = _rms_norm(cc[:, C_Q_RANK:], kvn_ref[...]).astype(BF16)
    kin = jnp.concatenate([ckv, sm2.astype(BF16)], axis=1)
    k_o[...] = _dot(kin, wke_ref[...]).astype(BF16)
    vt_o[...] = _dot(ckv, wv_ref[...]).T.astype(BF16)


def _proj(x2, wts, tabs):
    n = x2.shape[0]
    tm = min(PROJ_TM, n)
    assert n % tm == 0 and tm % B_CHUNK == 0
    row = lambda w: pl.BlockSpec((tm, w), lambda i: (i, 0))
    col = lambda h: pl.BlockSpec((h, tm), lambda i: (0, i))
    tab = pl.BlockSpec((3, tm, LANES), lambda i: (0, i, 0))
    in_specs = [
        row(D_MODEL),
        _const_spec((D_MODEL, _C_END)),
        _const_spec((C_Q_RANK, C_HEADS * LANES)),
        _const_spec((C_KV_RANK + LANES, C_HEADS * LANES)),
        _const_spec((C_KV_RANK, C_HEADS * C_V_DIM)),
        _const_spec((B_GROUPS, B_CHUNK, B_CHUNK)),
        _const_spec((1, BRANCH_WIDTH)),
        _const_spec((1, BRANCH_WIDTH)),
        _const_spec((B_CHUNK, BRANCH_WIDTH)),
        _const_spec((1, C_Q_RANK)),
        _const_spec((1, C_KV_RANK)),
        tab, tab, tab,
    ]
    out_shape = [
        jax.ShapeDtypeStruct((n, 512), BF16),
        jax.ShapeDtypeStruct((n, 512), BF16),
        jax.ShapeDtypeStruct((n, 512), BF16),
        jax.ShapeDtypeStruct((512, n), BF16),
        jax.ShapeDtypeStruct((n, LANES), BF16),
        jax.ShapeDtypeStruct((IDX_HEADS, n), F32),
        jax.ShapeDtypeStruct((n, 512), BF16),
        jax.ShapeDtypeStruct((n, C_HEADS * LANES), BF16),
        jax.ShapeDtypeStruct((n, C_HEADS * LANES), BF16),
        jax.ShapeDtypeStruct((C_HEADS * C_V_DIM, n), BF16),
    ]
    out_specs = [row(512), row(512), row(512), col(512), row(LANES), col(IDX_HEADS), row(512),
                 row(C_HEADS * LANES), row(C_HEADS * LANES), col(C_HEADS * C_V_DIM)]
    return pl.pallas_call(
        _proj_kernel,
        out_shape=out_shape,
        grid=(n // tm,),
        in_specs=in_specs,
        out_specs=out_specs,
        compiler_params=pltpu.CompilerParams(dimension_semantics=("arbitrary",),
                                             vmem_limit_bytes=VMEM_LIMIT),
        name="proj",
    )(x2, wts["wp"], wts["wuq"], wts["wke"], wts["wv"], wts["ws"], wts["gmg"], wts["gmb"],
      wts["bsf"], wts["qn"], wts["kvn"], tabs["t64"], tabs["t128"], tabs["ts2"])


def _rep8(x):
    return jnp.broadcast_to(x, (8, x.shape[-1]))


def _fold8(x3, op):
    return _rep8(op(op(x3, axis=0), axis=0, keepdims=True))


def _split8(x):
    return x.reshape(x.shape[0] // 8, 8, x.shape[1])


def _flash_init(m_ref, l_ref, acc_ref):
    m_ref[...] = jnp.full(m_ref.shape, NEG, F32)
    l_ref[...] = jnp.zeros(l_ref.shape, F32)
    acc_ref[...] = jnp.zeros(acc_ref.shape, F32)


def _flash_update(s3, vt, h, m_ref, l_ref, acc_ref, c2):
    n, _, tq = s3.shape
    m_old = m_ref[h]
    m_new = jnp.maximum(m_old, _fold8(s3, jnp.max))
    alpha = jnp.exp2((m_old - m_new) * c2)
    p = jnp.exp2((s3 - m_new[None]) * c2)
    m_ref[h] = m_new
    l_ref[h] = alpha * l_ref[h] + _fold8(p, jnp.sum)
    pv = _dot(vt, p.reshape(n * 8, tq).astype(BF16))
    acc_ref[h] = (_split8(acc_ref[h]) * alpha[None]).reshape(pv.shape) + pv


def _flash_chunk(logits, values, heads, m_ref, l_ref, acc_ref, c2):
    s = {h: logits(h) for h in range(min(FLASH_LOOKAHEAD, heads))}
    for h in range(heads):
        if h + FLASH_LOOKAHEAD < heads:
            s[h + FLASH_LOOKAHEAD] = logits(h + FLASH_LOOKAHEAD)
        _flash_update(s.pop(h), values(h), h, m_ref, l_ref, acc_ref, c2)


def _flash_finish(o_ref, l_ref, acc_ref, heads):
    outs = [(_split8(acc_ref[h]) / l_ref[h][None]).reshape(acc_ref.shape[1:]) for h in range(heads)]
    o_ref[...] = jnp.concatenate(outs, axis=0).T.astype(BF16)


def _dsa_kernel(iq_ref, iwt_ref, aq_ref, ik2_ref, ak_ref, avt_ref, o_ref,
                key_ref, iqm_ref, aqm_ref, thr_ref, m_ref, l_ref, acc_ref, *, n_sel):
    t = iq_ref.shape[0]
    i = pl.program_id(1)
    nchunk = i + 1
    lane = lax.broadcasted_iota(I32, (t, LANES), 1)
    lo = lane < IDX_HEAD_DIM
    krow = lax.broadcasted_iota(I32, (t, t), 0)
    qcol = lax.broadcasted_iota(I32, (t, t), 1)

    for h in range(IDX_HEADS):
        cols = slice((h // 2) * LANES, (h // 2 + 1) * LANES)
        keep = lo if h % 2 == 0 else jnp.logical_not(lo)
        iqm_ref[h] = jnp.where(keep, iq_ref[:, cols], jnp.zeros((t, LANES), BF16))
        aqm_ref[h] = jnp.where(keep, aq_ref[:, cols], jnp.zeros((t, LANES), BF16))

    def rows(c):
        return pl.ds(pl.multiple_of(c * t, t), t)

    def causal_at(c):
        return (krow + c * t) <= (qcol + i * t)

    def idx_body(c, carry):
        kc = ik2_ref[rows(c), :]
        acc = jnp.zeros((t // 8, 8, t), F32)
        for h in range(IDX_HEADS):
            w8 = _rep8(iwt_ref[h:h + 1, :])
            acc = acc + _split8(jnp.maximum(_dot_nt(kc, iqm_ref[h]), 0.0)) * w8[None]
        bits = lax.bitcast_convert_type(acc.reshape(t, t), I32)
        key = bits ^ ((bits >> 31) & 0x7FFFFFFF)
        key_ref[rows(c), :] = jnp.where(causal_at(c), key, INT_MIN)
        return carry

    lax.fori_loop(0, nchunk, idx_body, 0)

    def count(indicator):
        def body(c, acc):
            return acc + jnp.sum(indicator(_split8(key_ref[rows(c), :])), axis=0)
        part = lax.fori_loop(0, nchunk, body, jnp.zeros((8, t), I32))
        return _rep8(jnp.sum(part, axis=0, keepdims=True))

    def bit_body(it, prefix):
        trial_u = prefix | jnp.left_shift(jnp.int32(1), 31 - it)
        trial_s = trial_u ^ INT_MIN
        cnt = count(lambda kc: jnp.where(kc >= trial_s[None], 1, 0))
        return jnp.where(cnt >= n_sel, trial_u, prefix)

    thr = lax.fori_loop(0, 32, bit_body, jnp.zeros((8, t), I32)) ^ INT_MIN
    cnt_ge = count(lambda kc: jnp.where(kc >= thr[None], 1, 0))
    tied = jnp.where(cnt_ge > n_sel, jnp.where(thr != INT_MIN, 1, 0), 0)

    @pl.when(jnp.max(tied) > 0)
    def _():
        need = (n_sel - count(lambda kc: jnp.where(kc > thr[None], 1, 0))).astype(F32)
        lower = jnp.where(qcol < krow, 1.0, 0.0).astype(BF16)

        def body(c, seen):
            kc = _split8(key_ref[rows(c), :])
            eq = jnp.where(kc == thr[None], 1.0, 0.0)
            rank = _split8(_dot(lower, eq.reshape(t, t).astype(BF16))) + seen[None]
            drop = jnp.where(rank >= need[None], eq, 0.0)
            key_ref[rows(c), :] = jnp.where(drop > 0.0, INT_MIN, kc).reshape(t, t)
            return seen + _fold8(eq, jnp.sum)

        lax.fori_loop(0, nchunk, body, jnp.zeros((8, t), F32))

    thr_ref[...] = jnp.maximum(thr, INT_MIN + 1)

    _flash_init(m_ref, l_ref, acc_ref)
    log2e = math.log2(math.e)

    def att_body(c, carry):
        sel = _split8(key_ref[rows(c), :]) >= thr_ref[...][None]

        def logits(h):
            cols = slice((h // 2) * LANES, (h // 2 + 1) * LANES)
            return jnp.where(sel, _split8(_dot_nt(ak_ref[rows(c), cols], aqm_ref[h])), NEG)

        def values(h):
            return avt_ref[h * A_HEAD_DIM:(h + 1) * A_HEAD_DIM, rows(c)]

        _flash_chunk(logits, values, A_HEADS, m_ref, l_ref, acc_ref, log2e)
        return carry

    lax.fori_loop(0, nchunk, att_body, 0)
    _flash_finish(o_ref, l_ref, acc_ref, A_HEADS)


def _dsa(iq, iwt, aq, ik2, ak, avt, batch, seq):
    t = min(ATT_T, seq)
    nq = seq // t
    n_sel = min(IDX_TOPK_MAX, seq // 4)
    assert seq % t == 0 and t % LANES == 0
    qspec = lambda w: pl.BlockSpec((t, w), lambda b, i: (b * nq + i, 0))
    kspec = lambda w: pl.BlockSpec((seq, w), lambda b, i: (b, 0))
    return pl.pallas_call(
        functools.partial(_dsa_kernel, n_sel=n_sel),
        out_shape=jax.ShapeDtypeStruct((batch * seq, 512), BF16),
        grid=(batch, nq),
        in_specs=[qspec(512),
                  pl.BlockSpec((IDX_HEADS, t), lambda b, i: (0, b * nq + i)),
                  qspec(512), kspec(LANES), kspec(512),
                  pl.BlockSpec((512, seq), lambda b, i: (0, b))],
        out_specs=qspec(512),
        scratch_shapes=[
            pltpu.VMEM((seq, t), I32),
            pltpu.VMEM((IDX_HEADS, t, LANES), BF16),
            pltpu.VMEM((A_HEADS, t, LANES), BF16),
            pltpu.VMEM((8, t), I32),
            pltpu.VMEM((A_HEADS, 8, t), F32),
            pltpu.VMEM((A_HEADS, 8, t), F32),
            pltpu.VMEM((A_HEADS, A_HEAD_DIM, t), F32),
        ],
        compiler_params=pltpu.CompilerParams(dimension_semantics=("arbitrary", "arbitrary"),
                                             vmem_limit_bytes=VMEM_LIMIT),
        name="dsa",
    )(iq, iwt, aq, ik2, ak, avt)


def _mla_kernel(q_ref, k_ref, vt_ref, o_ref, m_ref, l_ref, acc_ref):
    t = q_ref.shape[0]
    i = pl.program_id(1)
    krow = lax.broadcasted_iota(I32, (t // 8, 8, t), 0) * 8 + lax.broadcasted_iota(I32, (t // 8, 8, t), 1)
    qcol = lax.broadcasted_iota(I32, (t // 8, 8, t), 2)
    c2 = (C_NOPE_DIM + C_ROPE_DIM) ** -0.5 * math.log2(math.e)
    _flash_init(m_ref, l_ref, acc_ref)

    def chunk(c, masked):
        rows = pl.ds(pl.multiple_of(c * t, t), t)

        def logits(h):
            cols = slice(h * LANES, (h + 1) * LANES)
            s3 = _split8(_dot_nt(k_ref[rows, cols], q_ref[:, cols]))
            return jnp.where(krow <= qcol, s3, NEG) if masked else s3

        def values(h):
            return vt_ref[h * C_V_DIM:(h + 1) * C_V_DIM, rows]

        _flash_chunk(logits, values, C_HEADS, m_ref, l_ref, acc_ref, c2)

    def body(c, carry):
        chunk(c, False)
        return carry

    lax.fori_loop(0, i, body, 0)
    chunk(i, True)
    _flash_finish(o_ref, l_ref, acc_ref, C_HEADS)


def _mla(q, k, vt, batch, seq):
    t = min(ATT_T, seq)
    nq = seq // t
    return pl.pallas_call(
        _mla_kernel,
        out_shape=jax.ShapeDtypeStruct((batch * seq, C_HEADS * C_V_DIM), BF16),
        grid=(batch, nq),
        in_specs=[pl.BlockSpec((t, C_HEADS * LANES), lambda b, i: (b * nq + i, 0)),
                  pl.BlockSpec((seq, C_HEADS * LANES), lambda b, i: (b, 0)),
                  pl.BlockSpec((C_HEADS * C_V_DIM, seq), lambda b, i: (0, b))],
        out_specs=pl.BlockSpec((t, C_HEADS * C_V_DIM), lambda b, i: (b * nq + i, 0)),
        scratch_shapes=[pltpu.VMEM((C_HEADS, 8, t), F32),
                        pltpu.VMEM((C_HEADS, 8, t), F32),
                        pltpu.VMEM((C_HEADS, C_V_DIM, t), F32)],
        compiler_params=pltpu.CompilerParams(dimension_semantics=("arbitrary", "arbitrary"),
                                             vmem_limit_bytes=VMEM_LIMIT),
        name="mla",
    )(q, k, vt)


def _merge_kernel(x_ref, oa_ref, ob_ref, oc_ref, wg_ref, wbr_ref, wo_ref, g1_ref, b1_ref,
                  rw_ref, rb_ref, x1_o, ids_o, rk_o, gt_o, cnt_o, run_ref, *, alpha):
    tm = x_ref.shape[0]

    @pl.when(pl.program_id(0) == 0)
    def _():
        run_ref[...] = jnp.zeros_like(run_ref)

    x = x_ref[...]
    xb = x.astype(BF16)
    merged = jnp.zeros((tm, D_MODEL), F32)
    for n, o_ref in enumerate((oa_ref, ob_ref, oc_ref)):
        z = _dot(xb, wg_ref[:, n * D_MODEL:(n + 1) * D_MODEL])
        merged = merged + (1.0 / (1.0 + jnp.exp(-z))) * _dot(o_ref[...], wbr_ref[n])
    y = _dot(merged.astype(BF16), wo_ref[...])
    x1 = _layer_norm(alpha * x + y, g1_ref[...], b1_ref[...])
    x1_o[...] = x1

    lane = lax.broadcasted_iota(I32, (tm, LANES), 1)
    logits = jnp.dot(x1, rw_ref[...], precision=lax.Precision.HIGHEST,
                     preferred_element_type=F32) + rb_ref[...]
    lg = jnp.where(lane < N_EXPERTS, logits, -jnp.inf)
    ids, vals = [], []
    for _ in range(TOP_K):
        mx = jnp.max(lg, axis=1, keepdims=True)
        idx = jnp.min(jnp.where(lg == mx, lane, LANES), axis=1, keepdims=True)
        ids.append(idx)
        vals.append(mx)
        lg = jnp.where(lane == idx, -jnp.inf, lg)
    es = [jnp.exp(v - vals[0]) for v in vals]
    den = es[0] + es[1] + es[2] + es[3]

    hot = jnp.zeros((tm, LANES), F32)
    for idx in ids:
        hot = hot + jnp.where(lane == idx, 1.0, 0.0)
    r2 = lax.broadcasted_iota(I32, (tm, tm), 0)
    c2 = lax.broadcasted_iota(I32, (tm, tm), 1)
    lower = jnp.where(c2 < r2, 1.0, 0.0).astype(BF16)
    base = run_ref[0:1, :] + _dot(lower, hot.astype(BF16))
    ids_v = jnp.zeros((tm, LANES), I32)
    rk_v = jnp.zeros((tm, LANES), I32)
    gt_v = jnp.zeros((tm, LANES), F32)
    for j in range(TOP_K):
        rank = jnp.sum(jnp.where(lane == ids[j], base, 0.0), axis=1, keepdims=True)
        ids_v = jnp.where(lane == j, ids[j], ids_v)
        rk_v = jnp.where(lane == j, rank.astype(I32), rk_v)
        gt_v = jnp.where(lane == j, es[j] / den, gt_v)
    ids_o[...] = ids_v
    rk_o[...] = rk_v
    gt_o[...] = gt_v
    run = run_ref[0:1, :] + jnp.sum(hot, axis=0, keepdims=True)
    run_ref[...] = jnp.broadcast_to(run, run_ref.shape)
    cnt_o[...] = jnp.broadcast_to(run, cnt_o.shape)


def _merge(x2, oa, ob, oc, wts, alpha):
    n = x2.shape[0]
    tm = min(MERGE_TM, n)
    assert n % tm == 0
    row = lambda w: pl.BlockSpec((tm, w), lambda i: (i, 0))
    in_specs = [row(D_MODEL), row(512), row(512), row(512),
                _const_spec((D_MODEL, N_BRANCH * D_MODEL)),
                _const_spec((N_BRANCH, BRANCH_WIDTH, D_MODEL)),
                _const_spec((D_MODEL, D_MODEL)),
                _const_spec((1, D_MODEL)), _const_spec((1, D_MODEL)),
                _const_spec((D_MODEL, LANES)), _const_spec((1, LANES))]
    out_shape = [jax.ShapeDtypeStruct((n, D_MODEL), F32),
                 jax.ShapeDtypeStruct((n, LANES), I32),
                 jax.ShapeDtypeStruct((n, LANES), I32),
                 jax.ShapeDtypeStruct((n, LANES), F32),
                 jax.ShapeDtypeStruct((8, LANES), F32)]
    out_specs = [row(D_MODEL), row(LANES), row(LANES), row(LANES), _const_spec((8, LANES))]
    return pl.pallas_call(
        functools.partial(_merge_kernel, alpha=alpha),
        out_shape=out_shape,
        grid=(n // tm,),
        in_specs=in_specs,
        out_specs=out_specs,
        scratch_shapes=[pltpu.VMEM((8, LANES), F32)],
        compiler_params=pltpu.CompilerParams(dimension_semantics=("arbitrary",),
                                             vmem_limit_bytes=VMEM_LIMIT),
        name="merge",
    )(x2, oa, ob, oc, wts["wg"], wts["wbr"], wts["wo"], wts["g1"], wts["b1"], wts["rw"], wts["rb"])


def _row_copy(src_ref, src_row, dst_ref, dst_row, sem):
    return pltpu.make_async_copy(src_ref.at[pl.ds(src_row, 1), :], dst_ref.at[pl.ds(dst_row, 1), :], sem)


def _dispatch_kernel(off_ref, ids_ref, rk_ref, x_ref, xs_in_ref, xs_ref, sem):
    del xs_in_ref
    tm = x_ref.shape[0]

    def issue(tok, carry):
        for j in range(TOP_K):
            a = tok * TOP_K + j
            _row_copy(x_ref, tok, xs_ref, off_ref[ids_ref[a]] + rk_ref[a], sem).start()
        return carry

    lax.fori_loop(0, tm, issue, 0)

    def drain(a, carry):
        _row_copy(x_ref, 0, xs_ref, 0, sem).wait()
        return carry

    lax.fori_loop(0, tm * TOP_K, drain, 0)


def _dispatch(offsets, ids_flat, rk_flat, x1, xs_init):
    n = x1.shape[0]
    tm = min(DISPATCH_TM, n)
    assert n % tm == 0
    smem = pl.BlockSpec((tm * TOP_K,), lambda i, off: (i,), memory_space=pltpu.SMEM)
    grid_spec = pltpu.PrefetchScalarGridSpec(
        num_scalar_prefetch=1,
        grid=(n // tm,),
        in_specs=[smem, smem,
                  pl.BlockSpec((tm, D_MODEL), lambda i, off: (i, 0)),
                  pl.BlockSpec(memory_space=pl.ANY)],
        out_specs=pl.BlockSpec(memory_space=pl.ANY),
        scratch_shapes=[pltpu.SemaphoreType.DMA],
    )
    return pl.pallas_call(
        _dispatch_kernel,
        out_shape=jax.ShapeDtypeStruct(xs_init.shape, xs_init.dtype),
        grid_spec=grid_spec,
        input_output_aliases={4: 0},
        compiler_params=pltpu.CompilerParams(dimension_semantics=("arbitrary",),
                                             vmem_limit_bytes=VMEM_LIMIT),
        name="dispatch",
    )(offsets, ids_flat, rk_flat, x1, xs_init)


def _expert_kernel(te_ref, nu_ref, xs_ref, wgu_ref, bgu_ref, wdn_ref, bdn_ref, y_ref,
                   wgu_b, wdn_b):
    t = pl.program_id(0)
    e = te_ref[t]
    valid = t < nu_ref[0]
    fresh = jnp.logical_or(t == 0, te_ref[jnp.maximum(t - 1, 0)] != e)

    @pl.when(jnp.logical_and(valid, fresh))
    def _():
        for r in range(0, D_MODEL, LANES):
            wgu_b[r:r + LANES, :] = wgu_ref[r:r + LANES, :].astype(BF16)
        for r in range(0, D_FF, LANES):
            wdn_b[r:r + LANES, :] = wdn_ref[r:r + LANES, :].astype(BF16)

    @pl.when(valid)
    def _():
        h = _dot(xs_ref[...].astype(BF16), wgu_b[...]) + bgu_ref[...]
        glu = jnp.minimum(h[:, :D_FF], SWIGLU_LIMIT)
        lin = jnp.clip(h[:, D_FF:], -SWIGLU_LIMIT, SWIGLU_LIMIT)
        act = (lin + 1.0) * (glu * (1.0 / (1.0 + jnp.exp(-SWIGLU_ALPHA * glu))))
        y_ref[...] = _dot(act.astype(BF16), wdn_b[...]) + bdn_ref[...]

    @pl.when(jnp.logical_not(valid))
    def _():
        y_ref[...] = jnp.zeros_like(y_ref)


def _experts(tile_e, n_used, xs, wgu, bgu, wdn, bdn, layer):
    p = xs.shape[0]
    tg = EXPERT_TG
    n_tiles = p // tg
    wspec = lambda r, c: pl.BlockSpec((None, None, r, c), lambda t, te, nu: (layer, te[t], 0, 0))
    grid_spec = pltpu.PrefetchScalarGridSpec(
        num_scalar_prefetch=2,
        grid=(n_tiles,),
        in_specs=[pl.BlockSpec((tg, D_MODEL), lambda t, te, nu: (t, 0)),
                  wspec(D_MODEL, 2 * D_FF), wspec(1, 2 * D_FF), wspec(D_FF, D_MODEL), wspec(1, D_MODEL)],
        out_specs=pl.BlockSpec((tg, D_MODEL), lambda t, te, nu: (t, 0)),
        scratch_shapes=[pltpu.VMEM((D_MODEL, 2 * D_FF), BF16), pltpu.VMEM((D_FF, D_MODEL), BF16)],
    )
    return pl.pallas_call(
        _expert_kernel,
        out_shape=jax.ShapeDtypeStruct((p, D_MODEL), F32),
        grid_spec=grid_spec,
        compiler_params=pltpu.CompilerParams(dimension_semantics=("arbitrary",),
                                             vmem_limit_bytes=VMEM_LIMIT),
        name="experts",
    )(tile_e, n_used, xs, wgu, bgu, wdn, bdn)


def _combine_kernel(off_ref, ids_ref, rk_ref, x1_ref, gt_ref, g2_ref, b2_ref, y_ref, o_ref,
                    buf, sem, *, alpha):
    tm = x1_ref.shape[0]

    def issue(tok, carry):
        for j in range(TOP_K):
            a = tok * TOP_K + j
            _row_copy(y_ref, off_ref[ids_ref[a]] + rk_ref[a], buf.at[j], tok, sem).start()
        return carry

    lax.fori_loop(0, tm, issue, 0)

    def drain(a, carry):
        _row_copy(y_ref, 0, buf.at[0], 0, sem).wait()
        return carry

    lax.fori_loop(0, tm * TOP_K, drain, 0)

    f = jnp.zeros((tm, D_MODEL), F32)
    for j in range(TOP_K):
        f = f + gt_ref[:, j:j + 1] * buf[j]
    o_ref[...] = _layer_norm(alpha * x1_ref[...] + f, g2_ref[...], b2_ref[...])


def _combine(offsets, ids_flat, rk_flat, x1, gates, g2, b2, y, alpha):
    n = x1.shape[0]
    tm = min(COMBINE_TM, n)
    assert n % tm == 0
    smem = pl.BlockSpec((tm * TOP_K,), lambda i, off: (i,), memory_space=pltpu.SMEM)
    grid_spec = pltpu.PrefetchScalarGridSpec(
        num_scalar_prefetch=1,
        grid=(n // tm,),
        in_specs=[smem, smem,
                  pl.BlockSpec((tm, D_MODEL), lambda i, off: (i, 0)),
                  pl.BlockSpec((tm, LANES), lambda i, off: (i, 0)),
                  pl.BlockSpec((1, D_MODEL), lambda i, off: (0, 0)),
                  pl.BlockSpec((1, D_MODEL), lambda i, off: (0, 0)),
                  pl.BlockSpec(memory_space=pl.ANY)],
        out_specs=pl.BlockSpec((tm, D_MODEL), lambda i, off: (i, 0)),
        scratch_shapes=[pltpu.VMEM((TOP_K, tm, D_MODEL), F32), pltpu.SemaphoreType.DMA],
    )
    return pl.pallas_call(
        functools.partial(_combine_kernel, alpha=alpha),
        out_shape=jax.ShapeDtypeStruct((n, D_MODEL), F32),
        grid_spec=grid_spec,
        compiler_params=pltpu.CompilerParams(dimension_semantics=("arbitrary",),
                                             vmem_limit_bytes=VMEM_LIMIT),
        name="combine",
    )(offsets, ids_flat, rk_flat, x1, gates, g2, b2, y)


def _rope_tables(positions):
    pos = positions.reshape(-1).astype(F32)[:, None]
    n = pos.shape[0]

    def cs(rot_dim):
        half = rot_dim // 2
        inv_freq = ROPE_THETA ** (-jnp.arange(half, dtype=F32) / half)
        ang = pos * inv_freq
        return jnp.cos(ang), jnp.sin(ang)

    cos_p, sin_p = cs(A_ROT_DIM)
    cos_c, sin_c = cs(C_ROPE_DIM)
    hp, hc = A_ROT_DIM // 2, C_ROPE_DIM // 2
    one = lambda w: jnp.ones((n, w), F32)
    zero = lambda w: jnp.zeros((n, w), F32)

    a64 = jnp.concatenate([cos_p, cos_p, one(64 - 2 * hp)] * 2, axis=1)
    bm64 = jnp.concatenate([-sin_p, zero(64 - hp)] * 2, axis=1)
    bp64 = jnp.concatenate([zero(hp), sin_p, zero(64 - 2 * hp)] * 2, axis=1)
    a128 = jnp.concatenate([one(64), cos_c, cos_c, one(32)], axis=1)
    bm128 = jnp.concatenate([zero(64), -sin_c, zero(64 - hc)], axis=1)
    bp128 = jnp.concatenate([zero(64 + hc), sin_c, zero(32)], axis=1)
    iw_scale = jnp.full((n, IDX_HEADS), IDX_HEADS ** -0.5, F32)
    as2 = jnp.concatenate([iw_scale, zero(64 - IDX_HEADS), cos_c, cos_c, zero(32)], axis=1)
    return {"t64": jnp.stack([a64, bm64, bp64]),
            "t128": jnp.stack([a128, bm128, bp128]),
            "ts2": jnp.stack([as2, bm128, bp128])}


def _layer_weights(l, w_in, gm_norm_g, gm_norm_b, gm_w_s, gm_b_s, mla_q_norm, mla_kv_norm,
                   mla_w_uq, mla_w_ukv, w_branch, w_out, ln1_g, ln1_b, router_w, router_b):
    w = w_in[l]
    o = 0
    cols = {}
    for name, width in (("aq", 512), ("ak", 512), ("av", 512), ("iq", 512), ("ik", 64), ("iw", 8),
                        ("buv", 1024), ("cdq", 256), ("cdkv", 128), ("ckr", 32), ("gate", 3072)):
        cols[name] = w[:, o:o + width]
        o += width
    zc = lambda width: jnp.zeros((D_MODEL, width), F32)
    s2 = jnp.concatenate([cols["iw"], zc(64 - IDX_HEADS), cols["ckr"], zc(32)], axis=1)
    wp = jnp.concatenate([cols["aq"] * (A_HEAD_DIM ** -0.5), cols["ak"],
                          cols["iq"] * (IDX_HEAD_DIM ** -0.5), cols["av"],
                          cols["ik"], cols["ik"], s2, cols["buv"], cols["cdq"], cols["cdkv"]], axis=1)

    uq = mla_w_uq[l].reshape(C_Q_RANK, C_HEADS, C_NOPE_DIM + C_ROPE_DIM)
    wuq = jnp.pad(uq, ((0, 0), (0, 0), (0, LANES - C_NOPE_DIM - C_ROPE_DIM))).reshape(C_Q_RANK, -1)
    ukv = mla_w_ukv[l].reshape(C_KV_RANK, C_HEADS, C_NOPE_DIM + C_V_DIM)
    wk = jnp.pad(ukv[:, :, :C_NOPE_DIM], ((0, 0), (0, 0), (0, LANES - C_NOPE_DIM))).reshape(C_KV_RANK, -1)
    wv = ukv[:, :, C_NOPE_DIM:].reshape(C_KV_RANK, -1)
    src = jnp.arange(LANES)[:, None]
    dst = jnp.arange(C_HEADS * LANES)[None, :]
    place = ((dst % LANES == src) & (src >= C_NOPE_DIM) & (src < C_NOPE_DIM + C_ROPE_DIM)).astype(F32)
    wke = jnp.concatenate([wk, place], axis=0)

    tril = jnp.tril(jnp.ones((B_CHUNK, B_CHUNK), dtype=bool))
    ws = jnp.where(tril[None], gm_w_s[l], 0)
    bsf = jnp.repeat(gm_b_s[l].T, B_GROUP_DIM, axis=1)
    rw = jnp.pad(router_w[l], ((0, 0), (0, LANES - N_EXPERTS)))
    rb = jnp.pad(router_b[l], (0, LANES - N_EXPERTS))[None, :]
    return {
        "wp": wp.astype(BF16), "wuq": wuq.astype(BF16), "wke": wke.astype(BF16), "wv": wv.astype(BF16),
        "ws": ws.astype(BF16), "gmg": gm_norm_g[l][None, :], "gmb": gm_norm_b[l][None, :], "bsf": bsf,
        "qn": mla_q_norm[l][None, :], "kvn": mla_kv_norm[l][None, :],
        "wg": cols["gate"].astype(BF16), "wbr": w_branch[l].astype(BF16), "wo": w_out[l].astype(BF16),
        "g1": ln1_g[l][None, :], "b1": ln1_b[l][None, :], "rw": rw, "rb": rb,
    }


def _expert_tiles(counts, n_tiles):
    tg = EXPERT_TG
    cnt = counts[0, :N_EXPERTS].astype(I32)
    tiles_per = (cnt + tg - 1) // tg
    tile_end = jnp.cumsum(tiles_per)
    offsets = (tile_end - tiles_per) * tg
    n_used = tile_end[-1]
    tid = jnp.minimum(jnp.arange(n_tiles, dtype=I32), n_used - 1)
    tile_e = jnp.sum(tid[:, None] >= tile_end[None, :], axis=1).astype(I32)
    off128 = jnp.pad(offsets, (0, LANES - N_EXPERTS)).astype(I32)
    return off128, tile_e, n_used.reshape(1).astype(I32)


def _layer(l, x2, tabs, batch, seq, depth, p):
    alpha = (2 * depth) ** 0.25
    n = x2.shape[0]
    wts = _layer_weights(l, p["w_in"], p["gm_norm_g"], p["gm_norm_b"], p["gm_w_s"], p["gm_b_s"],
                         p["mla_q_norm"], p["mla_kv_norm"], p["mla_w_uq"], p["mla_w_ukv"],
                         p["w_branch"], p["w_out"], p["ln1_g"], p["ln1_b"], p["router_w"], p["router_b"])
    aq, ak, iq, avt, ik2, iwt, o_b, q, k, vt = _proj(x2, wts, tabs)
    o_a = _dsa(iq, iwt, aq, ik2, ak, avt, batch, seq)
    o_c = _mla(q, k, vt, batch, seq)
    x1, ids, ranks, gates, counts = _merge(x2, o_a, o_b, o_c, wts, alpha)

    n_tiles = n * TOP_K // EXPERT_TG + N_EXPERTS
    offsets, tile_e, n_used = _expert_tiles(counts, n_tiles)
    ids_flat = ids[:, :TOP_K].reshape(-1)
    rk_flat = ranks[:, :TOP_K].reshape(-1)
    xs = _dispatch(offsets, ids_flat, rk_flat, x1, jnp.zeros((n_tiles * EXPERT_TG, D_MODEL), F32))
    y = _experts(tile_e, n_used, xs, p["exp_w_gu"], p["exp_b_gu"][:, :, None, :],
                 p["exp_w_dn"], p["exp_b_dn"][:, :, None, :], l)
    return _combine(offsets, ids_flat, rk_flat, x1, gates, p["ln2_g"][l][None, :],
                    p["ln2_b"][l][None, :], y, alpha)


def kernel(x, positions, w_in, gm_norm_g, gm_norm_b, gm_w_s, gm_b_s, mla_q_norm, mla_kv_norm, mla_w_uq, mla_w_ukv, w_branch, w_out, ln1_g, ln1_b, router_w, router_b, exp_w_gu, exp_b_gu, exp_w_dn, exp_b_dn, ln2_g, ln2_b):
    batch, seq, _ = x.shape
    depth = w_in.shape[0]
    p = dict(w_in=w_in, gm_norm_g=gm_norm_g, gm_norm_b=gm_norm_b, gm_w_s=gm_w_s, gm_b_s=gm_b_s,
             mla_q_norm=mla_q_norm, mla_kv_norm=mla_kv_norm, mla_w_uq=mla_w_uq, mla_w_ukv=mla_w_ukv,
             w_branch=w_branch, w_out=w_out, ln1_g=ln1_g, ln1_b=ln1_b, router_w=router_w,
             router_b=router_b, exp_w_gu=exp_w_gu, exp_b_gu=exp_b_gu, exp_w_dn=exp_w_dn,
             exp_b_dn=exp_b_dn, ln2_g=ln2_g, ln2_b=ln2_b)
    tabs = _rope_tables(positions)
    x2 = x.reshape(batch * seq, D_MODEL)
    for l in range(depth):
        x2 = _layer(l, x2, tabs, batch, seq, depth, p)
    return x2.reshape(batch, seq, D_MODEL)
```

```python
import functools
import math

import jax
import jax.numpy as jnp
from jax import lax
from jax.experimental import pallas as pl
from jax.experimental.pallas import tpu as pltpu

F32 = jnp.float32
BF16 = jnp.bfloat16
I32 = jnp.int32

D_MODEL = 1024
BRANCH_WIDTH = 512
N_BRANCH = 3
ROPE_THETA = 500000.0
LN_EPS = 1e-5
RMS_EPS = 1e-6

A_HEADS = 8
A_HEAD_DIM = 64
A_ROT_DIM = 16
IDX_HEADS = 8
IDX_HEAD_DIM = 64
IDX_TOPK_MAX = 256

B_GROUPS = 8
B_GROUP_DIM = 64
B_CHUNK = 128

C_HEADS = 8
C_NOPE_DIM = 64
C_ROPE_DIM = 32
C_V_DIM = 64
C_Q_RANK = 256
C_KV_RANK = 128

N_EXPERTS = 32
TOP_K = 4
D_FF = 1024
SWIGLU_LIMIT = 7.0
SWIGLU_ALPHA = 1.702

LANES = 128
ATT_T = 512
PROJ_TM = 512
MERGE_TM = 256
DISPATCH_TM = 256
COMBINE_TM = 128
EXPERT_TG = 256
MOE_COLS = 256
VMEM_LIMIT = 56 * 1024 * 1024

FLASH_LOOKAHEAD = 2
NEG = -1e30
INT_MIN = -(2 ** 31)

_C_AQ, _C_AK, _C_IQ, _C_AV = 0, 512, 1024, 1536
_C_S1, _C_S2, _C_B, _C_C, _C_END = 2048, 2176, 2304, 3328, 3712


def _dot(a, b):
    return jnp.dot(a, b, preferred_element_type=F32)


def _dot_nt(a, b):
    return lax.dot_general(a, b, (((1,), (1,)), ((), ())), preferred_element_type=F32)


def _rot(y, a, bm, bp, r):
    w = y.shape[1]
    return y * a + pltpu.roll(y, w - r, 1) * bm + pltpu.roll(y, r, 1) * bp


def _layer_norm(x, g, b):
    mu = jnp.mean(x, axis=-1, keepdims=True)
    xc = x - mu
    var = jnp.mean(xc * xc, axis=-1, keepdims=True)
    return xc * lax.rsqrt(var + LN_EPS) * g + b


def _rms_norm(x, g):
    return x * lax.rsqrt(jnp.mean(x * x, axis=-1, keepdims=True) + RMS_EPS) * g


def _const_spec(shape):
    nd = len(shape)
    return pl.BlockSpec(shape, lambda *_: (0,) * nd)


def _proj_kernel(x_ref, wp_ref, wuq_ref, wke_ref, wv_ref, ws_ref, gmg_ref, gmb_ref, bsf_ref,
                 qn_ref, kvn_ref, t64_ref, t128_ref, ts2_ref,
                 aq_o, ak_o, iq_o, avt_o, ik2_o, iwt_o, ob_o, q_o, k_o, vt_o):
    tm = x_ref.shape[0]
    xb = x_ref[...].astype(BF16)

    def mm(c0, c1):
        return _dot(xb, wp_ref[:, c0:c1])

    a64, bm64, bp64 = t64_ref[0], t64_ref[1], t64_ref[2]
    a4 = jnp.concatenate([a64] * 4, axis=1)
    bm4 = jnp.concatenate([bm64] * 4, axis=1)
    bp4 = jnp.concatenate([bp64] * 4, axis=1)
    for c0, out in ((_C_AQ, aq_o), (_C_AK, ak_o), (_C_IQ, iq_o)):
        out[...] = _rot(mm(c0, c0 + 512), a4, bm4, bp4, A_ROT_DIM // 2).astype(BF16)
    avt_o[...] = mm(_C_AV, _C_AV + 512).T.astype(BF16)
    ik2_o[...] = _rot(mm(_C_S1, _C_S1 + LANES), a64, bm64, bp64, A_ROT_DIM // 2).astype(BF16)
    sm2 = _rot(mm(_C_S2, _C_S2 + LANES), ts2_ref[0], ts2_ref[1], ts2_ref[2], C_ROPE_DIM // 2)
    iwt_o[...] = sm2.T[:IDX_HEADS, :]

    buv = mm(_C_B, _C_B + 2 * BRANCH_WIDTH)
    buv = 0.5 * buv * (1.0 + lax.erf(buv * (1.0 / math.sqrt(2.0))))
    u = buv[:, :BRANCH_WIDTH]
    v = _layer_norm(buv[:, BRANCH_WIDTH:], gmg_ref[...], gmb_ref[...]).astype(BF16)
    lane = lax.broadcasted_iota(I32, (B_CHUNK, LANES), 1)
    for c in range(tm // B_CHUNK):
        rows = slice(c * B_CHUNK, (c + 1) * B_CHUNK)
        for pr in range(B_GROUPS // 2):
            cols = slice(pr * LANES, (pr + 1) * LANES)
            vp = v[rows, cols]
            s = jnp.where(lane < B_GROUP_DIM, _dot(ws_ref[2 * pr], vp), _dot(ws_ref[2 * pr + 1], vp))
            ob_o[rows, cols] = (u[rows, cols] * (s + bsf_ref[:, cols])).astype(BF16)

    cc = mm(_C_C, _C_END)
    cq = _rms_norm(cc[:, :C_Q_RANK], qn_ref[...]).astype(BF16)
    a8 = jnp.concatenate([t128_ref[0]] * C_HEADS, axis=1)
    bm8 = jnp.concatenate([t128_ref[1]] * C_HEADS, axis=1)
    bp8 = jnp.concatenate([t128_ref[2]] * C_HEADS, axis=1)
    q_o[...] = _rot(_dot(cq, wuq_ref[...]), a8, bm8, bp8, C_ROPE_DIM // 2).astype(BF16)
    ckv = _rms_norm(cc[:, C_Q_RANK:], kvn_ref[...]).astype(BF16)
    kin = jnp.concatenate([ckv, sm2.astype(BF16)], axis=1)
    k_o[...] = _dot(kin, wke_ref[...]).astype(BF16)
    vt_o[...] = _dot(ckv, wv_ref[...]).T.astype(BF16)


def _proj(x2, wts, tabs):
    n = x2.shape[0]
    tm = min(PROJ_TM, n)
    assert n % tm == 0 and tm % B_CHUNK == 0
    row = lambda w: pl.BlockSpec((tm, w), lambda i: (i, 0))
    col = lambda h: pl.BlockSpec((h, tm), lambda i: (0, i))
    tab = pl.BlockSpec((3, tm, LANES), lambda i: (0, i, 0))
    in_specs = [
        row(D_MODEL),
        _const_spec((D_MODEL, _C_END)),
        _const_spec((C_Q_RANK, C_HEADS * LANES)),
        _const_spec((C_KV_RANK + LANES, C_HEADS * LANES)),
        _const_spec((C_KV_RANK, C_HEADS * C_V_DIM)),
        _const_spec((B_GROUPS, B_CHUNK, B_CHUNK)),
        _const_spec((1, BRANCH_WIDTH)),
        _const_spec((1, BRANCH_WIDTH)),
        _const_spec((B_CHUNK, BRANCH_WIDTH)),
        _const_spec((1, C_Q_RANK)),
        _const_spec((1, C_KV_RANK)),
        tab, tab, tab,
    ]
    out_shape = [
        jax.ShapeDtypeStruct((n, 512), BF16),
        jax.ShapeDtypeStruct((n, 512), BF16),
        jax.ShapeDtypeStruct((n, 512), BF16),
        jax.ShapeDtypeStruct((512, n), BF16),
        jax.ShapeDtypeStruct((n, LANES), BF16),
        jax.ShapeDtypeStruct((IDX_HEADS, n), F32),
        jax.ShapeDtypeStruct((n, 512), BF16),
        jax.ShapeDtypeStruct((n, C_HEADS * LANES), BF16),
        jax.ShapeDtypeStruct((n, C_HEADS * LANES), BF16),
        jax.ShapeDtypeStruct((C_HEADS * C_V_DIM, n), BF16),
    ]
    out_specs = [row(512), row(512), row(512), col(512), row(LANES), col(IDX_HEADS), row(512),
                 row(C_HEADS * LANES), row(C_HEADS * LANES), col(C_HEADS * C_V_DIM)]
    return pl.pallas_call(
        _proj_kernel,
        out_shape=out_shape,
        grid=(n // tm,),
        in_specs=in_specs,
        out_specs=out_specs,
        compiler_params=pltpu.CompilerParams(dimension_semantics=("arbitrary",),
                                             vmem_limit_bytes=VMEM_LIMIT),
        name="proj",
    )(x2, wts["wp"], wts["wuq"], wts["wke"], wts["wv"], wts["ws"], wts["gmg"], wts["gmb"],
      wts["bsf"], wts["qn"], wts["kvn"], tabs["t64"], tabs["t128"], tabs["ts2"])


def _rep8(x):
    return jnp.broadcast_to(x, (8, x.shape[-1]))


def _fold8(x3, op):
    return _rep8(op(op(x3, axis=0), axis=0, keepdims=True))


def _split8(x):
    return x.reshape(x.shape[0] // 8, 8, x.shape[1])


def _flash_init(m_ref, l_ref, acc_ref):
    m_ref[...] = jnp.full(m_ref.shape, NEG, F32)
    l_ref[...] = jnp.zeros(l_ref.shape, F32)
    acc_ref[...] = jnp.zeros(acc_ref.shape, F32)


def _flash_update(s3, vt, h, m_ref, l_ref, acc_ref, c2):
    n, _, tq = s3.shape
    m_old = m_ref[h]
    m_new = jnp.maximum(m_old, _fold8(s3, jnp.max))
    alpha = jnp.exp2((m_old - m_new) * c2)
    p = jnp.exp2((s3 - m_new[None]) * c2)
    m_ref[h] = m_new
    l_ref[h] = alpha * l_ref[h] + _fold8(p, jnp.sum)
    pv = _dot(vt, p.reshape(n * 8, tq).astype(BF16))
    acc_ref[h] = (_split8(acc_ref[h]) * alpha[None]).reshape(pv.shape) + pv


def _flash_chunk(logits, values, heads, m_ref, l_ref, acc_ref, c2):
    s = {h: logits(h) for h in range(min(FLASH_LOOKAHEAD, heads))}
    for h in range(heads):
        if h + FLASH_LOOKAHEAD < heads:
            s[h + FLASH_LOOKAHEAD] = logits(h + FLASH_LOOKAHEAD)
        _flash_update(s.pop(h), values(h), h, m_ref, l_ref, acc_ref, c2)


def _flash_finish(o_ref, l_ref, acc_ref, heads):
    outs = [(_split8(acc_ref[h]) / l_ref[h][None]).reshape(acc_ref.shape[1:]) for h in range(heads)]
    o_ref[...] = jnp.concatenate(outs, axis=0).T.astype(BF16)


def _dsa_kernel(iq_ref, iwt_ref, aq_ref, ik2_ref, ak_ref, avt_ref, o_ref,
                key_ref, iqm_ref, aqm_ref, thr_ref, m_ref, l_ref, acc_ref, *, n_sel):
    t = iq_ref.shape[0]
    i = pl.program_id(1)
    nchunk = i + 1
    lane = lax.broadcasted_iota(I32, (t, LANES), 1)
    lo = lane < IDX_HEAD_DIM
    krow = lax.broadcasted_iota(I32, (t, t), 0)
    qcol = lax.broadcasted_iota(I32, (t, t), 1)

    for h in range(IDX_HEADS):
        cols = slice((h // 2) * LANES, (h // 2 + 1) * LANES)
        keep = lo if h % 2 == 0 else jnp.logical_not(lo)
        iqm_ref[h] = jnp.where(keep, iq_ref[:, cols], jnp.zeros((t, LANES), BF16))
        aqm_ref[h] = jnp.where(keep, aq_ref[:, cols], jnp.zeros((t, LANES), BF16))

    def rows(c):
        return pl.ds(pl.multiple_of(c * t, t), t)

    def causal_at(c):
        return (krow + c * t) <= (qcol + i * t)

    def idx_body(c, carry):
        kc = ik2_ref[rows(c), :]
        acc = jnp.zeros((t // 8, 8, t), F32)
        for h in range(IDX_HEADS):
            w8 = _rep8(iwt_ref[h:h + 1, :])
            acc = acc + _split8(jnp.maximum(_dot_nt(kc, iqm_ref[h]), 0.0)) * w8[None]
        bits = lax.bitcast_convert_type(acc.reshape(t, t), I32)
        key = bits ^ ((bits >> 31) & 0x7FFFFFFF)
        key_ref[rows(c), :] = jnp.where(causal_at(c), key, INT_MIN)
        return carry

    lax.fori_loop(0, nchunk, idx_body, 0)

    def count(indicator):
        def body(c, acc):
            return acc + jnp.sum(indicator(_split8(key_ref[rows(c), :])), axis=0)
        part = lax.fori_loop(0, nchunk, body, jnp.zeros((8, t), I32))
        return _rep8(jnp.sum(part, axis=0, keepdims=True))

    def bit_body(it, prefix):
        trial_u = prefix | jnp.left_shift(jnp.int32(1), 31 - it)
        trial_s = trial_u ^ INT_MIN
        cnt = count(lambda kc: jnp.where(kc >= trial_s[None], 1, 0))
        return jnp.where(cnt >= n_sel, trial_u, prefix)

    thr = lax.fori_loop(0, 32, bit_body, jnp.zeros((8, t), I32)) ^ INT_MIN
    cnt_ge = count(lambda kc: jnp.where(kc >= thr[None], 1, 0))
    tied = jnp.where(cnt_ge > n_sel, jnp.where(thr != INT_MIN, 1, 0), 0)

    @pl.when(jnp.max(tied) > 0)
    def _():
        need = (n_sel - count(lambda kc: jnp.where(kc > thr[None], 1, 0))).astype(F32)
        lower = jnp.where(qcol < krow, 1.0, 0.0).astype(BF16)

        def body(c, seen):
            kc = _split8(key_ref[rows(c), :])
            eq = jnp.where(kc == thr[None], 1.0, 0.0)
            rank = _split8(_dot(lower, eq.reshape(t, t).astype(BF16))) + seen[None]
            drop = jnp.where(rank >= need[None], eq, 0.0)
            key_ref[rows(c), :] = jnp.where(drop > 0.0, INT_MIN, kc).reshape(t, t)
            return seen + _fold8(eq, jnp.sum)

        lax.fori_loop(0, nchunk, body, jnp.zeros((8, t), F32))

    thr_ref[...] = jnp.maximum(thr, INT_MIN + 1)

    _flash_init(m_ref, l_ref, acc_ref)
    log2e = math.log2(math.e)

    def att_body(c, carry):
        sel = _split8(key_ref[rows(c), :]) >= thr_ref[...][None]

        def logits(h):
            cols = slice((h // 2) * LANES, (h // 2 + 1) * LANES)
            return jnp.where(sel, _split8(_dot_nt(ak_ref[rows(c), cols], aqm_ref[h])), NEG)

        def values(h):
            return avt_ref[h * A_HEAD_DIM:(h + 1) * A_HEAD_DIM, rows(c)]

        _flash_chunk(logits, values, A_HEADS, m_ref, l_ref, acc_ref, log2e)
        return carry

    lax.fori_loop(0, nchunk, att_body, 0)
    _flash_finish(o_ref, l_ref, acc_ref, A_HEADS)


def _dsa(iq, iwt, aq, ik2, ak, avt, batch, seq):
    t = min(ATT_T, seq)
    nq = seq // t
    n_sel = min(IDX_TOPK_MAX, seq // 4)
    assert seq % t == 0 and t % LANES == 0
    qspec = lambda w: pl.BlockSpec((t, w), lambda b, i: (b * nq + i, 0))
    kspec = lambda w: pl.BlockSpec((seq, w), lambda b, i: (b, 0))
    return pl.pallas_call(
        functools.partial(_dsa_kernel, n_sel=n_sel),
        out_shape=jax.ShapeDtypeStruct((batch * seq, 512), BF16),
        grid=(batch, nq),
        in_specs=[qspec(512),
                  pl.BlockSpec((IDX_HEADS, t), lambda b, i: (0, b * nq + i)),
                  qspec(512), kspec(LANES), kspec(512),
                  pl.BlockSpec((512, seq), lambda b, i: (0, b))],
        out_specs=qspec(512),
        scratch_shapes=[
            pltpu.VMEM((seq, t), I32),
            pltpu.VMEM((IDX_HEADS, t, LANES), BF16),
            pltpu.VMEM((A_HEADS, t, LANES), BF16),
            pltpu.VMEM((8, t), I32),
            pltpu.VMEM((A_HEADS, 8, t), F32),
            pltpu.VMEM((A_HEADS, 8, t), F32),
            pltpu.VMEM((A_HEADS, A_HEAD_DIM, t), F32),
        ],
        compiler_params=pltpu.CompilerParams(dimension_semantics=("arbitrary", "arbitrary"),
                                             vmem_limit_bytes=VMEM_LIMIT),
        name="dsa",
    )(iq, iwt, aq, ik2, ak, avt)


def _mla_kernel(q_ref, k_ref, vt_ref, o_ref, m_ref, l_ref, acc_ref):
    t = q_ref.shape[0]
    i = pl.program_id(1)
    krow = lax.broadcasted_iota(I32, (t // 8, 8, t), 0) * 8 + lax.broadcasted_iota(I32, (t // 8, 8, t), 1)
    qcol = lax.broadcasted_iota(I32, (t // 8, 8, t), 2)
    c2 = (C_NOPE_DIM + C_ROPE_DIM) ** -0.5 * math.log2(math.e)
    _flash_init(m_ref, l_ref, acc_ref)

    def chunk(c, masked):
        rows = pl.ds(pl.multiple_of(c * t, t), t)

        def logits(h):
            cols = slice(h * LANES, (h + 1) * LANES)
            s3 = _split8(_dot_nt(k_ref[rows, cols], q_ref[:, cols]))
            return jnp.where(krow <= qcol, s3, NEG) if masked else s3

        def values(h):
            return vt_ref[h * C_V_DIM:(h + 1) * C_V_DIM, rows]

        _flash_chunk(logits, values, C_HEADS, m_ref, l_ref, acc_ref, c2)

    def body(c, carry):
        chunk(c, False)
        return carry

    lax.fori_loop(0, i, body, 0)
    chunk(i, True)
    _flash_finish(o_ref, l_ref, acc_ref, C_HEADS)


def _mla(q, k, vt, batch, seq):
    t = min(ATT_T, seq)
    nq = seq // t
    return pl.pallas_call(
        _mla_kernel,
        out_shape=jax.ShapeDtypeStruct((batch * seq, C_HEADS * C_V_DIM), BF16),
        grid=(batch, nq),
        in_specs=[pl.BlockSpec((t, C_HEADS * LANES), lambda b, i: (b * nq + i, 0)),
                  pl.BlockSpec((seq, C_HEADS * LANES), lambda b, i: (b, 0)),
                  pl.BlockSpec((C_HEADS * C_V_DIM, seq), lambda b, i: (0, b))],
        out_specs=pl.BlockSpec((t, C_HEADS * C_V_DIM), lambda b, i: (b * nq + i, 0)),
        scratch_shapes=[pltpu.VMEM((C_HEADS, 8, t), F32),
                        pltpu.VMEM((C_HEADS, 8, t), F32),
                        pltpu.VMEM((C_HEADS, C_V_DIM, t), F32)],
        compiler_params=pltpu.CompilerParams(dimension_semantics=("arbitrary", "arbitrary"),
                                             vmem_limit_bytes=VMEM_LIMIT),
        name="mla",
    )(q, k, vt)


def _merge_kernel(x_ref, oa_ref, ob_ref, oc_ref, wg_ref, wbr_ref, wo_ref, g1_ref, b1_ref,
                  rw_ref, rb_ref, x1_o, ids_o, rk_o, gt_o, cnt_o, run_ref, *, alpha):
    tm = x_ref.shape[0]

    @pl.when(pl.program_id(0) == 0)
    def _():
        run_ref[...] = jnp.zeros_like(run_ref)

    x = x_ref[...]
    xb = x.astype(BF16)
    merged = jnp.zeros((tm, D_MODEL), F32)
    for n, o_ref in enumerate((oa_ref, ob_ref, oc_ref)):
        z = _dot(xb, wg_ref[:, n * D_MODEL:(n + 1) * D_MODEL])
        merged = merged + (1.0 / (1.0 + jnp.exp(-z))) * _dot(o_ref[...], wbr_ref[n])
    y = _dot(merged.astype(BF16), wo_ref[...])
    x1 = _layer_norm(alpha * x + y, g1_ref[...], b1_ref[...])
    x1_o[...] = x1

    lane = lax.broadcasted_iota(I32, (tm, LANES), 1)
    logits = jnp.dot(x1, rw_ref[...], precision=lax.Precision.HIGHEST,
                     preferred_element_type=F32) + rb_ref[...]
    lg = jnp.where(lane < N_EXPERTS, logits, -jnp.inf)
    ids, vals = [], []
    for _ in range(TOP_K):
        mx = jnp.max(lg, axis=1, keepdims=True)
        idx = jnp.min(jnp.where(lg == mx, lane, LANES), axis=1, keepdims=True)
        ids.append(idx)
        vals.append(mx)
        lg = jnp.where(lane == idx, -jnp.inf, lg)
    es = [jnp.exp(v - vals[0]) for v in vals]
    den = es[0] + es[1] + es[2] + es[3]

    hot = jnp.zeros((tm, LANES), F32)
    for idx in ids:
        hot = hot + jnp.where(lane == idx, 1.0, 0.0)
    r2 = lax.broadcasted_iota(I32, (tm, tm), 0)
    c2 = lax.broadcasted_iota(I32, (tm, tm), 1)
    lower = jnp.where(c2 < r2, 1.0, 0.0).astype(BF16)
    base = run_ref[0:1, :] + _dot(lower, hot.astype(BF16))
    ids_v = jnp.zeros((tm, LANES), I32)
    rk_v = jnp.zeros((tm, LANES), I32)
    gt_v = jnp.zeros((tm, LANES), F32)
    for j in range(TOP_K):
        rank = jnp.sum(jnp.where(lane == ids[j], base, 0.0), axis=1, keepdims=True)
        ids_v = jnp.where(lane == j, ids[j], ids_v)
        rk_v = jnp.where(lane == j, rank.astype(I32), rk_v)
        gt_v = jnp.where(lane == j, es[j] / den, gt_v)
    ids_o[...] = ids_v
    rk_o[...] = rk_v
    gt_o[...] = gt_v
    run = run_ref[0:1, :] + jnp.sum(hot, axis=0, keepdims=True)
    run_ref[...] = jnp.broadcast_to(run, run_ref.shape)
    cnt_o[...] = jnp.broadcast_to(run, cnt_o.shape)


def _merge(x2, oa, ob, oc, wts, alpha):
    n = x2.shape[0]
    tm = min(MERGE_TM, n)
    assert n % tm == 0
    row = lambda w: pl.BlockSpec((tm, w), lambda i: (i, 0))
    in_specs = [row(D_MODEL), row(512), row(512), row(512),
                _const_spec((D_MODEL, N_BRANCH * D_MODEL)),
                _const_spec((N_BRANCH, BRANCH_WIDTH, D_MODEL)),
                _const_spec((D_MODEL, D_MODEL)),
                _const_spec((1, D_MODEL)), _const_spec((1, D_MODEL)),
                _const_spec((D_MODEL, LANES)), _const_spec((1, LANES))]
    out_shape = [jax.ShapeDtypeStruct((n, D_MODEL), F32),
                 jax.ShapeDtypeStruct((n, LANES), I32),
                 jax.ShapeDtypeStruct((n, LANES), I32),
                 jax.ShapeDtypeStruct((n, LANES), F32),
                 jax.ShapeDtypeStruct((8, LANES), F32)]
    out_specs = [row(D_MODEL), row(LANES), row(LANES), row(LANES), _const_spec((8, LANES))]
    return pl.pallas_call(
        functools.partial(_merge_kernel, alpha=alpha),
        out_shape=out_shape,
        grid=(n // tm,),
        in_specs=in_specs,
        out_specs=out_specs,
        scratch_shapes=[pltpu.VMEM((8, LANES), F32)],
        compiler_params=pltpu.CompilerParams(dimension_semantics=("arbitrary",),
                                             vmem_limit_bytes=VMEM_LIMIT),
        name="merge",
    )(x2, oa, ob, oc, wts["wg"], wts["wbr"], wts["wo"], wts["g1"], wts["b1"], wts["rw"], wts["rb"])


def _row_copy(src_ref, src_row, dst_ref, dst_row, sem):
    return pltpu.make_async_copy(src_ref.at[pl.ds(src_row, 1), :], dst_ref.at[pl.ds(dst_row, 1), :], sem)


def _dispatch_kernel(off_ref, ids_ref, rk_ref, x_ref, xs_in_ref, xs_ref, sem):
    del xs_in_ref
    tm = x_ref.shape[0]

    def issue(tok, carry):
        for j in range(TOP_K):
            a = tok * TOP_K + j
            _row_copy(x_ref, tok, xs_ref, off_ref[ids_ref[a]] + rk_ref[a], sem).start()
        return carry

    lax.fori_loop(0, tm, issue, 0)

    def drain(a, carry):
        _row_copy(x_ref, 0, xs_ref, 0, sem).wait()
        return carry

    lax.fori_loop(0, tm * TOP_K, drain, 0)


def _dispatch(offsets, ids_flat, rk_flat, x1, xs_init):
    n = x1.shape[0]
    tm = min(DISPATCH_TM, n)
    assert n % tm == 0
    smem = pl.BlockSpec((tm * TOP_K,), lambda i, off: (i,), memory_space=pltpu.SMEM)
    grid_spec = pltpu.PrefetchScalarGridSpec(
        num_scalar_prefetch=1,
        grid=(n // tm,),
        in_specs=[smem, smem,
                  pl.BlockSpec((tm, D_MODEL), lambda i, off: (i, 0)),
                  pl.BlockSpec(memory_space=pl.ANY)],
        out_specs=pl.BlockSpec(memory_space=pl.ANY),
        scratch_shapes=[pltpu.SemaphoreType.DMA],
    )
    return pl.pallas_call(
        _dispatch_kernel,
        out_shape=jax.ShapeDtypeStruct(xs_init.shape, xs_init.dtype),
        grid_spec=grid_spec,
        input_output_aliases={4: 0},
        compiler_params=pltpu.CompilerParams(dimension_semantics=("arbitrary",),
                                             vmem_limit_bytes=VMEM_LIMIT),
        name="dispatch",
    )(offsets, ids_flat, rk_flat, x1, xs_init)


def _expert_kernel(te_ref, nu_ref, xs_ref, wgu_ref, bgu_ref, wdn_ref, bdn_ref, y_ref,
                   wgu_b, wdn_b):
    t = pl.program_id(0)
    e = te_ref[t]
    valid = t < nu_ref[0]
    fresh = jnp.logical_or(t == 0, te_ref[jnp.maximum(t - 1, 0)] != e)

    @pl.when(jnp.logical_and(valid, fresh))
    def _():
        for r in range(0, D_MODEL, LANES):
            wgu_b[r:r + LANES, :] = wgu_ref[r:r + LANES, :].astype(BF16)
        for r in range(0, D_FF, LANES):
            wdn_b[r:r + LANES, :] = wdn_ref[r:r + LANES, :].astype(BF16)

    @pl.when(valid)
    def _():
        h = _dot(xs_ref[...].astype(BF16), wgu_b[...]) + bgu_ref[...]
        glu = jnp.minimum(h[:, :D_FF], SWIGLU_LIMIT)
        lin = jnp.clip(h[:, D_FF:], -SWIGLU_LIMIT, SWIGLU_LIMIT)
        act = (lin + 1.0) * (glu * (1.0 / (1.0 + jnp.exp(-SWIGLU_ALPHA * glu))))
        y_ref[...] = _dot(act.astype(BF16), wdn_b[...]) + bdn_ref[...]

    @pl.when(jnp.logical_not(valid))
    def _():
        y_ref[...] = jnp.zeros_like(y_ref)


def _experts(tile_e, n_used, xs, wgu, bgu, wdn, bdn, layer):
    p = xs.shape[0]
    tg = EXPERT_TG
    n_tiles = p // tg
    wspec = lambda r, c: pl.BlockSpec((None, None, r, c), lambda t, te, nu: (layer, te[t], 0, 0))
    grid_spec = pltpu.PrefetchScalarGridSpec(
        num_scalar_prefetch=2,
        grid=(n_tiles,),
        in_specs=[pl.BlockSpec((tg, D_MODEL), lambda t, te, nu: (t, 0)),
                  wspec(D_MODEL, 2 * D_FF), wspec(1, 2 * D_FF), wspec(D_FF, D_MODEL), wspec(1, D_MODEL)],
        out_specs=pl.BlockSpec((tg, D_MODEL), lambda t, te, nu: (t, 0)),
        scratch_shapes=[pltpu.VMEM((D_MODEL, 2 * D_FF), BF16), pltpu.VMEM((D_FF, D_MODEL), BF16)],
    )
    return pl.pallas_call(
        _expert_kernel,
        out_shape=jax.ShapeDtypeStruct((p, D_MODEL), F32),
        grid_spec=grid_spec,
        compiler_params=pltpu.CompilerParams(dimension_semantics=("arbitrary",),
                                             vmem_limit_bytes=VMEM_LIMIT),
        name="experts",
    )(tile_e, n_used, xs, wgu, bgu, wdn, bdn)


def _combine_kernel(off_ref, ids_ref, rk_ref, x1_ref, gt_ref, g2_ref, b2_ref, y_ref, o_ref,
                    buf, sem, *, alpha):
    tm = x1_ref.shape[0]

    def issue(tok, carry):
        for j in range(TOP_K):
            a = tok * TOP_K + j
            _row_copy(y_ref, off_ref[ids_ref[a]] + rk_ref[a], buf.at[j], tok, sem).start()
        return carry

    lax.fori_loop(0, tm, issue, 0)

    def drain(a, carry):
        _row_copy(y_ref, 0, buf.at[0], 0, sem).wait()
        return carry

    lax.fori_loop(0, tm * TOP_K, drain, 0)

    f = jnp.zeros((tm, D_MODEL), F32)
    for j in range(TOP_K):
        f = f + gt_ref[:, j:j + 1] * buf[j]
    o_ref[...] = _layer_norm(alpha * x1_ref[...] + f, g2_ref[...], b2_ref[...])


def _combine(offsets, ids_flat, rk_flat, x1, gates, g2, b2, y, alpha):
    n = x1.shape[0]
    tm = min(COMBINE_TM, n)
    assert n % tm == 0
    smem = pl.BlockSpec((tm * TOP_K,), lambda i, off: (i,), memory_space=pltpu.SMEM)
    grid_spec = pltpu.PrefetchScalarGridSpec(
        num_scalar_prefetch=1,
        grid=(n // tm,),
        in_specs=[smem, smem,
                  pl.BlockSpec((tm, D_MODEL), lambda i, off: (i, 0)),
                  pl.BlockSpec((tm, LANES), lambda i, off: (i, 0)),
                  pl.BlockSpec((1, D_MODEL), lambda i, off: (0, 0)),
                  pl.BlockSpec((1, D_MODEL), lambda i, off: (0, 0)),
                  pl.BlockSpec(memory_space=pl.ANY)],
        out_specs=pl.BlockSpec((tm, D_MODEL), lambda i, off: (i, 0)),
        scratch_shapes=[pltpu.VMEM((TOP_K, tm, D_MODEL), F32), pltpu.SemaphoreType.DMA],
    )
    return pl.pallas_call(
        functools.partial(_combine_kernel, alpha=alpha),
        out_shape=jax.ShapeDtypeStruct((n, D_MODEL), F32),
        grid_spec=grid_spec,
        compiler_params=pltpu.CompilerParams(dimension_semantics=("arbitrary",),
                                             vmem_limit_bytes=VMEM_LIMIT),
        name="combine",
    )(offsets, ids_flat, rk_flat, x1, gates, g2, b2, y)


def _moe_kernel(te_ref, nu_ref, srcc_ref, srcn_ref, dstp_ref, dstc_ref,
                x_hbm, wgu_ref, bgu_ref, wdn_ref, bdn_ref, y_hbm,
                wgu_b, wdn_b, xbuf, ybuf, h_ref, act_ref, gsem, ssem, *, dump_row):
    tg = EXPERT_TG
    t = pl.program_id(0)
    nu = nu_ref[0]
    slot = lax.rem(t, 2)
    other = 1 - slot

    def gather(src_ref, r, sl):
        return pltpu.make_async_copy(x_hbm.at[pl.ds(src_ref[r], 1), :],
                                     xbuf.at[sl, pl.ds(r, 1), :], gsem.at[sl])

    def scatter(dst_row, r, sl):
        return pltpu.make_async_copy(ybuf.at[sl, pl.ds(r, 1), :],
                                     y_hbm.at[pl.ds(dst_row, 1), :], ssem.at[sl])

    def tile_gather(sl):
        return pltpu.make_async_copy(x_hbm.at[pl.ds(0, tg), :], xbuf.at[sl], gsem.at[sl])

    def tile_scatter(sl, row):
        return pltpu.make_async_copy(ybuf.at[sl], y_hbm.at[pl.ds(row, tg), :], ssem.at[sl])

    @pl.when(t == 0)
    def _():
        ybuf[...] = jnp.zeros(ybuf.shape, F32)
        for sl in range(2):
            cp = tile_scatter(sl, dump_row + sl * tg)
            cp.start()
            cp.wait()

        def body(r, carry):
            gather(srcc_ref, r, 0).start()
            return carry
        lax.fori_loop(0, tg, body, 0)

    @pl.when(t < nu)
    def _():
        tile_gather(slot).wait()

        @pl.when(t >= 1)
        def _():
            tile_scatter(slot, 0).wait()

        @pl.when(jnp.logical_or(t == 0, te_ref[jnp.maximum(t - 1, 0)] != te_ref[t]))
        def _():
            for r in range(0, D_MODEL, LANES):
                wgu_b[r:r + LANES, :] = wgu_ref[r:r + LANES, :].astype(BF16)
            for r in range(0, D_FF, LANES):
                wdn_b[r:r + LANES, :] = wdn_ref[r:r + LANES, :].astype(BF16)

        xb = xbuf[slot].astype(BF16)
        n1 = 2 * D_FF // MOE_COLS
        for c in range(n1):
            cols = slice(c * MOE_COLS, (c + 1) * MOE_COLS)
            h_ref[:, cols] = _dot(xb, wgu_b[:, cols]) + bgu_ref[:, cols]
            for r in range(c * tg // n1, (c + 1) * tg // n1):
                gather(srcn_ref, r, other).start()

        h = h_ref[...]
        glu = jnp.minimum(h[:, :D_FF], SWIGLU_LIMIT)
        lin = jnp.clip(h[:, D_FF:], -SWIGLU_LIMIT, SWIGLU_LIMIT)
        act = (lin + 1.0) * (glu * (1.0 / (1.0 + jnp.exp(-SWIGLU_ALPHA * glu))))
        act_ref[...] = act.astype(BF16)

        first = t == 0
        n2 = D_MODEL // MOE_COLS
        for c in range(n2):
            cols = slice(c * MOE_COLS, (c + 1) * MOE_COLS)
            ybuf[slot, :, cols] = _dot(act_ref[...], wdn_b[:, cols]) + bdn_ref[:, cols]
            for r in range(c * tg // n2, (c + 1) * tg // n2):
                scatter(jnp.where(first, dump_row + tg + r, dstp_ref[r]), r, other).start()

        @pl.when(t == nu - 1)
        def _():
            def body(r, carry):
                scatter(dstc_ref[r], r, slot).start()
                return carry
            lax.fori_loop(0, tg, body, 0)
            tile_scatter(other, 0).wait()
            tile_scatter(slot, 0).wait()
            tile_gather(other).wait()


def _moe_experts(tile_e, n_used, src, dst, x1, wgu, bgu, wdn, bdn, layer):
    n = x1.shape[0]
    tg = EXPERT_TG
    n_tiles = src.shape[0] // tg
    dump_row = TOP_K * n
    wspec = lambda r, c: pl.BlockSpec((None, None, r, c), lambda t, te, nu: (layer, te[t], 0, 0))
    idx = lambda f: pl.BlockSpec((tg,), lambda t, te, nu: (f(t),), memory_space=pltpu.SMEM)
    grid_spec = pltpu.PrefetchScalarGridSpec(
        num_scalar_prefetch=2,
        grid=(n_tiles,),
        in_specs=[idx(lambda t: t), idx(lambda t: jnp.minimum(t + 1, n_tiles - 1)),
                  idx(lambda t: jnp.maximum(t - 1, 0)), idx(lambda t: t),
                  pl.BlockSpec(memory_space=pl.ANY),
                  wspec(D_MODEL, 2 * D_FF), wspec(1, 2 * D_FF), wspec(D_FF, D_MODEL), wspec(1, D_MODEL)],
        out_specs=pl.BlockSpec(memory_space=pl.ANY),
        scratch_shapes=[pltpu.VMEM((D_MODEL, 2 * D_FF), BF16), pltpu.VMEM((D_FF, D_MODEL), BF16),
                        pltpu.VMEM((2, tg, D_MODEL), F32), pltpu.VMEM((2, tg, D_MODEL), F32),
                        pltpu.VMEM((tg, 2 * D_FF), F32), pltpu.VMEM((tg, D_FF), BF16),
                        pltpu.SemaphoreType.DMA((2,)), pltpu.SemaphoreType.DMA((2,))],
    )
    return pl.pallas_call(
        functools.partial(_moe_kernel, dump_row=dump_row),
        out_shape=jax.ShapeDtypeStruct((dump_row + 2 * tg, D_MODEL), F32),
        grid_spec=grid_spec,
        compiler_params=pltpu.CompilerParams(dimension_semantics=("arbitrary",),
                                             vmem_limit_bytes=VMEM_LIMIT),
        name="experts",
    )(tile_e, n_used, src, src, dst, dst, x1, wgu, bgu, wdn, bdn)


def _moe_combine_kernel(x1_ref, gt_ref, g2_ref, b2_ref, y0_ref, y1_ref, y2_ref, y3_ref, o_ref, *, alpha):
    f = jnp.zeros(x1_ref.shape, F32)
    for j, y_ref in enumerate((y0_ref, y1_ref, y2_ref, y3_ref)):
        f = f + gt_ref[:, j:j + 1] * y_ref[...]
    o_ref[...] = _layer_norm(alpha * x1_ref[...] + f, g2_ref[...], b2_ref[...])


def _moe_combine(x1, gates, g2, b2, y, alpha):
    n = x1.shape[0]
    tm = min(MERGE_TM, n)
    nblk = n // tm
    row = lambda w: pl.BlockSpec((tm, w), lambda i: (i, 0))
    yspec = lambda j: pl.BlockSpec((tm, D_MODEL), lambda i: (j * nblk + i, 0))
    return pl.pallas_call(
        functools.partial(_moe_combine_kernel, alpha=alpha),
        out_shape=jax.ShapeDtypeStruct((n, D_MODEL), F32),
        grid=(nblk,),
        in_specs=[row(D_MODEL), row(LANES), _const_spec((1, D_MODEL)), _const_spec((1, D_MODEL)),
                  yspec(0), yspec(1), yspec(2), yspec(3)],
        out_specs=row(D_MODEL),
        compiler_params=pltpu.CompilerParams(dimension_semantics=("arbitrary",),
                                             vmem_limit_bytes=VMEM_LIMIT),
        name="combine",
    )(x1, gates, g2, b2, y, y, y, y)


def _sorted_row_maps(offsets, ids, ranks, n_rows):
    n = ids.shape[0]
    tg = EXPERT_TG
    pos = (offsets[ids[:, :TOP_K]] + ranks[:, :TOP_K]).reshape(-1)
    tok = jnp.repeat(jnp.arange(n, dtype=I32), TOP_K)
    slot = jnp.tile(jnp.arange(TOP_K, dtype=I32), n)
    rows = jnp.arange(n_rows, dtype=I32)
    src = jnp.zeros((n_rows,), I32).at[pos].set(tok, unique_indices=True)
    pad_dst = TOP_K * n + ((rows // tg) % 2) * tg + rows % tg
    dst = pad_dst.at[pos].set(slot * n + tok, unique_indices=True)
    return src, dst


def _rope_tables(positions):
    pos = positions.reshape(-1).astype(F32)[:, None]
    n = pos.shape[0]

    def cs(rot_dim):
        half = rot_dim // 2
        inv_freq = ROPE_THETA ** (-jnp.arange(half, dtype=F32) / half)
        ang = pos * inv_freq
        return jnp.cos(ang), jnp.sin(ang)

    cos_p, sin_p = cs(A_ROT_DIM)
    cos_c, sin_c = cs(C_ROPE_DIM)
    hp, hc = A_ROT_DIM // 2, C_ROPE_DIM // 2
    one = lambda w: jnp.ones((n, w), F32)
    zero = lambda w: jnp.zeros((n, w), F32)

    a64 = jnp.concatenate([cos_p, cos_p, one(64 - 2 * hp)] * 2, axis=1)
    bm64 = jnp.concatenate([-sin_p, zero(64 - hp)] * 2, axis=1)
    bp64 = jnp.concatenate([zero(hp), sin_p, zero(64 - 2 * hp)] * 2, axis=1)
    a128 = jnp.concatenate([one(64), cos_c, cos_c, one(32)], axis=1)
    bm128 = jnp.concatenate([zero(64), -sin_c, zero(64 - hc)], axis=1)
    bp128 = jnp.concatenate([zero(64 + hc), sin_c, zero(32)], axis=1)
    iw_scale = jnp.full((n, IDX_HEADS), IDX_HEADS ** -0.5, F32)
    as2 = jnp.concatenate([iw_scale, zero(64 - IDX_HEADS), cos_c, cos_c, zero(32)], axis=1)
    return {"t64": jnp.stack([a64, bm64, bp64]),
            "t128": jnp.stack([a128, bm128, bp128]),
            "ts2": jnp.stack([as2, bm128, bp128])}


def _layer_weights(l, w_in, gm_norm_g, gm_norm_b, gm_w_s, gm_b_s, mla_q_norm, mla_kv_norm,
                   mla_w_uq, mla_w_ukv, w_branch, w_out, ln1_g, ln1_b, router_w, router_b):
    w = w_in[l]
    o = 0
    cols = {}
    for name, width in (("aq", 512), ("ak", 512), ("av", 512), ("iq", 512), ("ik", 64), ("iw", 8),
                        ("buv", 1024), ("cdq", 256), ("cdkv", 128), ("ckr", 32), ("gate", 3072)):
        cols[name] = w[:, o:o + width]
        o += width
    zc = lambda width: jnp.zeros((D_MODEL, width), F32)
    s2 = jnp.concatenate([cols["iw"], zc(64 - IDX_HEADS), cols["ckr"], zc(32)], axis=1)
    wp = jnp.concatenate([cols["aq"] * (A_HEAD_DIM ** -0.5), cols["ak"],
                          cols["iq"] * (IDX_HEAD_DIM ** -0.5), cols["av"],
                          cols["ik"], cols["ik"], s2, cols["buv"], cols["cdq"], cols["cdkv"]], axis=1)

    uq = mla_w_uq[l].reshape(C_Q_RANK, C_HEADS, C_NOPE_DIM + C_ROPE_DIM)
    wuq = jnp.pad(uq, ((0, 0), (0, 0), (0, LANES - C_NOPE_DIM - C_ROPE_DIM))).reshape(C_Q_RANK, -1)
    ukv = mla_w_ukv[l].reshape(C_KV_RANK, C_HEADS, C_NOPE_DIM + C_V_DIM)
    wk = jnp.pad(ukv[:, :, :C_NOPE_DIM], ((0, 0), (0, 0), (0, LANES - C_NOPE_DIM))).reshape(C_KV_RANK, -1)
    wv = ukv[:, :, C_NOPE_DIM:].reshape(C_KV_RANK, -1)
    src = jnp.arange(LANES)[:, None]
    dst = jnp.arange(C_HEADS * LANES)[None, :]
    place = ((dst % LANES == src) & (src >= C_NOPE_DIM) & (src < C_NOPE_DIM + C_ROPE_DIM)).astype(F32)
    wke = jnp.concatenate([wk, place], axis=0)

    tril = jnp.tril(jnp.ones((B_CHUNK, B_CHUNK), dtype=bool))
    ws = jnp.where(tril[None], gm_w_s[l], 0)
    bsf = jnp.repeat(gm_b_s[l].T, B_GROUP_DIM, axis=1)
    rw = jnp.pad(router_w[l], ((0, 0), (0, LANES - N_EXPERTS)))
    rb = jnp.pad(router_b[l], (0, LANES - N_EXPERTS))[None, :]
    return {
        "wp": wp.astype(BF16), "wuq": wuq.astype(BF16), "wke": wke.astype(BF16), "wv": wv.astype(BF16),
        "ws": ws.astype(BF16), "gmg": gm_norm_g[l][None, :], "gmb": gm_norm_b[l][None, :], "bsf": bsf,
        "qn": mla_q_norm[l][None, :], "kvn": mla_kv_norm[l][None, :],
        "wg": cols["gate"].astype(BF16), "wbr": w_branch[l].astype(BF16), "wo": w_out[l].astype(BF16),
        "g1": ln1_g[l][None, :], "b1": ln1_b[l][None, :], "rw": rw, "rb": rb,
    }


def _expert_tiles(counts, n_tiles):
    tg = EXPERT_TG
    cnt = counts[0, :N_EXPERTS].astype(I32)
    tiles_per = (cnt + tg - 1) // tg
    tile_end = jnp.cumsum(tiles_per)
    offsets = (tile_end - tiles_per) * tg
    n_used = tile_end[-1]
    tid = jnp.minimum(jnp.arange(n_tiles, dtype=I32), n_used - 1)
    tile_e = jnp.sum(tid[:, None] >= tile_end[None, :], axis=1).astype(I32)
    off128 = jnp.pad(offsets, (0, LANES - N_EXPERTS)).astype(I32)
    return off128, tile_e, n_used.reshape(1).astype(I32)


def _layer(l, x2, tabs, batch, seq, depth, p):
    alpha = (2 * depth) ** 0.25
    n = x2.shape[0]
    wts = _layer_weights(l, p["w_in"], p["gm_norm_g"], p["gm_norm_b"], p["gm_w_s"], p["gm_b_s"],
                         p["mla_q_norm"], p["mla_kv_norm"], p["mla_w_uq"], p["mla_w_ukv"],
                         p["w_branch"], p["w_out"], p["ln1_g"], p["ln1_b"], p["router_w"], p["router_b"])
    aq, ak, iq, avt, ik2, iwt, o_b, q, k, vt = _proj(x2, wts, tabs)
    o_a = _dsa(iq, iwt, aq, ik2, ak, avt, batch, seq)
    o_c = _mla(q, k, vt, batch, seq)
    x1, ids, ranks, gates, counts = _merge(x2, o_a, o_b, o_c, wts, alpha)

    n_tiles = n * TOP_K // EXPERT_TG + N_EXPERTS
    offsets, tile_e, n_used = _expert_tiles(counts, n_tiles)
    src, dst = _sorted_row_maps(offsets, ids, ranks, n_tiles * EXPERT_TG)
    y = _moe_experts(tile_e, n_used, src, dst, x1, p["exp_w_gu"], p["exp_b_gu"][:, :, None, :],
                     p["exp_w_dn"], p["exp_b_dn"][:, :, None, :], l)
    return _moe_combine(x1, gates, p["ln2_g"][l][None, :], p["ln2_b"][l][None, :], y, alpha)


def kernel(x, positions, w_in, gm_norm_g, gm_norm_b, gm_w_s, gm_b_s, mla_q_norm, mla_kv_norm, mla_w_uq, mla_w_ukv, w_branch, w_out, ln1_g, ln1_b, router_w, router_b, exp_w_gu, exp_b_gu, exp_w_dn, exp_b_dn, ln2_g, ln2_b):
    batch, seq, _ = x.shape
    depth = w_in.shape[0]
    p = dict(w_in=w_in, gm_norm_g=gm_norm_g, gm_norm_b=gm_norm_b, gm_w_s=gm_w_s, gm_b_s=gm_b_s,
             mla_q_norm=mla_q_norm, mla_kv_norm=mla_kv_norm, mla_w_uq=mla_w_uq, mla_w_ukv=mla_w_ukv,
             w_branch=w_branch, w_out=w_out, ln1_g=ln1_g, ln1_b=ln1_b, router_w=router_w,
             router_b=router_b, exp_w_gu=exp_w_gu, exp_b_gu=exp_b_gu, exp_w_dn=exp_w_dn,
             exp_b_dn=exp_b_dn, ln2_g=ln2_g, ln2_b=ln2_b)
    tabs = _rope_tables(positions)
    x2 = x.reshape(batch * seq, D_MODEL)
    for l in range(depth):
        x2 = _layer(l, x2, tabs, batch, seq, depth, p)
    return x2.reshape(batch, seq, D_MODEL)
```

```python
import functools
import math

import jax
import jax.numpy as jnp
from jax import lax
from jax.experimental import pallas as pl
from jax.experimental.pallas import tpu as pltpu

F32 = jnp.float32
BF16 = jnp.bfloat16
I32 = jnp.int32

D_MODEL = 1024
BRANCH_WIDTH = 512
N_BRANCH = 3
ROPE_THETA = 500000.0
LN_EPS = 1e-5
RMS_EPS = 1e-6

A_HEADS = 8
A_HEAD_DIM = 64
A_ROT_DIM = 16
IDX_HEADS = 8
IDX_HEAD_DIM = 64
IDX_TOPK_MAX = 256

B_GROUPS = 8
B_GROUP_DIM = 64
B_CHUNK = 128

C_HEADS = 8
C_NOPE_DIM = 64
C_ROPE_DIM = 32
C_V_DIM = 64
C_Q_RANK = 256
C_KV_RANK = 128

N_EXPERTS = 32
TOP_K = 4
D_FF = 1024
SWIGLU_LIMIT = 7.0
SWIGLU_ALPHA = 1.702

LANES = 128
ATT_T = 512
PROJ_TM = 512
MERGE_TM = 256
DISPATCH_TM = 256
COMBINE_TM = 256
EXPERT_TG = 256
MOE_COLS = 256
VMEM_LIMIT = 56 * 1024 * 1024

FLASH_LOOKAHEAD = 2
NEG = -1e30
INT_MIN = -(2 ** 31)

_C_AQ, _C_AK, _C_IQ, _C_AV = 0, 512, 1024, 1536
_C_S1, _C_S2, _C_B, _C_C, _C_END = 2048, 2176, 2304, 3328, 3712


def _dot(a, b):
    return jnp.dot(a, b, preferred_element_type=F32)


def _dot_nt(a, b):
    return lax.dot_general(a, b, (((1,), (1,)), ((), ())), preferred_element_type=F32)


def _rot(y, a, bm, bp, r):
    w = y.shape[1]
    return y * a + pltpu.roll(y, w - r, 1) * bm + pltpu.roll(y, r, 1) * bp


def _layer_norm(x, g, b):
    mu = jnp.mean(x, axis=-1, keepdims=True)
    xc = x - mu
    var = jnp.mean(xc * xc, axis=-1, keepdims=True)
    return xc * lax.rsqrt(var + LN_EPS) * g + b


def _rms_norm(x, g):
    return x * lax.rsqrt(jnp.mean(x * x, axis=-1, keepdims=True) + RMS_EPS) * g


def _const_spec(shape):
    nd = len(shape)
    return pl.BlockSpec(shape, lambda *_: (0,) * nd)


def _proj_kernel(x_ref, wp_ref, wuq_ref, wke_ref, wv_ref, ws_ref, gmg_ref, gmb_ref, bsf_ref,
                 qn_ref, kvn_ref, t64_ref, t128_ref, ts2_ref,
                 aq_o, ak_o, iq_o, avt_o, ik2_o, iwt_o, ob_o, q_o, k_o, vt_o):
    tm = x_ref.shape[0]
    xb = x_ref[...].astype(BF16)

    def mm(c0, c1):
        return _dot(xb, wp_ref[:, c0:c1])

    a64, bm64, bp64 = t64_ref[0], t64_ref[1], t64_ref[2]
    a4 = jnp.concatenate([a64] * 4, axis=1)
    bm4 = jnp.concatenate([bm64] * 4, axis=1)
    bp4 = jnp.concatenate([bp64] * 4, axis=1)
    for c0, out in ((_C_AQ, aq_o), (_C_AK, ak_o), (_C_IQ, iq_o)):
        out[...] = _rot(mm(c0, c0 + 512), a4, bm4, bp4, A_ROT_DIM // 2).astype(BF16)
    avt_o[...] = mm(_C_AV, _C_AV + 512).T.astype(BF16)
    ik2_o[...] = _rot(mm(_C_S1, _C_S1 + LANES), a64, bm64, bp64, A_ROT_DIM // 2).astype(BF16)
    sm2 = _rot(mm(_C_S2, _C_S2 + LANES), ts2_ref[0], ts2_ref[1], ts2_ref[2], C_ROPE_DIM // 2)
    iwt_o[...] = sm2.T[:IDX_HEADS, :]

    buv = mm(_C_B, _C_B + 2 * BRANCH_WIDTH)
    buv = 0.5 * buv * (1.0 + lax.erf(buv * (1.0 / math.sqrt(2.0))))
    u = buv[:, :BRANCH_WIDTH]
    v = _layer_norm(buv[:, BRANCH_WIDTH:], gmg_ref[...], gmb_ref[...]).astype(BF16)
    lane = lax.broadcasted_iota(I32, (B_CHUNK, LANES), 1)
    for c in range(tm // B_CHUNK):
        rows = slice(c * B_CHUNK, (c + 1) * B_CHUNK)
        for pr in range(B_GROUPS // 2):
            cols = slice(pr * LANES, (pr + 1) * LANES)
            vp = v[rows, cols]
            s = jnp.where(lane < B_GROUP_DIM, _dot(ws_ref[2 * pr], vp), _dot(ws_ref[2 * pr + 1], vp))
            ob_o[rows, cols] = (u[rows, cols] * (s + bsf_ref[:, cols])).astype(BF16)

    cc = mm(_C_C, _C_END)
    cq = _rms_norm(cc[:, :C_Q_RANK], qn_ref[...]).astype(BF16)
    a8 = jnp.concatenate([t128_ref[0]] * C_HEADS, axis=1)
    bm8 = jnp.concatenate([t128_ref[1]] * C_HEADS, axis=1)
    bp8 = jnp.concatenate([t128_ref[2]] * C_HEADS, axis=1)
    q_o[...] = _rot(_dot(cq, wuq_ref[...]), a8, bm8, bp8, C_ROPE_DIM // 2).astype(BF16)
    ckv = _rms_norm(cc[:, C_Q_RANK:], kvn_ref[...]).astype(BF16)
    kin = jnp.concatenate([ckv, sm2.astype(BF16)], axis=1)
    k_o[...] = _dot(kin, wke_ref[...]).astype(BF16)
    vt_o[...] = _dot(ckv, wv_ref[...]).T.astype(BF16)


def _proj(x2, wts, tabs):
    n = x2.shape[0]
    tm = min(PROJ_TM, n)
    assert n % tm == 0 and tm % B_CHUNK == 0
    row = lambda w: pl.BlockSpec((tm, w), lambda i: (i, 0))
    col = lambda h: pl.BlockSpec((h, tm), lambda i: (0, i))
    tab = pl.BlockSpec((3, tm, LANES), lambda i: (0, i, 0))
    in_specs = [
        row(D_MODEL),
        _const_spec((D_MODEL, _C_END)),
        _const_spec((C_Q_RANK, C_HEADS * LANES)),
        _const_spec((C_KV_RANK + LANES, C_HEADS * LANES)),
        _const_spec((C_KV_RANK, C_HEADS * C_V_DIM)),
        _const_spec((B_GROUPS, B_CHUNK, B_CHUNK)),
        _const_spec((1, BRANCH_WIDTH)),
        _const_spec((1, BRANCH_WIDTH)),
        _const_spec((B_CHUNK, BRANCH_WIDTH)),
        _const_spec((1, C_Q_RANK)),
        _const_spec((1, C_KV_RANK)),
        tab, tab, tab,
    ]
    out_shape = [
        jax.ShapeDtypeStruct((n, 512), BF16),
        jax.ShapeDtypeStruct((n, 512), BF16),
        jax.ShapeDtypeStruct((n, 512), BF16),
        jax.ShapeDtypeStruct((512, n), BF16),
        jax.ShapeDtypeStruct((n, LANES), BF16),
        jax.ShapeDtypeStruct((IDX_HEADS, n), F32),
        jax.ShapeDtypeStruct((n, 512), BF16),
        jax.ShapeDtypeStruct((n, C_HEADS * LANES), BF16),
        jax.ShapeDtypeStruct((n, C_HEADS * LANES), BF16),
        jax.ShapeDtypeStruct((C_HEADS * C_V_DIM, n), BF16),
    ]
    out_specs = [row(512), row(512), row(512), col(512), row(LANES), col(IDX_HEADS), row(512),
                 row(C_HEADS * LANES), row(C_HEADS * LANES), col(C_HEADS * C_V_DIM)]
    return pl.pallas_call(
        _proj_kernel,
        out_shape=out_shape,
        grid=(n // tm,),
        in_specs=in_specs,
        out_specs=out_specs,
        compiler_params=pltpu.CompilerParams(dimension_semantics=("arbitrary",),
                                             vmem_limit_bytes=VMEM_LIMIT),
        name="proj",
    )(x2, wts["wp"], wts["wuq"], wts["wke"], wts["wv"], wts["ws"], wts["gmg"], wts["gmb"],
      wts["bsf"], wts["qn"], wts["kvn"], tabs["t64"], tabs["t128"], tabs["ts2"])


def _rep8(x):
    return jnp.broadcast_to(x, (8, x.shape[-1]))


def _fold8(x3, op):
    return _rep8(op(op(x3, axis=0), axis=0, keepdims=True))


def _split8(x):
    return x.reshape(x.shape[0] // 8, 8, x.shape[1])


def _flash_init(m_ref, l_ref, acc_ref):
    m_ref[...] = jnp.full(m_ref.shape, NEG, F32)
    l_ref[...] = jnp.zeros(l_ref.shape, F32)
    acc_ref[...] = jnp.zeros(acc_ref.shape, F32)


def _flash_update(s3, vt, h, m_ref, l_ref, acc_ref, c2):
    n, _, tq = s3.shape
    m_old = m_ref[h]
    m_new = jnp.maximum(m_old, _fold8(s3, jnp.max))
    alpha = jnp.exp2((m_old - m_new) * c2)
    p = jnp.exp2((s3 - m_new[None]) * c2)
    m_ref[h] = m_new
    l_ref[h] = alpha * l_ref[h] + _fold8(p, jnp.sum)
    pv = _dot(vt, p.reshape(n * 8, tq).astype(BF16))
    acc_ref[h] = (_split8(acc_ref[h]) * alpha[None]).reshape(pv.shape) + pv


def _flash_chunk(logits, values, heads, m_ref, l_ref, acc_ref, c2):
    s = {h: logits(h) for h in range(min(FLASH_LOOKAHEAD, heads))}
    for h in range(heads):
        if h + FLASH_LOOKAHEAD < heads:
            s[h + FLASH_LOOKAHEAD] = logits(h + FLASH_LOOKAHEAD)
        _flash_update(s.pop(h), values(h), h, m_ref, l_ref, acc_ref, c2)


def _flash_finish(o_ref, l_ref, acc_ref, heads):
    outs = [(_split8(acc_ref[h]) / l_ref[h][None]).reshape(acc_ref.shape[1:]) for h in range(heads)]
    o_ref[...] = jnp.concatenate(outs, axis=0).T.astype(BF16)


def _dsa_kernel(iq_ref, iwt_ref, aq_ref, ik2_ref, ak_ref, avt_ref, o_ref,
                key_ref, iqm_ref, aqm_ref, thr_ref, m_ref, l_ref, acc_ref, *, n_sel):
    t = iq_ref.shape[0]
    i = pl.program_id(1)
    nchunk = i + 1
    lane = lax.broadcasted_iota(I32, (t, LANES), 1)
    lo = lane < IDX_HEAD_DIM
    krow = lax.broadcasted_iota(I32, (t, t), 0)
    qcol = lax.broadcasted_iota(I32, (t, t), 1)

    for h in range(IDX_HEADS):
        cols = slice((h // 2) * LANES, (h // 2 + 1) * LANES)
        keep = lo if h % 2 == 0 else jnp.logical_not(lo)
        iqm_ref[h] = jnp.where(keep, iq_ref[:, cols], jnp.zeros((t, LANES), BF16))
        aqm_ref[h] = jnp.where(keep, aq_ref[:, cols], jnp.zeros((t, LANES), BF16))

    def rows(c):
        return pl.ds(pl.multiple_of(c * t, t), t)

    def causal_at(c):
        return (krow + c * t) <= (qcol + i * t)

    def idx_body(c, carry):
        kc = ik2_ref[rows(c), :]
        acc = jnp.zeros((t // 8, 8, t), F32)
        for h in range(IDX_HEADS):
            w8 = _rep8(iwt_ref[h:h + 1, :])
            acc = acc + _split8(jnp.maximum(_dot_nt(kc, iqm_ref[h]), 0.0)) * w8[None]
        bits = lax.bitcast_convert_type(acc.reshape(t, t), I32)
        key = bits ^ ((bits >> 31) & 0x7FFFFFFF)
        key_ref[rows(c), :] = jnp.where(causal_at(c), key, INT_MIN)
        return carry

    lax.fori_loop(0, nchunk, idx_body, 0)

    def count(indicator):
        def body(c, acc):
            return acc + jnp.sum(indicator(_split8(key_ref[rows(c), :])), axis=0)
        part = lax.fori_loop(0, nchunk, body, jnp.zeros((8, t), I32))
        return _rep8(jnp.sum(part, axis=0, keepdims=True))

    def bit_body(it, prefix):
        trial_u = prefix | jnp.left_shift(jnp.int32(1), 31 - it)
        trial_s = trial_u ^ INT_MIN
        cnt = count(lambda kc: jnp.where(kc >= trial_s[None], 1, 0))
        return jnp.where(cnt >= n_sel, trial_u, prefix)

    thr = lax.fori_loop(0, 32, bit_body, jnp.zeros((8, t), I32)) ^ INT_MIN
    cnt_ge = count(lambda kc: jnp.where(kc >= thr[None], 1, 0))
    tied = jnp.where(cnt_ge > n_sel, jnp.where(thr != INT_MIN, 1, 0), 0)

    @pl.when(jnp.max(tied) > 0)
    def _():
        need = (n_sel - count(lambda kc: jnp.where(kc > thr[None], 1, 0))).astype(F32)
        lower = jnp.where(qcol < krow, 1.0, 0.0).astype(BF16)

        def body(c, seen):
            kc = _split8(key_ref[rows(c), :])
            eq = jnp.where(kc == thr[None], 1.0, 0.0)
            rank = _split8(_dot(lower, eq.reshape(t, t).astype(BF16))) + seen[None]
            drop = jnp.where(rank >= need[None], eq, 0.0)
            key_ref[rows(c), :] = jnp.where(drop > 0.0, INT_MIN, kc).reshape(t, t)
            return seen + _fold8(eq, jnp.sum)

        lax.fori_loop(0, nchunk, body, jnp.zeros((8, t), F32))

    thr_ref[...] = jnp.maximum(thr, INT_MIN + 1)

    _flash_init(m_ref, l_ref, acc_ref)
    log2e = math.log2(math.e)

    def att_body(c, carry):
        sel = _split8(key_ref[rows(c), :]) >= thr_ref[...][None]

        def logits(h):
            cols = slice((h // 2) * LANES, (h // 2 + 1) * LANES)
            return jnp.where(sel, _split8(_dot_nt(ak_ref[rows(c), cols], aqm_ref[h])), NEG)

        def values(h):
            return avt_ref[h * A_HEAD_DIM:(h + 1) * A_HEAD_DIM, rows(c)]

        _flash_chunk(logits, values, A_HEADS, m_ref, l_ref, acc_ref, log2e)
        return carry

    lax.fori_loop(0, nchunk, att_body, 0)
    _flash_finish(o_ref, l_ref, acc_ref, A_HEADS)


def _dsa(iq, iwt, aq, ik2, ak, avt, batch, seq):
    t = min(ATT_T, seq)
    nq = seq // t
    n_sel = min(IDX_TOPK_MAX, seq // 4)
    assert seq % t == 0 and t % LANES == 0
    qspec = lambda w: pl.BlockSpec((t, w), lambda b, i: (b * nq + i, 0))
    kspec = lambda w: pl.BlockSpec((seq, w), lambda b, i: (b, 0))
    return pl.pallas_call(
        functools.partial(_dsa_kernel, n_sel=n_sel),
        out_shape=jax.ShapeDtypeStruct((batch * seq, 512), BF16),
        grid=(batch, nq),
        in_specs=[qspec(512),
                  pl.BlockSpec((IDX_HEADS, t), lambda b, i: (0, b * nq + i)),
                  qspec(512), kspec(LANES), kspec(512),
                  pl.BlockSpec((512, seq), lambda b, i: (0, b))],
        out_specs=qspec(512),
        scratch_shapes=[
            pltpu.VMEM((seq, t), I32),
            pltpu.VMEM((IDX_HEADS, t, LANES), BF16),
            pltpu.VMEM((A_HEADS, t, LANES), BF16),
            pltpu.VMEM((8, t), I32),
            pltpu.VMEM((A_HEADS, 8, t), F32),
            pltpu.VMEM((A_HEADS, 8, t), F32),
            pltpu.VMEM((A_HEADS, A_HEAD_DIM, t), F32),
        ],
        compiler_params=pltpu.CompilerParams(dimension_semantics=("arbitrary", "arbitrary"),
                                             vmem_limit_bytes=VMEM_LIMIT),
        name="dsa",
    )(iq, iwt, aq, ik2, ak, avt)


def _mla_kernel(q_ref, k_ref, vt_ref, o_ref, m_ref, l_ref, acc_ref):
    t = q_ref.shape[0]
    i = pl.program_id(1)
    krow = lax.broadcasted_iota(I32, (t // 8, 8, t), 0) * 8 + lax.broadcasted_iota(I32, (t // 8, 8, t), 1)
    qcol = lax.broadcasted_iota(I32, (t // 8, 8, t), 2)
    c2 = (C_NOPE_DIM + C_ROPE_DIM) ** -0.5 * math.log2(math.e)
    _flash_init(m_ref, l_ref, acc_ref)

    def chunk(c, masked):
        rows = pl.ds(pl.multiple_of(c * t, t), t)

        def logits(h):
            cols = slice(h * LANES, (h + 1) * LANES)
            s3 = _split8(_dot_nt(k_ref[rows, cols], q_ref[:, cols]))
            return jnp.where(krow <= qcol, s3, NEG) if masked else s3

        def values(h):
            return vt_ref[h * C_V_DIM:(h + 1) * C_V_DIM, rows]

        _flash_chunk(logits, values, C_HEADS, m_ref, l_ref, acc_ref, c2)

    def body(c, carry):
        chunk(c, False)
        return carry

    lax.fori_loop(0, i, body, 0)
    chunk(i, True)
    _flash_finish(o_ref, l_ref, acc_ref, C_HEADS)


def _mla(q, k, vt, batch, seq):
    t = min(ATT_T, seq)
    nq = seq // t
    return pl.pallas_call(
        _mla_kernel,
        out_shape=jax.ShapeDtypeStruct((batch * seq, C_HEADS * C_V_DIM), BF16),
        grid=(batch, nq),
        in_specs=[pl.BlockSpec((t, C_HEADS * LANES), lambda b, i: (b * nq + i, 0)),
                  pl.BlockSpec((seq, C_HEADS * LANES), lambda b, i: (b, 0)),
                  pl.BlockSpec((C_HEADS * C_V_DIM, seq), lambda b, i: (0, b))],
        out_specs=pl.BlockSpec((t, C_HEADS * C_V_DIM), lambda b, i: (b * nq + i, 0)),
        scratch_shapes=[pltpu.VMEM((C_HEADS, 8, t), F32),
                        pltpu.VMEM((C_HEADS, 8, t), F32),
                        pltpu.VMEM((C_HEADS, C_V_DIM, t), F32)],
        compiler_params=pltpu.CompilerParams(dimension_semantics=("arbitrary", "arbitrary"),
                                             vmem_limit_bytes=VMEM_LIMIT),
        name="mla",
    )(q, k, vt)


def _merge_kernel(x_ref, oa_ref, ob_ref, oc_ref, wg_ref, wbr_ref, wo_ref, g1_ref, b1_ref,
                  rw_ref, rb_ref, x1_o, x1r_o, ids_o, rk_o, gt_o, cnt_o, run_ref, *, alpha):
    tm = x_ref.shape[0]

    @pl.when(pl.program_id(0) == 0)
    def _():
        run_ref[...] = jnp.zeros_like(run_ref)

    x = x_ref[...]
    xb = x.astype(BF16)
    merged = jnp.zeros((tm, D_MODEL), F32)
    for n, o_ref in enumerate((oa_ref, ob_ref, oc_ref)):
        z = _dot(xb, wg_ref[:, n * D_MODEL:(n + 1) * D_MODEL])
        merged = merged + (1.0 / (1.0 + jnp.exp(-z))) * _dot(o_ref[...], wbr_ref[n])
    y = _dot(merged.astype(BF16), wo_ref[...])
    x1 = _layer_norm(alpha * x + y, g1_ref[...], b1_ref[...])
    x1_o[...] = x1
    _rows_to_slab(x1r_o, x1)

    lane = lax.broadcasted_iota(I32, (tm, LANES), 1)
    logits = jnp.dot(x1, rw_ref[...], precision=lax.Precision.HIGHEST,
                     preferred_element_type=F32) + rb_ref[...]
    lg = jnp.where(lane < N_EXPERTS, logits, -jnp.inf)
    ids, vals = [], []
    for _ in range(TOP_K):
        mx = jnp.max(lg, axis=1, keepdims=True)
        idx = jnp.min(jnp.where(lg == mx, lane, LANES), axis=1, keepdims=True)
        ids.append(idx)
        vals.append(mx)
        lg = jnp.where(lane == idx, -jnp.inf, lg)
    es = [jnp.exp(v - vals[0]) for v in vals]
    den = es[0] + es[1] + es[2] + es[3]

    hot = jnp.zeros((tm, LANES), F32)
    for idx in ids:
        hot = hot + jnp.where(lane == idx, 1.0, 0.0)
    r2 = lax.broadcasted_iota(I32, (tm, tm), 0)
    c2 = lax.broadcasted_iota(I32, (tm, tm), 1)
    lower = jnp.where(c2 < r2, 1.0, 0.0).astype(BF16)
    base = run_ref[0:1, :] + _dot(lower, hot.astype(BF16))
    ids_v = jnp.zeros((tm, LANES), I32)
    rk_v = jnp.zeros((tm, LANES), I32)
    gt_v = jnp.zeros((tm, LANES), F32)
    for j in range(TOP_K):
        rank = jnp.sum(jnp.where(lane == ids[j], base, 0.0), axis=1, keepdims=True)
        ids_v = jnp.where(lane == j, ids[j], ids_v)
        rk_v = jnp.where(lane == j, rank.astype(I32), rk_v)
        gt_v = jnp.where(lane == j, es[j] / den, gt_v)
    ids_o[...] = ids_v
    rk_o[...] = rk_v
    gt_o[...] = gt_v
    run = run_ref[0:1, :] + jnp.sum(hot, axis=0, keepdims=True)
    run_ref[...] = jnp.broadcast_to(run, run_ref.shape)
    cnt_o[...] = jnp.broadcast_to(run, cnt_o.shape)


def _merge(x2, oa, ob, oc, wts, alpha):
    n = x2.shape[0]
    tm = min(MERGE_TM, n)
    assert n % tm == 0
    row = lambda w: pl.BlockSpec((tm, w), lambda i: (i, 0))
    in_specs = [row(D_MODEL), row(512), row(512), row(512),
                _const_spec((D_MODEL, N_BRANCH * D_MODEL)),
                _const_spec((N_BRANCH, BRANCH_WIDTH, D_MODEL)),
                _const_spec((D_MODEL, D_MODEL)),
                _const_spec((1, D_MODEL)), _const_spec((1, D_MODEL)),
                _const_spec((D_MODEL, LANES)), _const_spec((1, LANES))]
    out_shape = [jax.ShapeDtypeStruct((n, D_MODEL), F32),
                 jax.ShapeDtypeStruct((n, 8, LANES), F32),
                 jax.ShapeDtypeStruct((n, LANES), I32),
                 jax.ShapeDtypeStruct((n, LANES), I32),
                 jax.ShapeDtypeStruct((n, LANES), F32),
                 jax.ShapeDtypeStruct((8, LANES), F32)]
    out_specs = [row(D_MODEL), pl.BlockSpec((tm, 8, LANES), lambda i: (i, 0, 0)),
                 row(LANES), row(LANES), row(LANES), _const_spec((8, LANES))]
    return pl.pallas_call(
        functools.partial(_merge_kernel, alpha=alpha),
        out_shape=out_shape,
        grid=(n // tm,),
        in_specs=in_specs,
        out_specs=out_specs,
        scratch_shapes=[pltpu.VMEM((8, LANES), F32)],
        compiler_params=pltpu.CompilerParams(dimension_semantics=("arbitrary",),
                                             vmem_limit_bytes=VMEM_LIMIT),
        name="merge",
    )(x2, oa, ob, oc, wts["wg"], wts["wbr"], wts["wo"], wts["g1"], wts["b1"], wts["rw"], wts["rb"])


def _row_copy(src_ref, src_row, dst_ref, dst_row, sem):
    return pltpu.make_async_copy(src_ref.at[pl.ds(src_row, 1), :], dst_ref.at[pl.ds(dst_row, 1), :], sem)


def _dispatch_kernel(off_ref, ids_ref, rk_ref, x_ref, xs_in_ref, xs_ref, sem):
    del xs_in_ref
    tm = x_ref.shape[0]

    def issue(tok, carry):
        for j in range(TOP_K):
            a = tok * TOP_K + j
            _row_copy(x_ref, tok, xs_ref, off_ref[ids_ref[a]] + rk_ref[a], sem).start()
        return carry

    lax.fori_loop(0, tm, issue, 0)

    def drain(a, carry):
        _row_copy(x_ref, 0, xs_ref, 0, sem).wait()
        return carry

    lax.fori_loop(0, tm * TOP_K, drain, 0)


def _dispatch(offsets, ids_flat, rk_flat, x1, xs_init):
    n = x1.shape[0]
    tm = min(DISPATCH_TM, n)
    assert n % tm == 0
    smem = pl.BlockSpec((tm * TOP_K,), lambda i, off: (i,), memory_space=pltpu.SMEM)
    grid_spec = pltpu.PrefetchScalarGridSpec(
        num_scalar_prefetch=1,
        grid=(n // tm,),
        in_specs=[smem, smem,
                  pl.BlockSpec((tm, D_MODEL), lambda i, off: (i, 0)),
                  pl.BlockSpec(memory_space=pl.ANY)],
        out_specs=pl.BlockSpec(memory_space=pl.ANY),
        scratch_shapes=[pltpu.SemaphoreType.DMA],
    )
    return pl.pallas_call(
        _dispatch_kernel,
        out_shape=jax.ShapeDtypeStruct(xs_init.shape, xs_init.dtype),
        grid_spec=grid_spec,
        input_output_aliases={4: 0},
        compiler_params=pltpu.CompilerParams(dimension_semantics=("arbitrary",),
                                             vmem_limit_bytes=VMEM_LIMIT),
        name="dispatch",
    )(offsets, ids_flat, rk_flat, x1, xs_init)


def _expert_kernel(te_ref, nu_ref, xs_ref, wgu_ref, bgu_ref, wdn_ref, bdn_ref, y_ref,
                   wgu_b, wdn_b):
    t = pl.program_id(0)
    e = te_ref[t]
    valid = t < nu_ref[0]
    fresh = jnp.logical_or(t == 0, te_ref[jnp.maximum(t - 1, 0)] != e)

    @pl.when(jnp.logical_and(valid, fresh))
    def _():
        for r in range(0, D_MODEL, LANES):
            wgu_b[r:r + LANES, :] = wgu_ref[r:r + LANES, :].astype(BF16)
        for r in range(0, D_FF, LANES):
            wdn_b[r:r + LANES, :] = wdn_ref[r:r + LANES, :].astype(BF16)

    @pl.when(valid)
    def _():
        h = _dot(xs_ref[...].astype(BF16), wgu_b[...]) + bgu_ref[...]
        glu = jnp.minimum(h[:, :D_FF], SWIGLU_LIMIT)
        lin = jnp.clip(h[:, D_FF:], -SWIGLU_LIMIT, SWIGLU_LIMIT)
        act = (lin + 1.0) * (glu * (1.0 / (1.0 + jnp.exp(-SWIGLU_ALPHA * glu))))
        y_ref[...] = _dot(act.astype(BF16), wdn_b[...]) + bdn_ref[...]

    @pl.when(jnp.logical_not(valid))
    def _():
        y_ref[...] = jnp.zeros_like(y_ref)


def _experts(tile_e, n_used, xs, wgu, bgu, wdn, bdn, layer):
    p = xs.shape[0]
    tg = EXPERT_TG
    n_tiles = p // tg
    wspec = lambda r, c: pl.BlockSpec((None, None, r, c), lambda t, te, nu: (layer, te[t], 0, 0))
    grid_spec = pltpu.PrefetchScalarGridSpec(
        num_scalar_prefetch=2,
        grid=(n_tiles,),
        in_specs=[pl.BlockSpec((tg, D_MODEL), lambda t, te, nu: (t, 0)),
                  wspec(D_MODEL, 2 * D_FF), wspec(1, 2 * D_FF), wspec(D_FF, D_MODEL), wspec(1, D_MODEL)],
        out_specs=pl.BlockSpec((tg, D_MODEL), lambda t, te, nu: (t, 0)),
        scratch_shapes=[pltpu.VMEM((D_MODEL, 2 * D_FF), BF16), pltpu.VMEM((D_FF, D_MODEL), BF16)],
    )
    return pl.pallas_call(
        _expert_kernel,
        out_shape=jax.ShapeDtypeStruct((p, D_MODEL), F32),
        grid_spec=grid_spec,
        compiler_params=pltpu.CompilerParams(dimension_semantics=("arbitrary",),
                                             vmem_limit_bytes=VMEM_LIMIT),
        name="experts",
    )(tile_e, n_used, xs, wgu, bgu, wdn, bdn)


def _combine_kernel(off_ref, ids_ref, rk_ref, x1_ref, gt_ref, g2_ref, b2_ref, y_ref, o_ref,
                    buf, sem, *, alpha):
    tm = x1_ref.shape[0]

    def issue(tok, carry):
        for j in range(TOP_K):
            a = tok * TOP_K + j
            _row_copy(y_ref, off_ref[ids_ref[a]] + rk_ref[a], buf.at[j], tok, sem).start()
        return carry

    lax.fori_loop(0, tm, issue, 0)

    def drain(a, carry):
        _row_copy(y_ref, 0, buf.at[0], 0, sem).wait()
        return carry

    lax.fori_loop(0, tm * TOP_K, drain, 0)

    f = jnp.zeros((tm, D_MODEL), F32)
    for j in range(TOP_K):
        f = f + gt_ref[:, j:j + 1] * buf[j]
    o_ref[...] = _layer_norm(alpha * x1_ref[...] + f, g2_ref[...], b2_ref[...])


def _combine(offsets, ids_flat, rk_flat, x1, gates, g2, b2, y, alpha):
    n = x1.shape[0]
    tm = min(COMBINE_TM, n)
    assert n % tm == 0
    smem = pl.BlockSpec((tm * TOP_K,), lambda i, off: (i,), memory_space=pltpu.SMEM)
    grid_spec = pltpu.PrefetchScalarGridSpec(
        num_scalar_prefetch=1,
        grid=(n // tm,),
        in_specs=[smem, smem,
                  pl.BlockSpec((tm, D_MODEL), lambda i, off: (i, 0)),
                  pl.BlockSpec((tm, LANES), lambda i, off: (i, 0)),
                  pl.BlockSpec((1, D_MODEL), lambda i, off: (0, 0)),
                  pl.BlockSpec((1, D_MODEL), lambda i, off: (0, 0)),
                  pl.BlockSpec(memory_space=pl.ANY)],
        out_specs=pl.BlockSpec((tm, D_MODEL), lambda i, off: (i, 0)),
        scratch_shapes=[pltpu.VMEM((TOP_K, tm, D_MODEL), F32), pltpu.SemaphoreType.DMA],
    )
    return pl.pallas_call(
        functools.partial(_combine_kernel, alpha=alpha),
        out_shape=jax.ShapeDtypeStruct((n, D_MODEL), F32),
        grid_spec=grid_spec,
        compiler_params=pltpu.CompilerParams(dimension_semantics=("arbitrary",),
                                             vmem_limit_bytes=VMEM_LIMIT),
        name="combine",
    )(offsets, ids_flat, rk_flat, x1, gates, g2, b2, y)


def _moe_kernel(te_ref, nu_ref, srcc_ref, srcn_ref, dstp_ref, dstc_ref,
                x_hbm, wgu_ref, bgu_ref, wdn_ref, bdn_ref, y_hbm,
                wgu_b, wdn_b, xbuf, ybuf, h_ref, act_ref, gsem, ssem, *, dump_row):
    tg = EXPERT_TG
    t = pl.program_id(0)
    nu = nu_ref[0]
    slot = lax.rem(t, 2)
    other = 1 - slot

    def gather(src_ref, r, sl):
        return pltpu.make_async_copy(x_hbm.at[pl.ds(src_ref[r], 1), :],
                                     xbuf.at[sl, pl.ds(r, 1), :], gsem.at[sl])

    def scatter(dst_row, r, sl):
        return pltpu.make_async_copy(ybuf.at[sl, pl.ds(r, 1), :],
                                     y_hbm.at[pl.ds(dst_row, 1), :], ssem.at[sl])

    def tile_gather(sl):
        return pltpu.make_async_copy(x_hbm.at[pl.ds(0, tg), :], xbuf.at[sl], gsem.at[sl])

    def tile_scatter(sl, row):
        return pltpu.make_async_copy(ybuf.at[sl], y_hbm.at[pl.ds(row, tg), :], ssem.at[sl])

    @pl.when(t == 0)
    def _():
        ybuf[...] = jnp.zeros(ybuf.shape, F32)
        for sl in range(2):
            cp = tile_scatter(sl, dump_row + sl * tg)
            cp.start()
            cp.wait()

        def body(r, carry):
            gather(srcc_ref, r, 0).start()
            return carry
        lax.fori_loop(0, tg, body, 0)

    @pl.when(t < nu)
    def _():
        tile_gather(slot).wait()

        @pl.when(t >= 1)
        def _():
            tile_scatter(slot, 0).wait()

        @pl.when(jnp.logical_or(t == 0, te_ref[jnp.maximum(t - 1, 0)] != te_ref[t]))
        def _():
            for r in range(0, D_MODEL, LANES):
                wgu_b[r:r + LANES, :] = wgu_ref[r:r + LANES, :].astype(BF16)
            for r in range(0, D_FF, LANES):
                wdn_b[r:r + LANES, :] = wdn_ref[r:r + LANES, :].astype(BF16)

        xb = xbuf[slot].astype(BF16)
        n1 = 2 * D_FF // MOE_COLS
        for c in range(n1):
            cols = slice(c * MOE_COLS, (c + 1) * MOE_COLS)
            h_ref[:, cols] = _dot(xb, wgu_b[:, cols]) + bgu_ref[:, cols]
            for r in range(c * tg // n1, (c + 1) * tg // n1):
                gather(srcn_ref, r, other).start()

        h = h_ref[...]
        glu = jnp.minimum(h[:, :D_FF], SWIGLU_LIMIT)
        lin = jnp.clip(h[:, D_FF:], -SWIGLU_LIMIT, SWIGLU_LIMIT)
        act = (lin + 1.0) * (glu * (1.0 / (1.0 + jnp.exp(-SWIGLU_ALPHA * glu))))
        act_ref[...] = act.astype(BF16)

        first = t == 0
        n2 = D_MODEL // MOE_COLS
        for c in range(n2):
            cols = slice(c * MOE_COLS, (c + 1) * MOE_COLS)
            ybuf[slot, :, cols] = _dot(act_ref[...], wdn_b[:, cols]) + bdn_ref[:, cols]
            for r in range(c * tg // n2, (c + 1) * tg // n2):
                scatter(jnp.where(first, dump_row + tg + r, dstp_ref[r]), r, other).start()

        @pl.when(t == nu - 1)
        def _():
            def body(r, carry):
                scatter(dstc_ref[r], r, slot).start()
                return carry
            lax.fori_loop(0, tg, body, 0)
            tile_scatter(other, 0).wait()
            tile_scatter(slot, 0).wait()
            tile_gather(other).wait()


def _moe_experts(tile_e, n_used, src, dst, x1, wgu, bgu, wdn, bdn, layer):
    n = x1.shape[0]
    tg = EXPERT_TG
    n_tiles = src.shape[0] // tg
    dump_row = TOP_K * n
    wspec = lambda r, c: pl.BlockSpec((None, None, r, c), lambda t, te, nu: (layer, te[t], 0, 0))
    idx = lambda f: pl.BlockSpec((tg,), lambda t, te, nu: (f(t),), memory_space=pltpu.SMEM)
    grid_spec = pltpu.PrefetchScalarGridSpec(
        num_scalar_prefetch=2,
        grid=(n_tiles,),
        in_specs=[idx(lambda t: t), idx(lambda t: jnp.minimum(t + 1, n_tiles - 1)),
                  idx(lambda t: jnp.maximum(t - 1, 0)), idx(lambda t: t),
                  pl.BlockSpec(memory_space=pl.ANY),
                  wspec(D_MODEL, 2 * D_FF), wspec(1, 2 * D_FF), wspec(D_FF, D_MODEL), wspec(1, D_MODEL)],
        out_specs=pl.BlockSpec(memory_space=pl.ANY),
        scratch_shapes=[pltpu.VMEM((D_MODEL, 2 * D_FF), BF16), pltpu.VMEM((D_FF, D_MODEL), BF16),
                        pltpu.VMEM((2, tg, D_MODEL), F32), pltpu.VMEM((2, tg, D_MODEL), F32),
                        pltpu.VMEM((tg, 2 * D_FF), F32), pltpu.VMEM((tg, D_FF), BF16),
                        pltpu.SemaphoreType.DMA((2,)), pltpu.SemaphoreType.DMA((2,))],
    )
    return pl.pallas_call(
        functools.partial(_moe_kernel, dump_row=dump_row),
        out_shape=jax.ShapeDtypeStruct((dump_row + 2 * tg, D_MODEL), F32),
        grid_spec=grid_spec,
        compiler_params=pltpu.CompilerParams(dimension_semantics=("arbitrary",),
                                             vmem_limit_bytes=VMEM_LIMIT),
        name="experts",
    )(tile_e, n_used, src, src, dst, dst, x1, wgu, bgu, wdn, bdn)


def _moe_combine_kernel(x1_ref, gt_ref, g2_ref, b2_ref, y0_ref, y1_ref, y2_ref, y3_ref, o_ref, *, alpha):
    f = jnp.zeros(x1_ref.shape, F32)
    for j, y_ref in enumerate((y0_ref, y1_ref, y2_ref, y3_ref)):
        f = f + gt_ref[:, j:j + 1] * y_ref[...]
    o_ref[...] = _layer_norm(alpha * x1_ref[...] + f, g2_ref[...], b2_ref[...])


def _moe_combine(x1, gates, g2, b2, y, alpha):
    n = x1.shape[0]
    tm = min(MERGE_TM, n)
    nblk = n // tm
    row = lambda w: pl.BlockSpec((tm, w), lambda i: (i, 0))
    yspec = lambda j: pl.BlockSpec((tm, D_MODEL), lambda i: (j * nblk + i, 0))
    return pl.pallas_call(
        functools.partial(_moe_combine_kernel, alpha=alpha),
        out_shape=jax.ShapeDtypeStruct((n, D_MODEL), F32),
        grid=(nblk,),
        in_specs=[row(D_MODEL), row(LANES), _const_spec((1, D_MODEL)), _const_spec((1, D_MODEL)),
                  yspec(0), yspec(1), yspec(2), yspec(3)],
        out_specs=row(D_MODEL),
        compiler_params=pltpu.CompilerParams(dimension_semantics=("arbitrary",),
                                             vmem_limit_bytes=VMEM_LIMIT),
        name="combine",
    )(x1, gates, g2, b2, y, y, y, y)


def _sorted_row_maps(offsets, ids, ranks, n_rows):
    n = ids.shape[0]
    tg = EXPERT_TG
    pos = (offsets[ids[:, :TOP_K]] + ranks[:, :TOP_K]).reshape(-1)
    tok = jnp.repeat(jnp.arange(n, dtype=I32), TOP_K)
    slot = jnp.tile(jnp.arange(TOP_K, dtype=I32), n)
    rows = jnp.arange(n_rows, dtype=I32)
    src = jnp.zeros((n_rows,), I32).at[pos].set(tok, unique_indices=True)
    pad_dst = TOP_K * n + ((rows // tg) % 2) * tg + rows % tg
    dst = pad_dst.at[pos].set(slot * n + tok, unique_indices=True)
    return src, dst


def _slab_to_rows(ref, lead=()):
    return jnp.concatenate([ref[lead + (slice(None), c, slice(None))] for c in range(8)], axis=1)


def _rows_to_slab(ref, val):
    for c in range(8):
        ref[:, c, :] = val[:, c * LANES:(c + 1) * LANES]


def _issue_rows(n_tok, copy):
    def body(g, carry):
        for u in range(8):
            for j in range(TOP_K):
                copy(g * 8 + u, j).start()
        return carry
    lax.fori_loop(0, n_tok // 8, body, 0)


def _scatter_kernel(zs_ref, pos_ref, x_ref, xs_ref, zbuf, sem):
    tm = x_ref.shape[0]
    tg = zbuf.shape[0]

    @pl.when(pl.program_id(0) == 0)
    def _():
        zbuf[...] = jnp.zeros(zbuf.shape, F32)
        for e in range(N_EXPERTS):
            cp = pltpu.make_async_copy(zbuf, xs_ref.at[pl.ds(zs_ref[e], tg)], sem)
            cp.start()
            cp.wait()
        n_tiles = xs_ref.shape[0] // tg
        for k in range(N_EXPERTS):
            @pl.when(n_tiles - 1 - k >= zs_ref[N_EXPERTS])
            def _():
                cp = pltpu.make_async_copy(zbuf, xs_ref.at[pl.ds((n_tiles - 1 - k) * tg, tg)], sem)
                cp.start()
                cp.wait()

    _issue_rows(tm, lambda tok, j: pltpu.make_async_copy(
        x_ref.at[tok], xs_ref.at[pos_ref[tok * TOP_K + j]], sem))
    for j in range(TOP_K):
        pltpu.make_async_copy(x_ref, xs_ref.at[pl.ds(0, tm)], sem).wait()


def _scatter_rows(zstart, pos_flat, x1r, n_rows):
    n = x1r.shape[0]
    tm = min(DISPATCH_TM, n)
    assert n % tm == 0
    grid_spec = pltpu.PrefetchScalarGridSpec(
        num_scalar_prefetch=1,
        grid=(n // tm,),
        in_specs=[pl.BlockSpec((tm * TOP_K,), lambda i, zs: (i,), memory_space=pltpu.SMEM),
                  pl.BlockSpec((tm, 8, LANES), lambda i, zs: (i, 0, 0))],
        out_specs=pl.BlockSpec(memory_space=pl.ANY),
        scratch_shapes=[pltpu.VMEM((EXPERT_TG, 8, LANES), F32), pltpu.SemaphoreType.DMA],
    )
    return pl.pallas_call(
        _scatter_kernel,
        out_shape=jax.ShapeDtypeStruct((n_rows, 8, LANES), F32),
        grid_spec=grid_spec,
        compiler_params=pltpu.CompilerParams(dimension_semantics=("arbitrary",),
                                             vmem_limit_bytes=VMEM_LIMIT),
        name="dispatch",
    )(zstart, pos_flat, x1r)


def _ffn_kernel(te_ref, nu_ref, xs_ref, wgu_ref, bgu_ref, wdn_ref, bdn_ref, y_ref, wgu_b, wdn_b):
    t = pl.program_id(0)
    valid = t < nu_ref[0]
    fresh = jnp.logical_or(t == 0, te_ref[jnp.maximum(t - 1, 0)] != te_ref[t])

    @pl.when(jnp.logical_and(valid, fresh))
    def _():
        for r in range(0, D_MODEL, LANES):
            wgu_b[r:r + LANES, :] = wgu_ref[r:r + LANES, :].astype(BF16)
        for r in range(0, D_FF, LANES):
            wdn_b[r:r + LANES, :] = wdn_ref[r:r + LANES, :].astype(BF16)

    @pl.when(valid)
    def _():
        h = _dot(_slab_to_rows(xs_ref).astype(BF16), wgu_b[...]) + bgu_ref[...]
        glu = jnp.minimum(h[:, :D_FF], SWIGLU_LIMIT)
        lin = jnp.clip(h[:, D_FF:], -SWIGLU_LIMIT, SWIGLU_LIMIT)
        act = (lin + 1.0) * (glu * (1.0 / (1.0 + jnp.exp(-SWIGLU_ALPHA * glu))))
        _rows_to_slab(y_ref, _dot(act.astype(BF16), wdn_b[...]) + bdn_ref[...])

    @pl.when(jnp.logical_not(valid))
    def _():
        y_ref[...] = jnp.zeros(y_ref.shape, F32)


def _ffn(tile_e, n_used, xs, wgu, bgu, wdn, bdn, layer):
    tg = EXPERT_TG
    n_tiles = xs.shape[0] // tg
    wspec = lambda r, c: pl.BlockSpec((None, None, r, c), lambda t, te, nu: (layer, te[t], 0, 0))
    slab = pl.BlockSpec((tg, 8, LANES), lambda t, te, nu: (t, 0, 0))
    grid_spec = pltpu.PrefetchScalarGridSpec(
        num_scalar_prefetch=2,
        grid=(n_tiles,),
        in_specs=[slab, wspec(D_MODEL, 2 * D_FF), wspec(1, 2 * D_FF), wspec(D_FF, D_MODEL),
                  wspec(1, D_MODEL)],
        out_specs=slab,
        scratch_shapes=[pltpu.VMEM((D_MODEL, 2 * D_FF), BF16), pltpu.VMEM((D_FF, D_MODEL), BF16)],
    )
    return pl.pallas_call(
        _ffn_kernel,
        out_shape=jax.ShapeDtypeStruct(xs.shape, F32),
        grid_spec=grid_spec,
        compiler_params=pltpu.CompilerParams(dimension_semantics=("arbitrary",),
                                             vmem_limit_bytes=VMEM_LIMIT),
        name="experts",
    )(tile_e, n_used, xs, wgu, bgu, wdn, bdn)


def _gather_kernel(posc_ref, posn_ref, x1_ref, gt_ref, g2_ref, b2_ref, y_ref, o_ref, buf, sem, *, alpha):
    tm = x1_ref.shape[0]
    i = pl.program_id(0)
    slot = lax.rem(i, 2)

    def issue(pos_ref, sl):
        _issue_rows(tm, lambda tok, j: pltpu.make_async_copy(
            y_ref.at[pos_ref[tok * TOP_K + j]], buf.at[sl, j, tok], sem.at[sl]))

    @pl.when(i == 0)
    def _():
        issue(posc_ref, 0)

    @pl.when(i + 1 < pl.num_programs(0))
    def _():
        issue(posn_ref, 1 - slot)

    for j in range(TOP_K):
        pltpu.make_async_copy(y_ref.at[pl.ds(0, tm)], buf.at[slot, j], sem.at[slot]).wait()

    gates = [gt_ref[:, j:j + 1] for j in range(TOP_K)]
    cols = []
    for c in range(8):
        fc = jnp.zeros((tm, LANES), F32)
        for j in range(TOP_K):
            fc = fc + gates[j] * buf[slot, j, :, c, :]
        cols.append(fc)
    f = jnp.concatenate(cols, axis=1)
    o_ref[...] = _layer_norm(alpha * x1_ref[...] + f, g2_ref[...], b2_ref[...])


def _gather_rows(pos_flat, x1, gates, g2, b2, y, alpha):
    n = x1.shape[0]
    tm = min(COMBINE_TM, n)
    assert n % tm == 0
    nblk = n // tm
    smem = lambda f: pl.BlockSpec((tm * TOP_K,), lambda i: (f(i),), memory_space=pltpu.SMEM)
    return pl.pallas_call(
        functools.partial(_gather_kernel, alpha=alpha),
        out_shape=jax.ShapeDtypeStruct((n, D_MODEL), F32),
        grid=(nblk,),
        in_specs=[smem(lambda i: i), smem(lambda i: jnp.minimum(i + 1, nblk - 1)),
                  pl.BlockSpec((tm, D_MODEL), lambda i: (i, 0)),
                  pl.BlockSpec((tm, LANES), lambda i: (i, 0)),
                  _const_spec((1, D_MODEL)), _const_spec((1, D_MODEL)),
                  pl.BlockSpec(memory_space=pl.ANY)],
        out_specs=pl.BlockSpec((tm, D_MODEL), lambda i: (i, 0)),
        scratch_shapes=[pltpu.VMEM((2, TOP_K, tm, 8, LANES), F32), pltpu.SemaphoreType.DMA((2,))],
        compiler_params=pltpu.CompilerParams(dimension_semantics=("arbitrary",),
                                             vmem_limit_bytes=VMEM_LIMIT),
        name="combine",
    )(pos_flat, pos_flat, x1, gates, g2, b2, y)


def _sorted_positions(offsets, counts, n_used, ids, ranks, n_rows):
    e = ids[:, :TOP_K]
    onehot = e[..., None] == jnp.arange(N_EXPERTS, dtype=I32)
    pos = ranks[:, :TOP_K] + jnp.sum(jnp.where(onehot, offsets[:N_EXPERTS], 0), axis=-1)
    cnt = counts[0, :N_EXPERTS].astype(I32)
    zstart = jnp.minimum(offsets[:N_EXPERTS] + cnt, n_rows - EXPERT_TG)
    return pos.reshape(-1).astype(I32), jnp.concatenate([zstart.astype(I32), n_used])


def _rope_tables(positions):
    pos = positions.reshape(-1).astype(F32)[:, None]
    n = pos.shape[0]

    def cs(rot_dim):
        half = rot_dim // 2
        inv_freq = ROPE_THETA ** (-jnp.arange(half, dtype=F32) / half)
        ang = pos * inv_freq
        return jnp.cos(ang), jnp.sin(ang)

    cos_p, sin_p = cs(A_ROT_DIM)
    cos_c, sin_c = cs(C_ROPE_DIM)
    hp, hc = A_ROT_DIM // 2, C_ROPE_DIM // 2
    one = lambda w: jnp.ones((n, w), F32)
    zero = lambda w: jnp.zeros((n, w), F32)

    a64 = jnp.concatenate([cos_p, cos_p, one(64 - 2 * hp)] * 2, axis=1)
    bm64 = jnp.concatenate([-sin_p, zero(64 - hp)] * 2, axis=1)
    bp64 = jnp.concatenate([zero(hp), sin_p, zero(64 - 2 * hp)] * 2, axis=1)
    a128 = jnp.concatenate([one(64), cos_c, cos_c, one(32)], axis=1)
    bm128 = jnp.concatenate([zero(64), -sin_c, zero(64 - hc)], axis=1)
    bp128 = jnp.concatenate([zero(64 + hc), sin_c, zero(32)], axis=1)
    iw_scale = jnp.full((n, IDX_HEADS), IDX_HEADS ** -0.5, F32)
    as2 = jnp.concatenate([iw_scale, zero(64 - IDX_HEADS), cos_c, cos_c, zero(32)], axis=1)
    return {"t64": jnp.stack([a64, bm64, bp64]),
            "t128": jnp.stack([a128, bm128, bp128]),
            "ts2": jnp.stack([as2, bm128, bp128])}


def _layer_weights(l, w_in, gm_norm_g, gm_norm_b, gm_w_s, gm_b_s, mla_q_norm, mla_kv_norm,
                   mla_w_uq, mla_w_ukv, w_branch, w_out, ln1_g, ln1_b, router_w, router_b):
    w = w_in[l]
    o = 0
    cols = {}
    for name, width in (("aq", 512), ("ak", 512), ("av", 512), ("iq", 512), ("ik", 64), ("iw", 8),
                        ("buv", 1024), ("cdq", 256), ("cdkv", 128), ("ckr", 32), ("gate", 3072)):
        cols[name] = w[:, o:o + width]
        o += width
    zc = lambda width: jnp.zeros((D_MODEL, width), F32)
    s2 = jnp.concatenate([cols["iw"], zc(64 - IDX_HEADS), cols["ckr"], zc(32)], axis=1)
    wp = jnp.concatenate([cols["aq"] * (A_HEAD_DIM ** -0.5), cols["ak"],
                          cols["iq"] * (IDX_HEAD_DIM ** -0.5), cols["av"],
                          cols["ik"], cols["ik"], s2, cols["buv"], cols["cdq"], cols["cdkv"]], axis=1)

    uq = mla_w_uq[l].reshape(C_Q_RANK, C_HEADS, C_NOPE_DIM + C_ROPE_DIM)
    wuq = jnp.pad(uq, ((0, 0), (0, 0), (0, LANES - C_NOPE_DIM - C_ROPE_DIM))).reshape(C_Q_RANK, -1)
    ukv = mla_w_ukv[l].reshape(C_KV_RANK, C_HEADS, C_NOPE_DIM + C_V_DIM)
    wk = jnp.pad(ukv[:, :, :C_NOPE_DIM], ((0, 0), (0, 0), (0, LANES - C_NOPE_DIM))).reshape(C_KV_RANK, -1)
    wv = ukv[:, :, C_NOPE_DIM:].reshape(C_KV_RANK, -1)
    src = jnp.arange(LANES)[:, None]
    dst = jnp.arange(C_HEADS * LANES)[None, :]
    place = ((dst % LANES == src) & (src >= C_NOPE_DIM) & (src < C_NOPE_DIM + C_ROPE_DIM)).astype(F32)
    wke = jnp.concatenate([wk, place], axis=0)

    tril = jnp.tril(jnp.ones((B_CHUNK, B_CHUNK), dtype=bool))
    ws = jnp.where(tril[None], gm_w_s[l], 0)
    bsf = jnp.repeat(gm_b_s[l].T, B_GROUP_DIM, axis=1)
    rw = jnp.pad(router_w[l], ((0, 0), (0, LANES - N_EXPERTS)))
    rb = jnp.pad(router_b[l], (0, LANES - N_EXPERTS))[None, :]
    return {
        "wp": wp.astype(BF16), "wuq": wuq.astype(BF16), "wke": wke.astype(BF16), "wv": wv.astype(BF16),
        "ws": ws.astype(BF16), "gmg": gm_norm_g[l][None, :], "gmb": gm_norm_b[l][None, :], "bsf": bsf,
        "qn": mla_q_norm[l][None, :], "kvn": mla_kv_norm[l][None, :],
        "wg": cols["gate"].astype(BF16), "wbr": w_branch[l].astype(BF16), "wo": w_out[l].astype(BF16),
        "g1": ln1_g[l][None, :], "b1": ln1_b[l][None, :], "rw": rw, "rb": rb,
    }


def _expert_tiles(counts, n_tiles):
    tg = EXPERT_TG
    cnt = counts[0, :N_EXPERTS].astype(I32)
    tiles_per = (cnt + tg - 1) // tg
    tile_end = jnp.cumsum(tiles_per)
    offsets = (tile_end - tiles_per) * tg
    n_used = tile_end[-1]
    tid = jnp.minimum(jnp.arange(n_tiles, dtype=I32), n_used - 1)
    tile_e = jnp.sum(tid[:, None] >= tile_end[None, :], axis=1).astype(I32)
    off128 = jnp.pad(offsets, (0, LANES - N_EXPERTS)).astype(I32)
    return off128, tile_e, n_used.reshape(1).astype(I32)


def _layer(l, x2, tabs, batch, seq, depth, p):
    alpha = (2 * depth) ** 0.25
    n = x2.shape[0]
    wts = _layer_weights(l, p["w_in"], p["gm_norm_g"], p["gm_norm_b"], p["gm_w_s"], p["gm_b_s"],
                         p["mla_q_norm"], p["mla_kv_norm"], p["mla_w_uq"], p["mla_w_ukv"],
                         p["w_branch"], p["w_out"], p["ln1_g"], p["ln1_b"], p["router_w"], p["router_b"])
    aq, ak, iq, avt, ik2, iwt, o_b, q, k, vt = _proj(x2, wts, tabs)
    o_a = _dsa(iq, iwt, aq, ik2, ak, avt, batch, seq)
    o_c = _mla(q, k, vt, batch, seq)
    x1, x1r, ids, ranks, gates, counts = _merge(x2, o_a, o_b, o_c, wts, alpha)

    n_tiles = n * TOP_K // EXPERT_TG + N_EXPERTS
    offsets, tile_e, n_used = _expert_tiles(counts, n_tiles)
    n_rows = n_tiles * EXPERT_TG
    pos_flat, zstart = _sorted_positions(offsets, counts, n_used, ids, ranks, n_rows)
    xs = _scatter_rows(zstart, pos_flat, x1r, n_rows)
    y = _ffn(tile_e, n_used, xs, p["exp_w_gu"], p["exp_b_gu"][:, :, None, :],
             p["exp_w_dn"], p["exp_b_dn"][:, :, None, :], l)
    return _gather_rows(pos_flat, x1, gates, p["ln2_g"][l][None, :], p["ln2_b"][l][None, :], y, alpha)


def kernel(x, positions, w_in, gm_norm_g, gm_norm_b, gm_w_s, gm_b_s, mla_q_norm, mla_kv_norm, mla_w_uq, mla_w_ukv, w_branch, w_out, ln1_g, ln1_b, router_w, router_b, exp_w_gu, exp_b_gu, exp_w_dn, exp_b_dn, ln2_g, ln2_b):
    batch, seq, _ = x.shape
    depth = w_in.shape[0]
    p = dict(w_in=w_in, gm_norm_g=gm_norm_g, gm_norm_b=gm_norm_b, gm_w_s=gm_w_s, gm_b_s=gm_b_s,
             mla_q_norm=mla_q_norm, mla_kv_norm=mla_kv_norm, mla_w_uq=mla_w_uq, mla_w_ukv=mla_w_ukv,
             w_branch=w_branch, w_out=w_out, ln1_g=ln1_g, ln1_b=ln1_b, router_w=router_w,
             router_b=router_b, exp_w_gu=exp_w_gu, exp_b_gu=exp_b_gu, exp_w_dn=exp_w_dn,
             exp_b_dn=exp_b_dn, ln2_g=ln2_g, ln2_b=ln2_b)
    tabs = _rope_tables(positions)
    x2 = x.reshape(batch * seq, D_MODEL)
    for l in range(depth):
        x2 = _layer(l, x2, tabs, batch, seq, depth, p)
    return x2.reshape(batch, seq, D_MODEL)
```

```python
import functools
import math

import jax
import jax.numpy as jnp
from jax import lax
from jax.experimental import pallas as pl
from jax.experimental.pallas import tpu as pltpu

F32 = jnp.float32
BF16 = jnp.bfloat16
I32 = jnp.int32

D_MODEL = 1024
BRANCH_WIDTH = 512
N_BRANCH = 3
ROPE_THETA = 500000.0
LN_EPS = 1e-5
RMS_EPS = 1e-6

A_HEADS = 8
A_HEAD_DIM = 64
A_ROT_DIM = 16
IDX_HEADS = 8
IDX_HEAD_DIM = 64
IDX_TOPK_MAX = 256

B_GROUPS = 8
B_GROUP_DIM = 64
B_CHUNK = 128

C_HEADS = 8
C_NOPE_DIM = 64
C_ROPE_DIM = 32
C_V_DIM = 64
C_Q_RANK = 256
C_KV_RANK = 128

N_EXPERTS = 32
TOP_K = 4
D_FF = 1024
SWIGLU_LIMIT = 7.0
SWIGLU_ALPHA = 1.702

LANES = 128
ATT_T = 512
PROJ_TM = 512
MERGE_TM = 256
DISPATCH_TM = 256
COMBINE_TM = 256
EXPERT_TG = 256
MOE_COLS = 256
VMEM_LIMIT = 56 * 1024 * 1024

FLASH_LOOKAHEAD = 3
NEG = -1e30
INT_MIN = -(2 ** 31)

_C_AQ, _C_AK, _C_IQ, _C_AV = 0, 512, 1024, 1536
_C_S1, _C_S2, _C_B, _C_C, _C_END = 2048, 2176, 2304, 3328, 3712


def _dot(a, b):
    return jnp.dot(a, b, preferred_element_type=F32)


def _dot_nt(a, b):
    return lax.dot_general(a, b, (((1,), (1,)), ((), ())), preferred_element_type=F32)


def _rot(y, a, bm, bp, r):
    w = y.shape[1]
    return y * a + pltpu.roll(y, w - r, 1) * bm + pltpu.roll(y, r, 1) * bp


def _layer_norm(x, g, b):
    mu = jnp.mean(x, axis=-1, keepdims=True)
    xc = x - mu
    var = jnp.mean(xc * xc, axis=-1, keepdims=True)
    return xc * lax.rsqrt(var + LN_EPS) * g + b


def _rms_norm(x, g):
    return x * lax.rsqrt(jnp.mean(x * x, axis=-1, keepdims=True) + RMS_EPS) * g


def _const_spec(shape):
    nd = len(shape)
    return pl.BlockSpec(shape, lambda *_: (0,) * nd)


def _proj_kernel(x_ref, wp_ref, wuq_ref, wke_ref, wv_ref, ws_ref, gmg_ref, gmb_ref, bsf_ref,
                 qn_ref, kvn_ref, t64_ref, t128_ref, ts2_ref,
                 aq_o, ak_o, iq_o, avt_o, ik2_o, iwt_o, ob_o, q_o, k_o, vt_o):
    tm = x_ref.shape[0]
    xb = x_ref[...].astype(BF16)

    def mm(c0, c1):
        return _dot(xb, wp_ref[:, c0:c1])

    a64, bm64, bp64 = t64_ref[0], t64_ref[1], t64_ref[2]
    a4 = jnp.concatenate([a64] * 4, axis=1)
    bm4 = jnp.concatenate([bm64] * 4, axis=1)
    bp4 = jnp.concatenate([bp64] * 4, axis=1)
    for c0, out in ((_C_AQ, aq_o), (_C_AK, ak_o), (_C_IQ, iq_o)):
        out[...] = _rot(mm(c0, c0 + 512), a4, bm4, bp4, A_ROT_DIM // 2).astype(BF16)
    avt_o[...] = mm(_C_AV, _C_AV + 512).T.astype(BF16)
    ik2_o[...] = _rot(mm(_C_S1, _C_S1 + LANES), a64, bm64, bp64, A_ROT_DIM // 2).astype(BF16)
    sm2 = _rot(mm(_C_S2, _C_S2 + LANES), ts2_ref[0], ts2_ref[1], ts2_ref[2], C_ROPE_DIM // 2)
    iwt_o[...] = sm2.T[:IDX_HEADS, :]

    buv = mm(_C_B, _C_B + 2 * BRANCH_WIDTH)
    buv = 0.5 * buv * (1.0 + lax.erf(buv * (1.0 / math.sqrt(2.0))))
    u = buv[:, :BRANCH_WIDTH]
    v = _layer_norm(buv[:, BRANCH_WIDTH:], gmg_ref[...], gmb_ref[...]).astype(BF16)
    lane = lax.broadcasted_iota(I32, (B_CHUNK, LANES), 1)
    for c in range(tm // B_CHUNK):
        rows = slice(c * B_CHUNK, (c + 1) * B_CHUNK)
        for pr in range(B_GROUPS // 2):
            cols = slice(pr * LANES, (pr + 1) * LANES)
            vp = v[rows, cols]
            s = jnp.where(lane < B_GROUP_DIM, _dot(ws_ref[2 * pr], vp), _dot(ws_ref[2 * pr + 1], vp))
            ob_o[rows, cols] = (u[rows, cols] * (s + bsf_ref[:, cols])).astype(BF16)

    cc = mm(_C_C, _C_END)
    cq = _rms_norm(cc[:, :C_Q_RANK], qn_ref[...]).astype(BF16)
    a8 = jnp.concatenate([t128_ref[0]] * C_HEADS, axis=1)
    bm8 = jnp.concatenate([t128_ref[1]] * C_HEADS, axis=1)
    bp8 = jnp.concatenate([t128_ref[2]] * C_HEADS, axis=1)
    q_o[...] = _rot(_dot(cq, wuq_ref[...]), a8, bm8, bp8, C_ROPE_DIM // 2).astype(BF16)
    ckv = _rms_norm(cc[:, C_Q_RANK:], kvn_ref[...]).astype(BF16)
    kin = jnp.concatenate([ckv, sm2.astype(BF16)], axis=1)
    k_o[...] = _dot(kin, wke_ref[...]).astype(BF16)
    vt_o[...] = _dot(ckv, wv_ref[...]).T.astype(BF16)


def _proj(x2, wts, tabs):
    n = x2.shape[0]
    tm = min(PROJ_TM, n)
    assert n % tm == 0 and tm % B_CHUNK == 0
    row = lambda w: pl.BlockSpec((tm, w), lambda i: (i, 0))
    col = lambda h: pl.BlockSpec((h, tm), lambda i: (0, i))
    tab = pl.BlockSpec((3, tm, LANES), lambda i: (0, i, 0))
    in_specs = [
        row(D_MODEL),
        _const_spec((D_MODEL, _C_END)),
        _const_spec((C_Q_RANK, C_HEADS * LANES)),
        _const_spec((C_KV_RANK + LANES, C_HEADS * LANES)),
        _const_spec((C_KV_RANK, C_HEADS * C_V_DIM)),
        _const_spec((B_GROUPS, B_CHUNK, B_CHUNK)),
        _const_spec((1, BRANCH_WIDTH)),
        _const_spec((1, BRANCH_WIDTH)),
        _const_spec((B_CHUNK, BRANCH_WIDTH)),
        _const_spec((1, C_Q_RANK)),
        _const_spec((1, C_KV_RANK)),
        tab, tab, tab,
    ]
    out_shape = [
        jax.ShapeDtypeStruct((n, 512), BF16),
        jax.ShapeDtypeStruct((n, 512), BF16),
        jax.ShapeDtypeStruct((n, 512), BF16),
        jax.ShapeDtypeStruct((512, n), BF16),
        jax.ShapeDtypeStruct((n, LANES), BF16),
        jax.ShapeDtypeStruct((IDX_HEADS, n), F32),
        jax.ShapeDtypeStruct((n, 512), BF16),
        jax.ShapeDtypeStruct((n, C_HEADS * LANES), BF16),
        jax.ShapeDtypeStruct((n, C_HEADS * LANES), BF16),
        jax.ShapeDtypeStruct((C_HEADS * C_V_DIM, n), BF16),
    ]
    out_specs = [row(512), row(512), row(512), col(512), row(LANES), col(IDX_HEADS), row(512),
                 row(C_HEADS * LANES), row(C_HEADS * LANES), col(C_HEADS * C_V_DIM)]
    return pl.pallas_call(
        _proj_kernel,
        out_shape=out_shape,
        grid=(n // tm,),
        in_specs=in_specs,
        out_specs=out_specs,
        compiler_params=pltpu.CompilerParams(dimension_semantics=("arbitrary",),
                                             vmem_limit_bytes=VMEM_LIMIT),
        name="proj",
    )(x2, wts["wp"], wts["wuq"], wts["wke"], wts["wv"], wts["ws"], wts["gmg"], wts["gmb"],
      wts["bsf"], wts["qn"], wts["kvn"], tabs["t64"], tabs["t128"], tabs["ts2"])


def _rep8(x):
    return jnp.broadcast_to(x, (8, x.shape[-1]))


def _fold8(x3, op):
    return _rep8(op(op(x3, axis=0), axis=0, keepdims=True))


def _split8(x):
    return x.reshape(x.shape[0] // 8, 8, x.shape[1])


def _flash_init(m_ref, l_ref, acc_ref):
    m_ref[...] = jnp.full(m_ref.shape, NEG, F32)
    l_ref[...] = jnp.zeros(l_ref.shape, F32)
    acc_ref[...] = jnp.zeros(acc_ref.shape, F32)


def _flash_update(s3, vt, h, m_ref, l_ref, acc_ref, c2):
    n, _, tq = s3.shape
    m_old = m_ref[h]
    m_new = jnp.maximum(m_old, _fold8(s3, jnp.max))
    alpha = jnp.exp2((m_old - m_new) * c2)
    p = jnp.exp2((s3 - m_new[None]) * c2)
    m_ref[h] = m_new
    l_ref[h] = alpha * l_ref[h] + _fold8(p, jnp.sum)
    pv = _dot(vt, p.reshape(n * 8, tq).astype(BF16))
    acc_ref[h] = (_split8(acc_ref[h]) * alpha[None]).reshape(pv.shape) + pv


def _flash_chunk(logits, values, heads, m_ref, l_ref, acc_ref, c2):
    s = {h: logits(h) for h in range(min(FLASH_LOOKAHEAD, heads))}
    for h in range(heads):
        if h + FLASH_LOOKAHEAD < heads:
            s[h + FLASH_LOOKAHEAD] = logits(h + FLASH_LOOKAHEAD)
        _flash_update(s.pop(h), values(h), h, m_ref, l_ref, acc_ref, c2)


def _flash_finish(o_ref, l_ref, acc_ref, heads):
    outs = [(_split8(acc_ref[h]) / l_ref[h][None]).reshape(acc_ref.shape[1:]) for h in range(heads)]
    o_ref[...] = jnp.concatenate(outs, axis=0).T.astype(BF16)


def _dsa_kernel(iq_ref, iwt_ref, aq_ref, ik2_ref, ak_ref, avt_ref, o_ref,
                key_ref, iqm_ref, aqm_ref, thr_ref, m_ref, l_ref, acc_ref, *, n_sel):
    t = iq_ref.shape[0]
    i = pl.program_id(1)
    nchunk = i + 1
    lane = lax.broadcasted_iota(I32, (t, LANES), 1)
    lo = lane < IDX_HEAD_DIM
    krow = lax.broadcasted_iota(I32, (t, t), 0)
    qcol = lax.broadcasted_iota(I32, (t, t), 1)

    for h in range(IDX_HEADS):
        cols = slice((h // 2) * LANES, (h // 2 + 1) * LANES)
        keep = lo if h % 2 == 0 else jnp.logical_not(lo)
        iqm_ref[h] = jnp.where(keep, iq_ref[:, cols], jnp.zeros((t, LANES), BF16))
        aqm_ref[h] = jnp.where(keep, aq_ref[:, cols], jnp.zeros((t, LANES), BF16))

    def rows(c):
        return pl.ds(pl.multiple_of(c * t, t), t)

    def causal_at(c):
        return (krow + c * t) <= (qcol + i * t)

    def idx_body(c, carry):
        kc = ik2_ref[rows(c), :]
        acc = jnp.zeros((t // 8, 8, t), F32)
        for h in range(IDX_HEADS):
            w8 = _rep8(iwt_ref[h:h + 1, :])
            acc = acc + _split8(jnp.maximum(_dot_nt(kc, iqm_ref[h]), 0.0)) * w8[None]
        bits = lax.bitcast_convert_type(acc.reshape(t, t), I32)
        key = bits ^ ((bits >> 31) & 0x7FFFFFFF)
        key_ref[rows(c), :] = jnp.where(causal_at(c), key, INT_MIN)
        return carry

    lax.fori_loop(0, nchunk, idx_body, 0)

    def count(indicator):
        def body(c, acc):
            return acc + jnp.sum(indicator(_split8(key_ref[rows(c), :])), axis=0)
        part = lax.fori_loop(0, nchunk, body, jnp.zeros((8, t), I32))
        return _rep8(jnp.sum(part, axis=0, keepdims=True))

    def bit_body(it, prefix):
        trial_u = prefix | jnp.left_shift(jnp.int32(1), 31 - it)
        trial_s = trial_u ^ INT_MIN
        cnt = count(lambda kc: jnp.where(kc >= trial_s[None], 1, 0))
        return jnp.where(cnt >= n_sel, trial_u, prefix)

    thr = lax.fori_loop(0, 32, bit_body, jnp.zeros((8, t), I32)) ^ INT_MIN
    cnt_ge = count(lambda kc: jnp.where(kc >= thr[None], 1, 0))
    tied = jnp.where(cnt_ge > n_sel, jnp.where(thr != INT_MIN, 1, 0), 0)

    @pl.when(jnp.max(tied) > 0)
    def _():
        need = (n_sel - count(lambda kc: jnp.where(kc > thr[None], 1, 0))).astype(F32)
        lower = jnp.where(qcol < krow, 1.0, 0.0).astype(BF16)

        def body(c, seen):
            kc = _split8(key_ref[rows(c), :])
            eq = jnp.where(kc == thr[None], 1.0, 0.0)
            rank = _split8(_dot(lower, eq.reshape(t, t).astype(BF16))) + seen[None]
            drop = jnp.where(rank >= need[None], eq, 0.0)
            key_ref[rows(c), :] = jnp.where(drop > 0.0, INT_MIN, kc).reshape(t, t)
            return seen + _fold8(eq, jnp.sum)

        lax.fori_loop(0, nchunk, body, jnp.zeros((8, t), F32))

    thr_ref[...] = jnp.maximum(thr, INT_MIN + 1)

    _flash_init(m_ref, l_ref, acc_ref)
    log2e = math.log2(math.e)

    def att_body(c, carry):
        sel = _split8(key_ref[rows(c), :]) >= thr_ref[...][None]

        def logits(h):
            cols = slice((h // 2) * LANES, (h // 2 + 1) * LANES)
            return jnp.where(sel, _split8(_dot_nt(ak_ref[rows(c), cols], aqm_ref[h])), NEG)

        def values(h):
            return avt_ref[h * A_HEAD_DIM:(h + 1) * A_HEAD_DIM, rows(c)]

        _flash_chunk(logits, values, A_HEADS, m_ref, l_ref, acc_ref, log2e)
        return carry

    lax.fori_loop(0, nchunk, att_body, 0)
    _flash_finish(o_ref, l_ref, acc_ref, A_HEADS)


def _dsa(iq, iwt, aq, ik2, ak, avt, batch, seq):
    t = min(ATT_T, seq)
    nq = seq // t
    n_sel = min(IDX_TOPK_MAX, seq // 4)
    assert seq % t == 0 and t % LANES == 0
    qspec = lambda w: pl.BlockSpec((t, w), lambda b, i: (b * nq + i, 0))
    kspec = lambda w: pl.BlockSpec((seq, w), lambda b, i: (b, 0))
    return pl.pallas_call(
        functools.partial(_dsa_kernel, n_sel=n_sel),
        out_shape=jax.ShapeDtypeStruct((batch * seq, 512), BF16),
        grid=(batch, nq),
        in_specs=[qspec(512),
                  pl.BlockSpec((IDX_HEADS, t), lambda b, i: (0, b * nq + i)),
                  qspec(512), kspec(LANES), kspec(512),
                  pl.BlockSpec((512, seq), lambda b, i: (0, b))],
        out_specs=qspec(512),
        scratch_shapes=[
            pltpu.VMEM((seq, t), I32),
            pltpu.VMEM((IDX_HEADS, t, LANES), BF16),
            pltpu.VMEM((A_HEADS, t, LANES), BF16),
            pltpu.VMEM((8, t), I32),
            pltpu.VMEM((A_HEADS, 8, t), F32),
            pltpu.VMEM((A_HEADS, 8, t), F32),
            pltpu.VMEM((A_HEADS, A_HEAD_DIM, t), F32),
        ],
        compiler_params=pltpu.CompilerParams(dimension_semantics=("arbitrary", "arbitrary"),
                                             vmem_limit_bytes=VMEM_LIMIT),
        name="dsa",
    )(iq, iwt, aq, ik2, ak, avt)


def _mla_kernel(q_ref, k_ref, vt_ref, o_ref, m_ref, l_ref, acc_ref):
    t = q_ref.shape[0]
    i = pl.program_id(1)
    krow = lax.broadcasted_iota(I32, (t // 8, 8, t), 0) * 8 + lax.broadcasted_iota(I32, (t // 8, 8, t), 1)
    qcol = lax.broadcasted_iota(I32, (t // 8, 8, t), 2)
    c2 = (C_NOPE_DIM + C_ROPE_DIM) ** -0.5 * math.log2(math.e)
    _flash_init(m_ref, l_ref, acc_ref)

    def chunk(c, masked):
        rows = pl.ds(pl.multiple_of(c * t, t), t)

        def logits(h):
            cols = slice(h * LANES, (h + 1) * LANES)
            s3 = _split8(_dot_nt(k_ref[rows, cols], q_ref[:, cols]))
            return jnp.where(krow <= qcol, s3, NEG) if masked else s3

        def values(h):
            return vt_ref[h * C_V_DIM:(h + 1) * C_V_DIM, rows]

        _flash_chunk(logits, values, C_HEADS, m_ref, l_ref, acc_ref, c2)

    def body(c, carry):
        chunk(c, False)
        return carry

    lax.fori_loop(0, i, body, 0)
    chunk(i, True)
    _flash_finish(o_ref, l_ref, acc_ref, C_HEADS)


def _mla(q, k, vt, batch, seq):
    t = min(ATT_T, seq)
    nq = seq // t
    return pl.pallas_call(
        _mla_kernel,
        out_shape=jax.ShapeDtypeStruct((batch * seq, C_HEADS * C_V_DIM), BF16),
        grid=(batch, nq),
        in_specs=[pl.BlockSpec((t, C_HEADS * LANES), lambda b, i: (b * nq + i, 0)),
                  pl.BlockSpec((seq, C_HEADS * LANES), lambda b, i: (b, 0)),
                  pl.BlockSpec((C_HEADS * C_V_DIM, seq), lambda b, i: (0, b))],
        out_specs=pl.BlockSpec((t, C_HEADS * C_V_DIM), lambda b, i: (b * nq + i, 0)),
        scratch_shapes=[pltpu.VMEM((C_HEADS, 8, t), F32),
                        pltpu.VMEM((C_HEADS, 8, t), F32),
                        pltpu.VMEM((C_HEADS, C_V_DIM, t), F32)],
        compiler_params=pltpu.CompilerParams(dimension_semantics=("arbitrary", "arbitrary"),
                                             vmem_limit_bytes=VMEM_LIMIT),
        name="mla",
    )(q, k, vt)


def _merge_kernel(x_ref, oa_ref, ob_ref, oc_ref, wg_ref, wbr_ref, wo_ref, g1_ref, b1_ref,
                  rw_ref, rb_ref, x1_o, x1r_o, ids_o, rk_o, gt_o, cnt_o, run_ref, *, alpha):
    tm = x_ref.shape[0]

    @pl.when(pl.program_id(0) == 0)
    def _():
        run_ref[...] = jnp.zeros_like(run_ref)

    x = x_ref[...]
    xb = x.astype(BF16)
    merged = jnp.zeros((tm, D_MODEL), F32)
    for n, o_ref in enumerate((oa_ref, ob_ref, oc_ref)):
        z = _dot(xb, wg_ref[:, n * D_MODEL:(n + 1) * D_MODEL])
        merged = merged + (1.0 / (1.0 + jnp.exp(-z))) * _dot(o_ref[...], wbr_ref[n])
    y = _dot(merged.astype(BF16), wo_ref[...])
    x1 = _layer_norm(alpha * x + y, g1_ref[...], b1_ref[...])
    x1_o[...] = x1
    x1r_o[...] = _rows_to_slab(x1)

    lane = lax.broadcasted_iota(I32, (tm, LANES), 1)
    logits = jnp.dot(x1, rw_ref[...], precision=lax.Precision.HIGHEST,
                     preferred_element_type=F32) + rb_ref[...]
    lg = jnp.where(lane < N_EXPERTS, logits, -jnp.inf)
    ids, vals = [], []
    for _ in range(TOP_K):
        mx = jnp.max(lg, axis=1, keepdims=True)
        idx = jnp.min(jnp.where(lg == mx, lane, LANES), axis=1, keepdims=True)
        ids.append(idx)
        vals.append(mx)
        lg = jnp.where(lane == idx, -jnp.inf, lg)
    es = [jnp.exp(v - vals[0]) for v in vals]
    den = es[0] + es[1] + es[2] + es[3]

    hot = jnp.zeros((tm, LANES), F32)
    for idx in ids:
        hot = hot + jnp.where(lane == idx, 1.0, 0.0)
    r2 = lax.broadcasted_iota(I32, (tm, tm), 0)
    c2 = lax.broadcasted_iota(I32, (tm, tm), 1)
    lower = jnp.where(c2 < r2, 1.0, 0.0).astype(BF16)
    base = run_ref[0:1, :] + _dot(lower, hot.astype(BF16))
    ids_v = jnp.zeros((tm, LANES), I32)
    rk_v = jnp.zeros((tm, LANES), I32)
    gt_v = jnp.zeros((tm, LANES), F32)
    for j in range(TOP_K):
        rank = jnp.sum(jnp.where(lane == ids[j], base, 0.0), axis=1, keepdims=True)
        ids_v = jnp.where(lane == j, ids[j], ids_v)
        rk_v = jnp.where(lane == j, rank.astype(I32), rk_v)
        gt_v = jnp.where(lane == j, es[j] / den, gt_v)
    ids_o[...] = ids_v
    rk_o[...] = rk_v
    gt_o[...] = gt_v
    run = run_ref[0:1, :] + jnp.sum(hot, axis=0, keepdims=True)
    run_ref[...] = jnp.broadcast_to(run, run_ref.shape)
    cnt_o[...] = jnp.broadcast_to(run, cnt_o.shape)


def _merge(x2, oa, ob, oc, wts, alpha):
    n = x2.shape[0]
    tm = min(MERGE_TM, n)
    assert n % tm == 0
    row = lambda w: pl.BlockSpec((tm, w), lambda i: (i, 0))
    in_specs = [row(D_MODEL), row(512), row(512), row(512),
                _const_spec((D_MODEL, N_BRANCH * D_MODEL)),
                _const_spec((N_BRANCH, BRANCH_WIDTH, D_MODEL)),
                _const_spec((D_MODEL, D_MODEL)),
                _const_spec((1, D_MODEL)), _const_spec((1, D_MODEL)),
                _const_spec((D_MODEL, LANES)), _const_spec((1, LANES))]
    out_shape = [jax.ShapeDtypeStruct((n, D_MODEL), F32),
                 jax.ShapeDtypeStruct((n, 8, LANES), F32),
                 jax.ShapeDtypeStruct((n, LANES), I32),
                 jax.ShapeDtypeStruct((n, LANES), I32),
                 jax.ShapeDtypeStruct((n, LANES), F32),
                 jax.ShapeDtypeStruct((8, LANES), F32)]
    out_specs = [row(D_MODEL), pl.BlockSpec((tm, 8, LANES), lambda i: (i, 0, 0)),
                 row(LANES), row(LANES), row(LANES), _const_spec((8, LANES))]
    return pl.pallas_call(
        functools.partial(_merge_kernel, alpha=alpha),
        out_shape=out_shape,
        grid=(n // tm,),
        in_specs=in_specs,
        out_specs=out_specs,
        scratch_shapes=[pltpu.VMEM((8, LANES), F32)],
        compiler_params=pltpu.CompilerParams(dimension_semantics=("arbitrary",),
                                             vmem_limit_bytes=VMEM_LIMIT),
        name="merge",
    )(x2, oa, ob, oc, wts["wg"], wts["wbr"], wts["wo"], wts["g1"], wts["b1"], wts["rw"], wts["rb"])


def _row_copy(src_ref, src_row, dst_ref, dst_row, sem):
    return pltpu.make_async_copy(src_ref.at[pl.ds(src_row, 1), :], dst_ref.at[pl.ds(dst_row, 1), :], sem)


def _dispatch_kernel(off_ref, ids_ref, rk_ref, x_ref, xs_in_ref, xs_ref, sem):
    del xs_in_ref
    tm = x_ref.shape[0]

    def issue(tok, carry):
        for j in range(TOP_K):
            a = tok * TOP_K + j
            _row_copy(x_ref, tok, xs_ref, off_ref[ids_ref[a]] + rk_ref[a], sem).start()
        return carry

    lax.fori_loop(0, tm, issue, 0)

    def drain(a, carry):
        _row_copy(x_ref, 0, xs_ref, 0, sem).wait()
        return carry

    lax.fori_loop(0, tm * TOP_K, drain, 0)


def _dispatch(offsets, ids_flat, rk_flat, x1, xs_init):
    n = x1.shape[0]
    tm = min(DISPATCH_TM, n)
    assert n % tm == 0
    smem = pl.BlockSpec((tm * TOP_K,), lambda i, off: (i,), memory_space=pltpu.SMEM)
    grid_spec = pltpu.PrefetchScalarGridSpec(
        num_scalar_prefetch=1,
        grid=(n // tm,),
        in_specs=[smem, smem,
                  pl.BlockSpec((tm, D_MODEL), lambda i, off: (i, 0)),
                  pl.BlockSpec(memory_space=pl.ANY)],
        out_specs=pl.BlockSpec(memory_space=pl.ANY),
        scratch_shapes=[pltpu.SemaphoreType.DMA],
    )
    return pl.pallas_call(
        _dispatch_kernel,
        out_shape=jax.ShapeDtypeStruct(xs_init.shape, xs_init.dtype),
        grid_spec=grid_spec,
        input_output_aliases={4: 0},
        compiler_params=pltpu.CompilerParams(dimension_semantics=("arbitrary",),
                                             vmem_limit_bytes=VMEM_LIMIT),
        name="dispatch",
    )(offsets, ids_flat, rk_flat, x1, xs_init)


def _expert_kernel(te_ref, nu_ref, xs_ref, wgu_ref, bgu_ref, wdn_ref, bdn_ref, y_ref,
                   wgu_b, wdn_b):
    t = pl.program_id(0)
    e = te_ref[t]
    valid = t < nu_ref[0]
    fresh = jnp.logical_or(t == 0, te_ref[jnp.maximum(t - 1, 0)] != e)

    @pl.when(jnp.logical_and(valid, fresh))
    def _():
        for r in range(0, D_MODEL, LANES):
            wgu_b[r:r + LANES, :] = wgu_ref[r:r + LANES, :].astype(BF16)
        for r in range(0, D_FF, LANES):
            wdn_b[r:r + LANES, :] = wdn_ref[r:r + LANES, :].astype(BF16)

    @pl.when(valid)
    def _():
        h = _dot(xs_ref[...].astype(BF16), wgu_b[...]) + bgu_ref[...]
        glu = jnp.minimum(h[:, :D_FF], SWIGLU_LIMIT)
        lin = jnp.clip(h[:, D_FF:], -SWIGLU_LIMIT, SWIGLU_LIMIT)
        act = (lin + 1.0) * (glu * (1.0 / (1.0 + jnp.exp(-SWIGLU_ALPHA * glu))))
        y_ref[...] = _dot(act.astype(BF16), wdn_b[...]) + bdn_ref[...]

    @pl.when(jnp.logical_not(valid))
    def _():
        y_ref[...] = jnp.zeros_like(y_ref)


def _experts(tile_e, n_used, xs, wgu, bgu, wdn, bdn, layer):
    p = xs.shape[0]
    tg = EXPERT_TG
    n_tiles = p // tg
    wspec = lambda r, c: pl.BlockSpec((None, None, r, c), lambda t, te, nu: (layer, te[t], 0, 0))
    grid_spec = pltpu.PrefetchScalarGridSpec(
        num_scalar_prefetch=2,
        grid=(n_tiles,),
        in_specs=[pl.BlockSpec((tg, D_MODEL), lambda t, te, nu: (t, 0)),
                  wspec(D_MODEL, 2 * D_FF), wspec(1, 2 * D_FF), wspec(D_FF, D_MODEL), wspec(1, D_MODEL)],
        out_specs=pl.BlockSpec((tg, D_MODEL), lambda t, te, nu: (t, 0)),
        scratch_shapes=[pltpu.VMEM((D_MODEL, 2 * D_FF), BF16), pltpu.VMEM((D_FF, D_MODEL), BF16)],
    )
    return pl.pallas_call(
        _expert_kernel,
        out_shape=jax.ShapeDtypeStruct((p, D_MODEL), F32),
        grid_spec=grid_spec,
        compiler_params=pltpu.CompilerParams(dimension_semantics=("arbitrary",),
                                             vmem_limit_bytes=VMEM_LIMIT),
        name="experts",
    )(tile_e, n_used, xs, wgu, bgu, wdn, bdn)


def _combine_kernel(off_ref, ids_ref, rk_ref, x1_ref, gt_ref, g2_ref, b2_ref, y_ref, o_ref,
                    buf, sem, *, alpha):
    tm = x1_ref.shape[0]

    def issue(tok, carry):
        for j in range(TOP_K):
            a = tok * TOP_K + j
            _row_copy(y_ref, off_ref[ids_ref[a]] + rk_ref[a], buf.at[j], tok, sem).start()
        return carry

    lax.fori_loop(0, tm, issue, 0)

    def drain(a, carry):
        _row_copy(y_ref, 0, buf.at[0], 0, sem).wait()
        return carry

    lax.fori_loop(0, tm * TOP_K, drain, 0)

    f = jnp.zeros((tm, D_MODEL), F32)
    for j in range(TOP_K):
        f = f + gt_ref[:, j:j + 1] * buf[j]
    o_ref[...] = _layer_norm(alpha * x1_ref[...] + f, g2_ref[...], b2_ref[...])


def _combine(offsets, ids_flat, rk_flat, x1, gates, g2, b2, y, alpha):
    n = x1.shape[0]
    tm = min(COMBINE_TM, n)
    assert n % tm == 0
    smem = pl.BlockSpec((tm * TOP_K,), lambda i, off: (i,), memory_space=pltpu.SMEM)
    grid_spec = pltpu.PrefetchScalarGridSpec(
        num_scalar_prefetch=1,
        grid=(n // tm,),
        in_specs=[smem, smem,
                  pl.BlockSpec((tm, D_MODEL), lambda i, off: (i, 0)),
                  pl.BlockSpec((tm, LANES), lambda i, off: (i, 0)),
                  pl.BlockSpec((1, D_MODEL), lambda i, off: (0, 0)),
                  pl.BlockSpec((1, D_MODEL), lambda i, off: (0, 0)),
                  pl.BlockSpec(memory_space=pl.ANY)],
        out_specs=pl.BlockSpec((tm, D_MODEL), lambda i, off: (i, 0)),
        scratch_shapes=[pltpu.VMEM((TOP_K, tm, D_MODEL), F32), pltpu.SemaphoreType.DMA],
    )
    return pl.pallas_call(
        functools.partial(_combine_kernel, alpha=alpha),
        out_shape=jax.ShapeDtypeStruct((n, D_MODEL), F32),
        grid_spec=grid_spec,
        compiler_params=pltpu.CompilerParams(dimension_semantics=("arbitrary",),
                                             vmem_limit_bytes=VMEM_LIMIT),
        name="combine",
    )(offsets, ids_flat, rk_flat, x1, gates, g2, b2, y)


def _moe_kernel(te_ref, nu_ref, srcc_ref, srcn_ref, dstp_ref, dstc_ref,
                x_hbm, wgu_ref, bgu_ref, wdn_ref, bdn_ref, y_hbm,
                wgu_b, wdn_b, xbuf, ybuf, h_ref, act_ref, gsem, ssem, *, dump_row):
    tg = EXPERT_TG
    t = pl.program_id(0)
    nu = nu_ref[0]
    slot = lax.rem(t, 2)
    other = 1 - slot

    def gather(src_ref, r, sl):
        return pltpu.make_async_copy(x_hbm.at[pl.ds(src_ref[r], 1), :],
                                     xbuf.at[sl, pl.ds(r, 1), :], gsem.at[sl])

    def scatter(dst_row, r, sl):
        return pltpu.make_async_copy(ybuf.at[sl, pl.ds(r, 1), :],
                                     y_hbm.at[pl.ds(dst_row, 1), :], ssem.at[sl])

    def tile_gather(sl):
        return pltpu.make_async_copy(x_hbm.at[pl.ds(0, tg), :], xbuf.at[sl], gsem.at[sl])

    def tile_scatter(sl, row):
        return pltpu.make_async_copy(ybuf.at[sl], y_hbm.at[pl.ds(row, tg), :], ssem.at[sl])

    @pl.when(t == 0)
    def _():
        ybuf[...] = jnp.zeros(ybuf.shape, F32)
        for sl in range(2):
            cp = tile_scatter(sl, dump_row + sl * tg)
            cp.start()
            cp.wait()

        def body(r, carry):
            gather(srcc_ref, r, 0).start()
            return carry
        lax.fori_loop(0, tg, body, 0)

    @pl.when(t < nu)
    def _():
        tile_gather(slot).wait()

        @pl.when(t >= 1)
        def _():
            tile_scatter(slot, 0).wait()

        @pl.when(jnp.logical_or(t == 0, te_ref[jnp.maximum(t - 1, 0)] != te_ref[t]))
        def _():
            for r in range(0, D_MODEL, LANES):
                wgu_b[r:r + LANES, :] = wgu_ref[r:r + LANES, :].astype(BF16)
            for r in range(0, D_FF, LANES):
                wdn_b[r:r + LANES, :] = wdn_ref[r:r + LANES, :].astype(BF16)

        xb = xbuf[slot].astype(BF16)
        n1 = 2 * D_FF // MOE_COLS
        for c in range(n1):
            cols = slice(c * MOE_COLS, (c + 1) * MOE_COLS)
            h_ref[:, cols] = _dot(xb, wgu_b[:, cols]) + bgu_ref[:, cols]
            for r in range(c * tg // n1, (c + 1) * tg // n1):
                gather(srcn_ref, r, other).start()

        h = h_ref[...]
        glu = jnp.minimum(h[:, :D_FF], SWIGLU_LIMIT)
        lin = jnp.clip(h[:, D_FF:], -SWIGLU_LIMIT, SWIGLU_LIMIT)
        act = (lin + 1.0) * (glu * (1.0 / (1.0 + jnp.exp(-SWIGLU_ALPHA * glu))))
        act_ref[...] = act.astype(BF16)

        first = t == 0
        n2 = D_MODEL // MOE_COLS
        for c in range(n2):
            cols = slice(c * MOE_COLS, (c + 1) * MOE_COLS)
            ybuf[slot, :, cols] = _dot(act_ref[...], wdn_b[:, cols]) + bdn_ref[:, cols]
            for r in range(c * tg // n2, (c + 1) * tg // n2):
                scatter(jnp.where(first, dump_row + tg + r, dstp_ref[r]), r, other).start()

        @pl.when(t == nu - 1)
        def _():
            def body(r, carry):
                scatter(dstc_ref[r], r, slot).start()
                return carry
            lax.fori_loop(0, tg, body, 0)
            tile_scatter(other, 0).wait()
            tile_scatter(slot, 0).wait()
            tile_gather(other).wait()


def _moe_experts(tile_e, n_used, src, dst, x1, wgu, bgu, wdn, bdn, layer):
    n = x1.shape[0]
    tg = EXPERT_TG
    n_tiles = src.shape[0] // tg
    dump_row = TOP_K * n
    wspec = lambda r, c: pl.BlockSpec((None, None, r, c), lambda t, te, nu: (layer, te[t], 0, 0))
    idx = lambda f: pl.BlockSpec((tg,), lambda t, te, nu: (f(t),), memory_space=pltpu.SMEM)
    grid_spec = pltpu.PrefetchScalarGridSpec(
        num_scalar_prefetch=2,
        grid=(n_tiles,),
        in_specs=[idx(lambda t: t), idx(lambda t: jnp.minimum(t + 1, n_tiles - 1)),
                  idx(lambda t: jnp.maximum(t - 1, 0)), idx(lambda t: t),
                  pl.BlockSpec(memory_space=pl.ANY),
                  wspec(D_MODEL, 2 * D_FF), wspec(1, 2 * D_FF), wspec(D_FF, D_MODEL), wspec(1, D_MODEL)],
        out_specs=pl.BlockSpec(memory_space=pl.ANY),
        scratch_shapes=[pltpu.VMEM((D_MODEL, 2 * D_FF), BF16), pltpu.VMEM((D_FF, D_MODEL), BF16),
                        pltpu.VMEM((2, tg, D_MODEL), F32), pltpu.VMEM((2, tg, D_MODEL), F32),
                        pltpu.VMEM((tg, 2 * D_FF), F32), pltpu.VMEM((tg, D_FF), BF16),
                        pltpu.SemaphoreType.DMA((2,)), pltpu.SemaphoreType.DMA((2,))],
    )
    return pl.pallas_call(
        functools.partial(_moe_kernel, dump_row=dump_row),
        out_shape=jax.ShapeDtypeStruct((dump_row + 2 * tg, D_MODEL), F32),
        grid_spec=grid_spec,
        compiler_params=pltpu.CompilerParams(dimension_semantics=("arbitrary",),
                                             vmem_limit_bytes=VMEM_LIMIT),
        name="experts",
    )(tile_e, n_used, src, src, dst, dst, x1, wgu, bgu, wdn, bdn)


def _moe_combine_kernel(x1_ref, gt_ref, g2_ref, b2_ref, y0_ref, y1_ref, y2_ref, y3_ref, o_ref, *, alpha):
    f = jnp.zeros(x1_ref.shape, F32)
    for j, y_ref in enumerate((y0_ref, y1_ref, y2_ref, y3_ref)):
        f = f + gt_ref[:, j:j + 1] * y_ref[...]
    o_ref[...] = _layer_norm(alpha * x1_ref[...] + f, g2_ref[...], b2_ref[...])


def _moe_combine(x1, gates, g2, b2, y, alpha):
    n = x1.shape[0]
    tm = min(MERGE_TM, n)
    nblk = n // tm
    row = lambda w: pl.BlockSpec((tm, w), lambda i: (i, 0))
    yspec = lambda j: pl.BlockSpec((tm, D_MODEL), lambda i: (j * nblk + i, 0))
    return pl.pallas_call(
        functools.partial(_moe_combine_kernel, alpha=alpha),
        out_shape=jax.ShapeDtypeStruct((n, D_MODEL), F32),
        grid=(nblk,),
        in_specs=[row(D_MODEL), row(LANES), _const_spec((1, D_MODEL)), _const_spec((1, D_MODEL)),
                  yspec(0), yspec(1), yspec(2), yspec(3)],
        out_specs=row(D_MODEL),
        compiler_params=pltpu.CompilerParams(dimension_semantics=("arbitrary",),
                                             vmem_limit_bytes=VMEM_LIMIT),
        name="combine",
    )(x1, gates, g2, b2, y, y, y, y)


def _sorted_row_maps(offsets, ids, ranks, n_rows):
    n = ids.shape[0]
    tg = EXPERT_TG
    pos = (offsets[ids[:, :TOP_K]] + ranks[:, :TOP_K]).reshape(-1)
    tok = jnp.repeat(jnp.arange(n, dtype=I32), TOP_K)
    slot = jnp.tile(jnp.arange(TOP_K, dtype=I32), n)
    rows = jnp.arange(n_rows, dtype=I32)
    src = jnp.zeros((n_rows,), I32).at[pos].set(tok, unique_indices=True)
    pad_dst = TOP_K * n + ((rows // tg) % 2) * tg + rows % tg
    dst = pad_dst.at[pos].set(slot * n + tok, unique_indices=True)
    return src, dst


def _slab_to_rows(x):
    t = x.shape[0]
    y = jnp.swapaxes(x.reshape(t // 8, 8, 8, LANES), 1, 2)
    return jnp.concatenate([y[:, c].reshape(t, LANES) for c in range(8)], axis=1)


def _rows_to_slab(v):
    t = v.shape[0]
    y = jnp.stack([v[:, c * LANES:(c + 1) * LANES].reshape(t // 8, 8, LANES) for c in range(8)], axis=1)
    return jnp.swapaxes(y, 1, 2).reshape(t, 8, LANES)


def _issue_rows(n_tok, copy):
    def body(g, carry):
        for u in range(8):
            for j in range(TOP_K):
                copy(g * 8 + u, j).start()
        return carry
    lax.fori_loop(0, n_tok // 8, body, 0)


def _scatter_kernel(zs_ref, pos_ref, x_ref, xs_ref, zbuf, sem):
    tm = x_ref.shape[0]
    tg = zbuf.shape[0]

    @pl.when(pl.program_id(0) == 0)
    def _():
        zbuf[...] = jnp.zeros(zbuf.shape, F32)
        for e in range(N_EXPERTS):
            cp = pltpu.make_async_copy(zbuf, xs_ref.at[pl.ds(zs_ref[e], tg)], sem)
            cp.start()
            cp.wait()
        n_tiles = xs_ref.shape[0] // tg
        for k in range(N_EXPERTS):
            @pl.when(n_tiles - 1 - k >= zs_ref[N_EXPERTS])
            def _():
                cp = pltpu.make_async_copy(zbuf, xs_ref.at[pl.ds((n_tiles - 1 - k) * tg, tg)], sem)
                cp.start()
                cp.wait()

    _issue_rows(tm, lambda tok, j: pltpu.make_async_copy(
        x_ref.at[tok], xs_ref.at[pos_ref[tok * TOP_K + j]], sem))
    for j in range(TOP_K):
        pltpu.make_async_copy(x_ref, xs_ref.at[pl.ds(0, tm)], sem).wait()


def _scatter_rows(zstart, pos_flat, x1r, n_rows):
    n = x1r.shape[0]
    tm = min(DISPATCH_TM, n)
    assert n % tm == 0
    grid_spec = pltpu.PrefetchScalarGridSpec(
        num_scalar_prefetch=1,
        grid=(n // tm,),
        in_specs=[pl.BlockSpec((tm * TOP_K,), lambda i, zs: (i,), memory_space=pltpu.SMEM),
                  pl.BlockSpec((tm, 8, LANES), lambda i, zs: (i, 0, 0))],
        out_specs=pl.BlockSpec(memory_space=pl.ANY),
        scratch_shapes=[pltpu.VMEM((EXPERT_TG, 8, LANES), F32), pltpu.SemaphoreType.DMA],
    )
    return pl.pallas_call(
        _scatter_kernel,
        out_shape=jax.ShapeDtypeStruct((n_rows, 8, LANES), F32),
        grid_spec=grid_spec,
        compiler_params=pltpu.CompilerParams(dimension_semantics=("arbitrary",),
                                             vmem_limit_bytes=VMEM_LIMIT),
        name="dispatch",
    )(zstart, pos_flat, x1r)


def _ffn_kernel(te_ref, nu_ref, xs_ref, wgu_ref, bgu_ref, wdn_ref, bdn_ref, y_ref, wgu_b, wdn_b):
    t = pl.program_id(0)
    valid = t < nu_ref[0]
    fresh = jnp.logical_or(t == 0, te_ref[jnp.maximum(t - 1, 0)] != te_ref[t])

    @pl.when(jnp.logical_and(valid, fresh))
    def _():
        for r in range(0, D_MODEL, LANES):
            wgu_b[r:r + LANES, :] = wgu_ref[r:r + LANES, :].astype(BF16)
        for r in range(0, D_FF, LANES):
            wdn_b[r:r + LANES, :] = wdn_ref[r:r + LANES, :].astype(BF16)

    @pl.when(valid)
    def _():
        h = _dot(_slab_to_rows(xs_ref[...]).astype(BF16), wgu_b[...]) + bgu_ref[...]
        glu = jnp.minimum(h[:, :D_FF], SWIGLU_LIMIT)
        lin = jnp.clip(h[:, D_FF:], -SWIGLU_LIMIT, SWIGLU_LIMIT)
        act = (lin + 1.0) * (glu * (1.0 / (1.0 + jnp.exp(-SWIGLU_ALPHA * glu))))
        y_ref[...] = _rows_to_slab(_dot(act.astype(BF16), wdn_b[...]) + bdn_ref[...])

    @pl.when(jnp.logical_not(valid))
    def _():
        y_ref[...] = jnp.zeros(y_ref.shape, F32)


def _ffn(tile_e, n_used, xs, wgu, bgu, wdn, bdn, layer):
    tg = EXPERT_TG
    n_tiles = xs.shape[0] // tg
    wspec = lambda r, c: pl.BlockSpec((None, None, r, c), lambda t, te, nu: (layer, te[t], 0, 0))
    slab = pl.BlockSpec((tg, 8, LANES), lambda t, te, nu: (t, 0, 0))
    grid_spec = pltpu.PrefetchScalarGridSpec(
        num_scalar_prefetch=2,
        grid=(n_tiles,),
        in_specs=[slab, wspec(D_MODEL, 2 * D_FF), wspec(1, 2 * D_FF), wspec(D_FF, D_MODEL),
                  wspec(1, D_MODEL)],
        out_specs=slab,
        scratch_shapes=[pltpu.VMEM((D_MODEL, 2 * D_FF), BF16), pltpu.VMEM((D_FF, D_MODEL), BF16)],
    )
    return pl.pallas_call(
        _ffn_kernel,
        out_shape=jax.ShapeDtypeStruct(xs.shape, F32),
        grid_spec=grid_spec,
        compiler_params=pltpu.CompilerParams(dimension_semantics=("arbitrary",),
                                             vmem_limit_bytes=VMEM_LIMIT),
        name="experts",
    )(tile_e, n_used, xs, wgu, bgu, wdn, bdn)


def _gather_kernel(posc_ref, posn_ref, x1_ref, gt_ref, g2_ref, b2_ref, y_ref, o_ref, buf, sem, *, alpha):
    tm = x1_ref.shape[0]
    i = pl.program_id(0)
    slot = lax.rem(i, 2)

    def issue(pos_ref, sl):
        _issue_rows(tm, lambda tok, j: pltpu.make_async_copy(
            y_ref.at[pos_ref[tok * TOP_K + j]], buf.at[sl, j, tok], sem.at[sl]))

    @pl.when(i == 0)
    def _():
        issue(posc_ref, 0)

    @pl.when(i + 1 < pl.num_programs(0))
    def _():
        issue(posn_ref, 1 - slot)

    for j in range(TOP_K):
        pltpu.make_async_copy(y_ref.at[pl.ds(0, tm)], buf.at[slot, j], sem.at[slot]).wait()

    f = jnp.zeros((tm, D_MODEL), F32)
    for j in range(TOP_K):
        f = f + gt_ref[:, j:j + 1] * _slab_to_rows(buf[slot, j])
    o_ref[...] = _layer_norm(alpha * x1_ref[...] + f, g2_ref[...], b2_ref[...])


def _gather_rows(pos_flat, x1, gates, g2, b2, y, alpha):
    n = x1.shape[0]
    tm = min(COMBINE_TM, n)
    assert n % tm == 0
    nblk = n // tm
    smem = lambda f: pl.BlockSpec((tm * TOP_K,), lambda i: (f(i),), memory_space=pltpu.SMEM)
    return pl.pallas_call(
        functools.partial(_gather_kernel, alpha=alpha),
        out_shape=jax.ShapeDtypeStruct((n, D_MODEL), F32),
        grid=(nblk,),
        in_specs=[smem(lambda i: i), smem(lambda i: jnp.minimum(i + 1, nblk - 1)),
                  pl.BlockSpec((tm, D_MODEL), lambda i: (i, 0)),
                  pl.BlockSpec((tm, LANES), lambda i: (i, 0)),
                  _const_spec((1, D_MODEL)), _const_spec((1, D_MODEL)),
                  pl.BlockSpec(memory_space=pl.ANY)],
        out_specs=pl.BlockSpec((tm, D_MODEL), lambda i: (i, 0)),
        scratch_shapes=[pltpu.VMEM((2, TOP_K, tm, 8, LANES), F32), pltpu.SemaphoreType.DMA((2,))],
        compiler_params=pltpu.CompilerParams(dimension_semantics=("arbitrary",),
                                             vmem_limit_bytes=VMEM_LIMIT),
        name="combine",
    )(pos_flat, pos_flat, x1, gates, g2, b2, y)


def _sorted_positions(offsets, counts, n_used, ids, ranks, n_rows):
    e = ids[:, :TOP_K]
    onehot = e[..., None] == jnp.arange(N_EXPERTS, dtype=I32)
    pos = ranks[:, :TOP_K] + jnp.sum(jnp.where(onehot, offsets[:N_EXPERTS], 0), axis=-1)
    cnt = counts[0, :N_EXPERTS].astype(I32)
    zstart = jnp.minimum(offsets[:N_EXPERTS] + cnt, n_rows - EXPERT_TG)
    return pos.reshape(-1).astype(I32), jnp.concatenate([zstart.astype(I32), n_used])


def _rope_tables(positions):
    pos = positions.reshape(-1).astype(F32)[:, None]
    n = pos.shape[0]

    def cs(rot_dim):
        half = rot_dim // 2
        inv_freq = ROPE_THETA ** (-jnp.arange(half, dtype=F32) / half)
        ang = pos * inv_freq
        return jnp.cos(ang), jnp.sin(ang)

    cos_p, sin_p = cs(A_ROT_DIM)
    cos_c, sin_c = cs(C_ROPE_DIM)
    hp, hc = A_ROT_DIM // 2, C_ROPE_DIM // 2
    one = lambda w: jnp.ones((n, w), F32)
    zero = lambda w: jnp.zeros((n, w), F32)

    a64 = jnp.concatenate([cos_p, cos_p, one(64 - 2 * hp)] * 2, axis=1)
    bm64 = jnp.concatenate([-sin_p, zero(64 - hp)] * 2, axis=1)
    bp64 = jnp.concatenate([zero(hp), sin_p, zero(64 - 2 * hp)] * 2, axis=1)
    a128 = jnp.concatenate([one(64), cos_c, cos_c, one(32)], axis=1)
    bm128 = jnp.concatenate([zero(64), -sin_c, zero(64 - hc)], axis=1)
    bp128 = jnp.concatenate([zero(64 + hc), sin_c, zero(32)], axis=1)
    iw_scale = jnp.full((n, IDX_HEADS), IDX_HEADS ** -0.5, F32)
    as2 = jnp.concatenate([iw_scale, zero(64 - IDX_HEADS), cos_c, cos_c, zero(32)], axis=1)
    return {"t64": jnp.stack([a64, bm64, bp64]),
            "t128": jnp.stack([a128, bm128, bp128]),
            "ts2": jnp.stack([as2, bm128, bp128])}


def _layer_weights(l, w_in, gm_norm_g, gm_norm_b, gm_w_s, gm_b_s, mla_q_norm, mla_kv_norm,
                   mla_w_uq, mla_w_ukv, w_branch, w_out, ln1_g, ln1_b, router_w, router_b):
    w = w_in[l]
    o = 0
    cols = {}
    for name, width in (("aq", 512), ("ak", 512), ("av", 512), ("iq", 512), ("ik", 64), ("iw", 8),
                        ("buv", 1024), ("cdq", 256), ("cdkv", 128), ("ckr", 32), ("gate", 3072)):
        cols[name] = w[:, o:o + width]
        o += width
    zc = lambda width: jnp.zeros((D_MODEL, width), F32)
    s2 = jnp.concatenate([cols["iw"], zc(64 - IDX_HEADS), cols["ckr"], zc(32)], axis=1)
    wp = jnp.concatenate([cols["aq"] * (A_HEAD_DIM ** -0.5), cols["ak"],
                          cols["iq"] * (IDX_HEAD_DIM ** -0.5), cols["av"],
                          cols["ik"], cols["ik"], s2, cols["buv"], cols["cdq"], cols["cdkv"]], axis=1)

    uq = mla_w_uq[l].reshape(C_Q_RANK, C_HEADS, C_NOPE_DIM + C_ROPE_DIM)
    wuq = jnp.pad(uq, ((0, 0), (0, 0), (0, LANES - C_NOPE_DIM - C_ROPE_DIM))).reshape(C_Q_RANK, -1)
    ukv = mla_w_ukv[l].reshape(C_KV_RANK, C_HEADS, C_NOPE_DIM + C_V_DIM)
    wk = jnp.pad(ukv[:, :, :C_NOPE_DIM], ((0, 0), (0, 0), (0, LANES - C_NOPE_DIM))).reshape(C_KV_RANK, -1)
    wv = ukv[:, :, C_NOPE_DIM:].reshape(C_KV_RANK, -1)
    src = jnp.arange(LANES)[:, None]
    dst = jnp.arange(C_HEADS * LANES)[None, :]
    place = ((dst % LANES == src) & (src >= C_NOPE_DIM) & (src < C_NOPE_DIM + C_ROPE_DIM)).astype(F32)
    wke = jnp.concatenate([wk, place], axis=0)

    tril = jnp.tril(jnp.ones((B_CHUNK, B_CHUNK), dtype=bool))
    ws = jnp.where(tril[None], gm_w_s[l], 0)
    bsf = jnp.repeat(gm_b_s[l].T, B_GROUP_DIM, axis=1)
    rw = jnp.pad(router_w[l], ((0, 0), (0, LANES - N_EXPERTS)))
    rb = jnp.pad(router_b[l], (0, LANES - N_EXPERTS))[None, :]
    return {
        "wp": wp.astype(BF16), "wuq": wuq.astype(BF16), "wke": wke.astype(BF16), "wv": wv.astype(BF16),
        "ws": ws.astype(BF16), "gmg": gm_norm_g[l][None, :], "gmb": gm_norm_b[l][None, :], "bsf": bsf,
        "qn": mla_q_norm[l][None, :], "kvn": mla_kv_norm[l][None, :],
        "wg": cols["gate"].astype(BF16), "wbr": w_branch[l].astype(BF16), "wo": w_out[l].astype(BF16),
        "g1": ln1_g[l][None, :], "b1": ln1_b[l][None, :], "rw": rw, "rb": rb,
    }


def _expert_tiles(counts, n_tiles):
    tg = EXPERT_TG
    cnt = counts[0, :N_EXPERTS].astype(I32)
    tiles_per = (cnt + tg - 1) // tg
    tile_end = jnp.cumsum(tiles_per)
    offsets = (tile_end - tiles_per) * tg
    n_used = tile_end[-1]
    tid = jnp.minimum(jnp.arange(n_tiles, dtype=I32), n_used - 1)
    tile_e = jnp.sum(tid[:, None] >= tile_end[None, :], axis=1).astype(I32)
    off128 = jnp.pad(offsets, (0, LANES - N_EXPERTS)).astype(I32)
    return off128, tile_e, n_used.reshape(1).astype(I32)


def _layer(l, x2, tabs, batch, seq, depth, p):
    alpha = (2 * depth) ** 0.25
    n = x2.shape[0]
    wts = _layer_weights(l, p["w_in"], p["gm_norm_g"], p["gm_norm_b"], p["gm_w_s"], p["gm_b_s"],
                         p["mla_q_norm"], p["mla_kv_norm"], p["mla_w_uq"], p["mla_w_ukv"],
                         p["w_branch"], p["w_out"], p["ln1_g"], p["ln1_b"], p["router_w"], p["router_b"])
    aq, ak, iq, avt, ik2, iwt, o_b, q, k, vt = _proj(x2, wts, tabs)
    o_a = _dsa(iq, iwt, aq, ik2, ak, avt, batch, seq)
    o_c = _mla(q, k, vt, batch, seq)
    x1, x1r, ids, ranks, gates, counts = _merge(x2, o_a, o_b, o_c, wts, alpha)

    n_tiles = n * TOP_K // EXPERT_TG + N_EXPERTS
    offsets, tile_e, n_used = _expert_tiles(counts, n_tiles)
    n_rows = n_tiles * EXPERT_TG
    pos_flat, zstart = _sorted_positions(offsets, counts, n_used, ids, ranks, n_rows)
    xs = _scatter_rows(zstart, pos_flat, x1r, n_rows)
    y = _ffn(tile_e, n_used, xs, p["exp_w_gu"], p["exp_b_gu"][:, :, None, :],
             p["exp_w_dn"], p["exp_b_dn"][:, :, None, :], l)
    return _gather_rows(pos_flat, x1, gates, p["ln2_g"][l][None, :], p["ln2_b"][l][None, :], y, alpha)


def kernel(x, positions, w_in, gm_norm_g, gm_norm_b, gm_w_s, gm_b_s, mla_q_norm, mla_kv_norm, mla_w_uq, mla_w_ukv, w_branch, w_out, ln1_g, ln1_b, router_w, router_b, exp_w_gu, exp_b_gu, exp_w_dn, exp_b_dn, ln2_g, ln2_b):
    batch, seq, _ = x.shape
    depth = w_in.shape[0]
    p = dict(w_in=w_in, gm_norm_g=gm_norm_g, gm_norm_b=gm_norm_b, gm_w_s=gm_w_s, gm_b_s=gm_b_s,
             mla_q_norm=mla_q_norm, mla_kv_norm=mla_kv_norm, mla_w_uq=mla_w_uq, mla_w_ukv=mla_w_ukv,
             w_branch=w_branch, w_out=w_out, ln1_g=ln1_g, ln1_b=ln1_b, router_w=router_w,
             router_b=router_b, exp_w_gu=exp_w_gu, exp_b_gu=exp_b_gu, exp_w_dn=exp_w_dn,
             exp_b_dn=exp_b_dn, ln2_g=ln2_g, ln2_b=ln2_b)
    tabs = _rope_tables(positions)
    x2 = x.reshape(batch * seq, D_MODEL)
    for l in range(depth):
        x2 = _layer(l, x2, tabs, batch, seq, depth, p)
    return x2.reshape(batch, seq, D_MODEL)
```

```python
import functools
import math

import jax
import jax.numpy as jnp
from jax import lax
from jax.experimental import pallas as pl
from jax.experimental.pallas import tpu as pltpu

F32 = jnp.float32
BF16 = jnp.bfloat16
I32 = jnp.int32

D_MODEL = 1024
BRANCH_WIDTH = 512
N_BRANCH = 3
ROPE_THETA = 500000.0
LN_EPS = 1e-5
RMS_EPS = 1e-6

A_HEADS = 8
A_HEAD_DIM = 64
A_ROT_DIM = 16
IDX_HEADS = 8
IDX_HEAD_DIM = 64
IDX_TOPK_MAX = 256

B_GROUPS = 8
B_GROUP_DIM = 64
B_CHUNK = 128

C_HEADS = 8
C_NOPE_DIM = 64
C_ROPE_DIM = 32
C_V_DIM = 64
C_Q_RANK = 256
C_KV_RANK = 128

N_EXPERTS = 32
TOP_K = 4
D_FF = 1024
SWIGLU_LIMIT = 7.0
SWIGLU_ALPHA = 1.702

LANES = 128
ATT_T = 512
PROJ_TM = 512
MERGE_TM = 256
DISPATCH_TM = 256
COMBINE_TM = 256
EXPERT_TG = 256
MOE_COLS = 256
VMEM_LIMIT = 56 * 1024 * 1024

FLASH_LOOKAHEAD = 2
NEG = -1e30
LOG2E = math.log2(math.e)
INT_MIN = -(2 ** 31)

_C_AQ, _C_AK, _C_IQ, _C_AV = 0, 512, 1024, 1536
_C_S1, _C_S2, _C_B, _C_C, _C_END = 2048, 2176, 2304, 3328, 3712


def _dot(a, b):
    return jnp.dot(a, b, preferred_element_type=F32)


def _dot_nt(a, b):
    return lax.dot_general(a, b, (((1,), (1,)), ((), ())), preferred_element_type=F32)


def _rot(y, a, bm, bp, r):
    w = y.shape[1]
    return y * a + pltpu.roll(y, w - r, 1) * bm + pltpu.roll(y, r, 1) * bp


def _layer_norm(x, g, b):
    mu = jnp.mean(x, axis=-1, keepdims=True)
    xc = x - mu
    var = jnp.mean(xc * xc, axis=-1, keepdims=True)
    return xc * lax.rsqrt(var + LN_EPS) * g + b


def _rms_norm(x, g):
    return x * lax.rsqrt(jnp.mean(x * x, axis=-1, keepdims=True) + RMS_EPS) * g


def _const_spec(shape):
    nd = len(shape)
    return pl.BlockSpec(shape, lambda *_: (0,) * nd)


def _proj_kernel(x_ref, wp_ref, wuq_ref, wke_ref, wv_ref, ws_ref, gmg_ref, gmb_ref, bsf_ref,
                 qn_ref, kvn_ref, t64_ref, t128_ref, ts2_ref,
                 aq_o, ak_o, iq_o, avt_o, ik2_o, iwt_o, ob_o, q_o, k_o, vt_o):
    tm = x_ref.shape[0]
    xb = x_ref[...].astype(BF16)

    def mm(c0, c1):
        return _dot(xb, wp_ref[:, c0:c1])

    a64, bm64, bp64 = t64_ref[0], t64_ref[1], t64_ref[2]
    a4 = jnp.concatenate([a64] * 4, axis=1)
    bm4 = jnp.concatenate([bm64] * 4, axis=1)
    bp4 = jnp.concatenate([bp64] * 4, axis=1)
    for c0, out in ((_C_AQ, aq_o), (_C_AK, ak_o), (_C_IQ, iq_o)):
        out[...] = _rot(mm(c0, c0 + 512), a4, bm4, bp4, A_ROT_DIM // 2).astype(BF16)
    avt_o[...] = mm(_C_AV, _C_AV + 512).T.astype(BF16)
    ik2_o[...] = _rot(mm(_C_S1, _C_S1 + LANES), a64, bm64, bp64, A_ROT_DIM // 2).astype(BF16)
    sm2 = _rot(mm(_C_S2, _C_S2 + LANES), ts2_ref[0], ts2_ref[1], ts2_ref[2], C_ROPE_DIM // 2)
    iwt_o[...] = sm2.T[:IDX_HEADS, :]

    buv = mm(_C_B, _C_B + 2 * BRANCH_WIDTH)
    buv = 0.5 * buv * (1.0 + lax.erf(buv * (1.0 / math.sqrt(2.0))))
    u = buv[:, :BRANCH_WIDTH]
    v = _layer_norm(buv[:, BRANCH_WIDTH:], gmg_ref[...], gmb_ref[...]).astype(BF16)
    lane = lax.broadcasted_iota(I32, (B_CHUNK, LANES), 1)
    for c in range(tm // B_CHUNK):
        rows = slice(c * B_CHUNK, (c + 1) * B_CHUNK)
        for pr in range(B_GROUPS // 2):
            cols = slice(pr * LANES, (pr + 1) * LANES)
            vp = v[rows, cols]
            s = jnp.where(lane < B_GROUP_DIM, _dot(ws_ref[2 * pr], vp), _dot(ws_ref[2 * pr + 1], vp))
            ob_o[rows, cols] = (u[rows, cols] * (s + bsf_ref[:, cols])).astype(BF16)

    cc = mm(_C_C, _C_END)
    cq = _rms_norm(cc[:, :C_Q_RANK], qn_ref[...]).astype(BF16)
    a8 = jnp.concatenate([t128_ref[0]] * C_HEADS, axis=1)
    bm8 = jnp.concatenate([t128_ref[1]] * C_HEADS, axis=1)
    bp8 = jnp.concatenate([t128_ref[2]] * C_HEADS, axis=1)
    q_o[...] = _rot(_dot(cq, wuq_ref[...]), a8, bm8, bp8, C_ROPE_DIM // 2).astype(BF16)
    ckv = _rms_norm(cc[:, C_Q_RANK:], kvn_ref[...]).astype(BF16)
    kin = jnp.concatenate([ckv, sm2.astype(BF16)], axis=1)
    k_o[...] = _dot(kin, wke_ref[...]).astype(BF16)
    vt_o[...] = _dot(ckv, wv_ref[...]).T.astype(BF16)


def _proj(x2, wts, tabs):
    n = x2.shape[0]
    tm = min(PROJ_TM, n)
    assert n % tm == 0 and tm % B_CHUNK == 0
    row = lambda w: pl.BlockSpec((tm, w), lambda i: (i, 0))
    col = lambda h: pl.BlockSpec((h, tm), lambda i: (0, i))
    tab = pl.BlockSpec((3, tm, LANES), lambda i: (0, i, 0))
    in_specs = [
        row(D_MODEL),
        _const_spec((D_MODEL, _C_END)),
        _const_spec((C_Q_RANK, C_HEADS * LANES)),
        _const_spec((C_KV_RANK + LANES, C_HEADS * LANES)),
        _const_spec((C_KV_RANK, C_HEADS * C_V_DIM)),
        _const_spec((B_GROUPS, B_CHUNK, B_CHUNK)),
        _const_spec((1, BRANCH_WIDTH)),
        _const_spec((1, BRANCH_WIDTH)),
        _const_spec((B_CHUNK, BRANCH_WIDTH)),
        _const_spec((1, C_Q_RANK)),
        _const_spec((1, C_KV_RANK)),
        tab, tab, tab,
    ]
    out_shape = [
        jax.ShapeDtypeStruct((n, 512), BF16),
        jax.ShapeDtypeStruct((n, 512), BF16),
        jax.ShapeDtypeStruct((n, 512), BF16),
        jax.ShapeDtypeStruct((512, n), BF16),
        jax.ShapeDtypeStruct((n, LANES), BF16),
        jax.ShapeDtypeStruct((IDX_HEADS, n), F32),
        jax.ShapeDtypeStruct((n, 512), BF16),
        jax.ShapeDtypeStruct((n, C_HEADS * LANES), BF16),
        jax.ShapeDtypeStruct((n, C_HEADS * LANES), BF16),
        jax.ShapeDtypeStruct((C_HEADS * C_V_DIM, n), BF16),
    ]
    out_specs = [row(512), row(512), row(512), col(512), row(LANES), col(IDX_HEADS), row(512),
                 row(C_HEADS * LANES), row(C_HEADS * LANES), col(C_HEADS * C_V_DIM)]
    return pl.pallas_call(
        _proj_kernel,
        out_shape=out_shape,
        grid=(n // tm,),
        in_specs=in_specs,
        out_specs=out_specs,
        compiler_params=pltpu.CompilerParams(dimension_semantics=("arbitrary",),
                                             vmem_limit_bytes=VMEM_LIMIT),
        name="proj",
    )(x2, wts["wp"], wts["wuq"], wts["wke"], wts["wv"], wts["ws"], wts["gmg"], wts["gmb"],
      wts["bsf"], wts["qn"], wts["kvn"], tabs["t64"], tabs["t128"], tabs["ts2"])


def _rep8(x):
    return jnp.broadcast_to(x, (8, x.shape[-1]))


def _fold8(x3, op):
    return _rep8(op(op(x3, axis=0), axis=0, keepdims=True))


def _split8(x):
    return x.reshape(x.shape[0] // 8, 8, x.shape[1])


def _flash_init(m_ref, l_ref, acc_ref):
    m_ref[...] = jnp.full(m_ref.shape, NEG, F32)
    l_ref[...] = jnp.zeros(l_ref.shape, F32)
    acc_ref[...] = jnp.zeros(acc_ref.shape, F32)


def _flash_update(s3, vt, h, m_ref, l_ref, acc_ref):
    n, _, tq = s3.shape
    m_old = m_ref[h]
    m_new = jnp.maximum(m_old, _fold8(s3, jnp.max))
    alpha = jnp.exp2(m_old - m_new)
    p = jnp.exp2(s3 - m_new[None])
    m_ref[h] = m_new
    l_ref[h] = alpha * l_ref[h] + _fold8(p, jnp.sum)
    pv = _dot(vt, p.reshape(n * 8, tq).astype(BF16))
    acc_ref[h] = (_split8(acc_ref[h]) * alpha[None]).reshape(pv.shape) + pv


def _flash_chunk(logits, values, heads, m_ref, l_ref, acc_ref):
    s = {h: logits(h) for h in range(min(FLASH_LOOKAHEAD, heads))}
    for h in range(heads):
        if h + FLASH_LOOKAHEAD < heads:
            s[h + FLASH_LOOKAHEAD] = logits(h + FLASH_LOOKAHEAD)
        _flash_update(s.pop(h), values(h), h, m_ref, l_ref, acc_ref)


def _flash_finish(o_ref, l_ref, acc_ref, heads):
    outs = [(_split8(acc_ref[h]) / l_ref[h][None]).reshape(acc_ref.shape[1:]) for h in range(heads)]
    o_ref[...] = jnp.concatenate(outs, axis=0).T.astype(BF16)


def _dsa_kernel(iq_ref, iwt_ref, aq_ref, ik2_ref, ak_ref, avt_ref, o_ref,
                key_ref, iqm_ref, aqm_ref, thr_ref, m_ref, l_ref, acc_ref, *, n_sel):
    t = iq_ref.shape[0]
    i = pl.program_id(1)
    nchunk = i + 1
    lane = lax.broadcasted_iota(I32, (t, LANES), 1)
    lo = lane < IDX_HEAD_DIM
    krow = lax.broadcasted_iota(I32, (t, t), 0)
    qcol = lax.broadcasted_iota(I32, (t, t), 1)

    for h in range(IDX_HEADS):
        cols = slice((h // 2) * LANES, (h // 2 + 1) * LANES)
        keep = lo if h % 2 == 0 else jnp.logical_not(lo)
        iqm_ref[h] = jnp.where(keep, iq_ref[:, cols], jnp.zeros((t, LANES), BF16))
        aqm_ref[h] = jnp.where(keep, aq_ref[:, cols], jnp.zeros((t, LANES), BF16))

    def rows(c):
        return pl.ds(pl.multiple_of(c * t, t), t)

    def causal_at(c):
        return (krow + c * t) <= (qcol + i * t)

    def idx_body(c, carry):
        kc = ik2_ref[rows(c), :]
        acc = jnp.zeros((t // 8, 8, t), F32)
        for h in range(IDX_HEADS):
            w8 = _rep8(iwt_ref[h:h + 1, :])
            acc = acc + _split8(jnp.maximum(_dot_nt(kc, iqm_ref[h]), 0.0)) * w8[None]
        bits = lax.bitcast_convert_type(acc.reshape(t, t), I32)
        key = bits ^ ((bits >> 31) & 0x7FFFFFFF)
        key_ref[rows(c), :] = jnp.where(causal_at(c), key, INT_MIN)
        return carry

    lax.fori_loop(0, nchunk, idx_body, 0)

    def count(indicator):
        def body(c, acc):
            return acc + jnp.sum(indicator(_split8(key_ref[rows(c), :])), axis=0)
        part = lax.fori_loop(0, nchunk, body, jnp.zeros((8, t), I32))
        return _rep8(jnp.sum(part, axis=0, keepdims=True))

    def bit_body(it, prefix):
        trial_u = prefix | jnp.left_shift(jnp.int32(1), 31 - it)
        trial_s = trial_u ^ INT_MIN
        cnt = count(lambda kc: jnp.where(kc >= trial_s[None], 1, 0))
        return jnp.where(cnt >= n_sel, trial_u, prefix)

    thr = lax.fori_loop(0, 32, bit_body, jnp.zeros((8, t), I32)) ^ INT_MIN
    cnt_ge = count(lambda kc: jnp.where(kc >= thr[None], 1, 0))
    tied = jnp.where(cnt_ge > n_sel, jnp.where(thr != INT_MIN, 1, 0), 0)

    @pl.when(jnp.max(tied) > 0)
    def _():
        need = (n_sel - count(lambda kc: jnp.where(kc > thr[None], 1, 0))).astype(F32)
        lower = jnp.where(qcol < krow, 1.0, 0.0).astype(BF16)

        def body(c, seen):
            kc = _split8(key_ref[rows(c), :])
            eq = jnp.where(kc == thr[None], 1.0, 0.0)
            rank = _split8(_dot(lower, eq.reshape(t, t).astype(BF16))) + seen[None]
            drop = jnp.where(rank >= need[None], eq, 0.0)
            key_ref[rows(c), :] = jnp.where(drop > 0.0, INT_MIN, kc).reshape(t, t)
            return seen + _fold8(eq, jnp.sum)

        lax.fori_loop(0, nchunk, body, jnp.zeros((8, t), F32))

    thr_ref[...] = jnp.maximum(thr, INT_MIN + 1)

    _flash_init(m_ref, l_ref, acc_ref)

    def att_body(c, carry):
        sel = _split8(key_ref[rows(c), :]) >= thr_ref[...][None]

        def logits(h):
            cols = slice((h // 2) * LANES, (h // 2 + 1) * LANES)
            return jnp.where(sel, _split8(_dot_nt(ak_ref[rows(c), cols], aqm_ref[h])), NEG)

        def values(h):
            return avt_ref[h * A_HEAD_DIM:(h + 1) * A_HEAD_DIM, rows(c)]

        _flash_chunk(logits, values, A_HEADS, m_ref, l_ref, acc_ref)
        return carry

    lax.fori_loop(0, nchunk, att_body, 0)
    _flash_finish(o_ref, l_ref, acc_ref, A_HEADS)


def _dsa(iq, iwt, aq, ik2, ak, avt, batch, seq):
    t = min(ATT_T, seq)
    nq = seq // t
    n_sel = min(IDX_TOPK_MAX, seq // 4)
    assert seq % t == 0 and t % LANES == 0
    qspec = lambda w: pl.BlockSpec((t, w), lambda b, i: (b * nq + i, 0))
    kspec = lambda w: pl.BlockSpec((seq, w), lambda b, i: (b, 0))
    return pl.pallas_call(
        functools.partial(_dsa_kernel, n_sel=n_sel),
        out_shape=jax.ShapeDtypeStruct((batch * seq, 512), BF16),
        grid=(batch, nq),
        in_specs=[qspec(512),
                  pl.BlockSpec((IDX_HEADS, t), lambda b, i: (0, b * nq + i)),
                  qspec(512), kspec(LANES), kspec(512),
                  pl.BlockSpec((512, seq), lambda b, i: (0, b))],
        out_specs=qspec(512),
        scratch_shapes=[
            pltpu.VMEM((seq, t), I32),
            pltpu.VMEM((IDX_HEADS, t, LANES), BF16),
            pltpu.VMEM((A_HEADS, t, LANES), BF16),
            pltpu.VMEM((8, t), I32),
            pltpu.VMEM((A_HEADS, 8, t), F32),
            pltpu.VMEM((A_HEADS, 8, t), F32),
            pltpu.VMEM((A_HEADS, A_HEAD_DIM, t), F32),
        ],
        compiler_params=pltpu.CompilerParams(dimension_semantics=("arbitrary", "arbitrary"),
                                             vmem_limit_bytes=VMEM_LIMIT),
        name="dsa",
    )(iq, iwt, aq, ik2, ak, avt)


def _mla_kernel(q_ref, k_ref, vt_ref, o_ref, m_ref, l_ref, acc_ref):
    t = q_ref.shape[0]
    i = pl.program_id(1)
    krow = lax.broadcasted_iota(I32, (t // 8, 8, t), 0) * 8 + lax.broadcasted_iota(I32, (t // 8, 8, t), 1)
    qcol = lax.broadcasted_iota(I32, (t // 8, 8, t), 2)
    _flash_init(m_ref, l_ref, acc_ref)

    def chunk(c, masked):
        rows = pl.ds(pl.multiple_of(c * t, t), t)

        def logits(h):
            cols = slice(h * LANES, (h + 1) * LANES)
            s3 = _split8(_dot_nt(k_ref[rows, cols], q_ref[:, cols]))
            return jnp.where(krow <= qcol, s3, NEG) if masked else s3

        def values(h):
            return vt_ref[h * C_V_DIM:(h + 1) * C_V_DIM, rows]

        _flash_chunk(logits, values, C_HEADS, m_ref, l_ref, acc_ref)

    def body(c, carry):
        chunk(c, False)
        return carry

    lax.fori_loop(0, i, body, 0)
    chunk(i, True)
    _flash_finish(o_ref, l_ref, acc_ref, C_HEADS)


def _mla(q, k, vt, batch, seq):
    t = min(ATT_T, seq)
    nq = seq // t
    return pl.pallas_call(
        _mla_kernel,
        out_shape=jax.ShapeDtypeStruct((batch * seq, C_HEADS * C_V_DIM), BF16),
        grid=(batch, nq),
        in_specs=[pl.BlockSpec((t, C_HEADS * LANES), lambda b, i: (b * nq + i, 0)),
                  pl.BlockSpec((seq, C_HEADS * LANES), lambda b, i: (b, 0)),
                  pl.BlockSpec((C_HEADS * C_V_DIM, seq), lambda b, i: (0, b))],
        out_specs=pl.BlockSpec((t, C_HEADS * C_V_DIM), lambda b, i: (b * nq + i, 0)),
        scratch_shapes=[pltpu.VMEM((C_HEADS, 8, t), F32),
                        pltpu.VMEM((C_HEADS, 8, t), F32),
                        pltpu.VMEM((C_HEADS, C_V_DIM, t), F32)],
        compiler_params=pltpu.CompilerParams(dimension_semantics=("arbitrary", "arbitrary"),
                                             vmem_limit_bytes=VMEM_LIMIT),
        name="mla",
    )(q, k, vt)


def _merge_kernel(x_ref, oa_ref, ob_ref, oc_ref, wg_ref, wbr_ref, wo_ref, g1_ref, b1_ref,
                  rw_ref, rb_ref, x1_o, x1r_o, ids_o, rk_o, gt_o, cnt_o, run_ref, *, alpha):
    tm = x_ref.shape[0]

    @pl.when(pl.program_id(0) == 0)
    def _():
        run_ref[...] = jnp.zeros_like(run_ref)

    x = x_ref[...]
    xb = x.astype(BF16)
    merged = jnp.zeros((tm, D_MODEL), F32)
    for n, o_ref in enumerate((oa_ref, ob_ref, oc_ref)):
        z = _dot(xb, wg_ref[:, n * D_MODEL:(n + 1) * D_MODEL])
        merged = merged + (1.0 / (1.0 + jnp.exp(-z))) * _dot(o_ref[...], wbr_ref[n])
    y = _dot(merged.astype(BF16), wo_ref[...])
    x1 = _layer_norm(alpha * x + y, g1_ref[...], b1_ref[...])
    x1_o[...] = x1
    x1r_o[...] = _rows_to_slab(x1)

    lane = lax.broadcasted_iota(I32, (tm, LANES), 1)
    logits = jnp.dot(x1, rw_ref[...], precision=lax.Precision.HIGHEST,
                     preferred_element_type=F32) + rb_ref[...]
    lg = jnp.where(lane < N_EXPERTS, logits, -jnp.inf)
    ids, vals = [], []
    for _ in range(TOP_K):
        mx = jnp.max(lg, axis=1, keepdims=True)
        idx = jnp.min(jnp.where(lg == mx, lane, LANES), axis=1, keepdims=True)
        ids.append(idx)
        vals.append(mx)
        lg = jnp.where(lane == idx, -jnp.inf, lg)
    es = [jnp.exp(v - vals[0]) for v in vals]
    den = es[0] + es[1] + es[2] + es[3]

    hot = jnp.zeros((tm, LANES), F32)
    for idx in ids:
        hot = hot + jnp.where(lane == idx, 1.0, 0.0)
    r2 = lax.broadcasted_iota(I32, (tm, tm), 0)
    c2 = lax.broadcasted_iota(I32, (tm, tm), 1)
    lower = jnp.where(c2 < r2, 1.0, 0.0).astype(BF16)
    base = run_ref[0:1, :] + _dot(lower, hot.astype(BF16))
    ids_v = jnp.zeros((tm, LANES), I32)
    rk_v = jnp.zeros((tm, LANES), I32)
    gt_v = jnp.zeros((tm, LANES), F32)
    for j in range(TOP_K):
        rank = jnp.sum(jnp.where(lane == ids[j], base, 0.0), axis=1, keepdims=True)
        ids_v = jnp.where(lane == j, ids[j], ids_v)
        rk_v = jnp.where(lane == j, rank.astype(I32), rk_v)
        gt_v = jnp.where(lane == j, es[j] / den, gt_v)
    ids_o[...] = ids_v
    rk_o[...] = rk_v
    gt_o[...] = gt_v
    run = run_ref[0:1, :] + jnp.sum(hot, axis=0, keepdims=True)
    run_ref[...] = jnp.broadcast_to(run, run_ref.shape)
    cnt_o[...] = jnp.broadcast_to(run, cnt_o.shape)


def _merge(x2, oa, ob, oc, wts, alpha):
    n = x2.shape[0]
    tm = min(MERGE_TM, n)
    assert n % tm == 0
    row = lambda w: pl.BlockSpec((tm, w), lambda i: (i, 0))
    in_specs = [row(D_MODEL), row(512), row(512), row(512),
                _const_spec((D_MODEL, N_BRANCH * D_MODEL)),
                _const_spec((N_BRANCH, BRANCH_WIDTH, D_MODEL)),
                _const_spec((D_MODEL, D_MODEL)),
                _const_spec((1, D_MODEL)), _const_spec((1, D_MODEL)),
                _const_spec((D_MODEL, LANES)), _const_spec((1, LANES))]
    out_shape = [jax.ShapeDtypeStruct((n, D_MODEL), F32),
                 jax.ShapeDtypeStruct((n, 8, LANES), F32),
                 jax.ShapeDtypeStruct((n, LANES), I32),
                 jax.ShapeDtypeStruct((n, LANES), I32),
                 jax.ShapeDtypeStruct((n, LANES), F32),
                 jax.ShapeDtypeStruct((8, LANES), F32)]
    out_specs = [row(D_MODEL), pl.BlockSpec((tm, 8, LANES), lambda i: (i, 0, 0)),
                 row(LANES), row(LANES), row(LANES), _const_spec((8, LANES))]
    return pl.pallas_call(
        functools.partial(_merge_kernel, alpha=alpha),
        out_shape=out_shape,
        grid=(n // tm,),
        in_specs=in_specs,
        out_specs=out_specs,
        scratch_shapes=[pltpu.VMEM((8, LANES), F32)],
        compiler_params=pltpu.CompilerParams(dimension_semantics=("arbitrary",),
                                             vmem_limit_bytes=VMEM_LIMIT),
        name="merge",
    )(x2, oa, ob, oc, wts["wg"], wts["wbr"], wts["wo"], wts["g1"], wts["b1"], wts["rw"], wts["rb"])


def _row_copy(src_ref, src_row, dst_ref, dst_row, sem):
    return pltpu.make_async_copy(src_ref.at[pl.ds(src_row, 1), :], dst_ref.at[pl.ds(dst_row, 1), :], sem)


def _dispatch_kernel(off_ref, ids_ref, rk_ref, x_ref, xs_in_ref, xs_ref, sem):
    del xs_in_ref
    tm = x_ref.shape[0]

    def issue(tok, carry):
        for j in range(TOP_K):
            a = tok * TOP_K + j
            _row_copy(x_ref, tok, xs_ref, off_ref[ids_ref[a]] + rk_ref[a], sem).start()
        return carry

    lax.fori_loop(0, tm, issue, 0)

    def drain(a, carry):
        _row_copy(x_ref, 0, xs_ref, 0, sem).wait()
        return carry

    lax.fori_loop(0, tm * TOP_K, drain, 0)


def _dispatch(offsets, ids_flat, rk_flat, x1, xs_init):
    n = x1.shape[0]
    tm = min(DISPATCH_TM, n)
    assert n % tm == 0
    smem = pl.BlockSpec((tm * TOP_K,), lambda i, off: (i,), memory_space=pltpu.SMEM)
    grid_spec = pltpu.PrefetchScalarGridSpec(
        num_scalar_prefetch=1,
        grid=(n // tm,),
        in_specs=[smem, smem,
                  pl.BlockSpec((tm, D_MODEL), lambda i, off: (i, 0)),
                  pl.BlockSpec(memory_space=pl.ANY)],
        out_specs=pl.BlockSpec(memory_space=pl.ANY),
        scratch_shapes=[pltpu.SemaphoreType.DMA],
    )
    return pl.pallas_call(
        _dispatch_kernel,
        out_shape=jax.ShapeDtypeStruct(xs_init.shape, xs_init.dtype),
        grid_spec=grid_spec,
        input_output_aliases={4: 0},
        compiler_params=pltpu.CompilerParams(dimension_semantics=("arbitrary",),
                                             vmem_limit_bytes=VMEM_LIMIT),
        name="dispatch",
    )(offsets, ids_flat, rk_flat, x1, xs_init)


def _expert_kernel(te_ref, nu_ref, xs_ref, wgu_ref, bgu_ref, wdn_ref, bdn_ref, y_ref,
                   wgu_b, wdn_b):
    t = pl.program_id(0)
    e = te_ref[t]
    valid = t < nu_ref[0]
    fresh = jnp.logical_or(t == 0, te_ref[jnp.maximum(t - 1, 0)] != e)

    @pl.when(jnp.logical_and(valid, fresh))
    def _():
        for r in range(0, D_MODEL, LANES):
            wgu_b[r:r + LANES, :] = wgu_ref[r:r + LANES, :].astype(BF16)
        for r in range(0, D_FF, LANES):
            wdn_b[r:r + LANES, :] = wdn_ref[r:r + LANES, :].astype(BF16)

    @pl.when(valid)
    def _():
        h = _dot(xs_ref[...].astype(BF16), wgu_b[...]) + bgu_ref[...]
        glu = jnp.minimum(h[:, :D_FF], SWIGLU_LIMIT)
        lin = jnp.clip(h[:, D_FF:], -SWIGLU_LIMIT, SWIGLU_LIMIT)
        act = (lin + 1.0) * (glu * (1.0 / (1.0 + jnp.exp(-SWIGLU_ALPHA * glu))))
        y_ref[...] = _dot(act.astype(BF16), wdn_b[...]) + bdn_ref[...]

    @pl.when(jnp.logical_not(valid))
    def _():
        y_ref[...] = jnp.zeros_like(y_ref)


def _experts(tile_e, n_used, xs, wgu, bgu, wdn, bdn, layer):
    p = xs.shape[0]
    tg = EXPERT_TG
    n_tiles = p // tg
    wspec = lambda r, c: pl.BlockSpec((None, None, r, c), lambda t, te, nu: (layer, te[t], 0, 0))
    grid_spec = pltpu.PrefetchScalarGridSpec(
        num_scalar_prefetch=2,
        grid=(n_tiles,),
        in_specs=[pl.BlockSpec((tg, D_MODEL), lambda t, te, nu: (t, 0)),
                  wspec(D_MODEL, 2 * D_FF), wspec(1, 2 * D_FF), wspec(D_FF, D_MODEL), wspec(1, D_MODEL)],
        out_specs=pl.BlockSpec((tg, D_MODEL), lambda t, te, nu: (t, 0)),
        scratch_shapes=[pltpu.VMEM((D_MODEL, 2 * D_FF), BF16), pltpu.VMEM((D_FF, D_MODEL), BF16)],
    )
    return pl.pallas_call(
        _expert_kernel,
        out_shape=jax.ShapeDtypeStruct((p, D_MODEL), F32),
        grid_spec=grid_spec,
        compiler_params=pltpu.CompilerParams(dimension_semantics=("arbitrary",),
                                             vmem_limit_bytes=VMEM_LIMIT),
        name="experts",
    )(tile_e, n_used, xs, wgu, bgu, wdn, bdn)


def _combine_kernel(off_ref, ids_ref, rk_ref, x1_ref, gt_ref, g2_ref, b2_ref, y_ref, o_ref,
                    buf, sem, *, alpha):
    tm = x1_ref.shape[0]

    def issue(tok, carry):
        for j in range(TOP_K):
            a = tok * TOP_K + j
            _row_copy(y_ref, off_ref[ids_ref[a]] + rk_ref[a], buf.at[j], tok, sem).start()
        return carry

    lax.fori_loop(0, tm, issue, 0)

    def drain(a, carry):
        _row_copy(y_ref, 0, buf.at[0], 0, sem).wait()
        return carry

    lax.fori_loop(0, tm * TOP_K, drain, 0)

    f = jnp.zeros((tm, D_MODEL), F32)
    for j in range(TOP_K):
        f = f + gt_ref[:, j:j + 1] * buf[j]
    o_ref[...] = _layer_norm(alpha * x1_ref[...] + f, g2_ref[...], b2_ref[...])


def _combine(offsets, ids_flat, rk_flat, x1, gates, g2, b2, y, alpha):
    n = x1.shape[0]
    tm = min(COMBINE_TM, n)
    assert n % tm == 0
    smem = pl.BlockSpec((tm * TOP_K,), lambda i, off: (i,), memory_space=pltpu.SMEM)
    grid_spec = pltpu.PrefetchScalarGridSpec(
        num_scalar_prefetch=1,
        grid=(n // tm,),
        in_specs=[smem, smem,
                  pl.BlockSpec((tm, D_MODEL), lambda i, off: (i, 0)),
                  pl.BlockSpec((tm, LANES), lambda i, off: (i, 0)),
                  pl.BlockSpec((1, D_MODEL), lambda i, off: (0, 0)),
                  pl.BlockSpec((1, D_MODEL), lambda i, off: (0, 0)),
                  pl.BlockSpec(memory_space=pl.ANY)],
        out_specs=pl.BlockSpec((tm, D_MODEL), lambda i, off: (i, 0)),
        scratch_shapes=[pltpu.VMEM((TOP_K, tm, D_MODEL), F32), pltpu.SemaphoreType.DMA],
    )
    return pl.pallas_call(
        functools.partial(_combine_kernel, alpha=alpha),
        out_shape=jax.ShapeDtypeStruct((n, D_MODEL), F32),
        grid_spec=grid_spec,
        compiler_params=pltpu.CompilerParams(dimension_semantics=("arbitrary",),
                                             vmem_limit_bytes=VMEM_LIMIT),
        name="combine",
    )(offsets, ids_flat, rk_flat, x1, gates, g2, b2, y)


def _moe_kernel(te_ref, nu_ref, srcc_ref, srcn_ref, dstp_ref, dstc_ref,
                x_hbm, wgu_ref, bgu_ref, wdn_ref, bdn_ref, y_hbm,
                wgu_b, wdn_b, xbuf, ybuf, h_ref, act_ref, gsem, ssem, *, dump_row):
    tg = EXPERT_TG
    t = pl.program_id(0)
    nu = nu_ref[0]
    slot = lax.rem(t, 2)
    other = 1 - slot

    def gather(src_ref, r, sl):
        return pltpu.make_async_copy(x_hbm.at[pl.ds(src_ref[r], 1), :],
                                     xbuf.at[sl, pl.ds(r, 1), :], gsem.at[sl])

    def scatter(dst_row, r, sl):
        return pltpu.make_async_copy(ybuf.at[sl, pl.ds(r, 1), :],
                                     y_hbm.at[pl.ds(dst_row, 1), :], ssem.at[sl])

    def tile_gather(sl):
        return pltpu.make_async_copy(x_hbm.at[pl.ds(0, tg), :], xbuf.at[sl], gsem.at[sl])

    def tile_scatter(sl, row):
        return pltpu.make_async_copy(ybuf.at[sl], y_hbm.at[pl.ds(row, tg), :], ssem.at[sl])

    @pl.when(t == 0)
    def _():
        ybuf[...] = jnp.zeros(ybuf.shape, F32)
        for sl in range(2):
            cp = tile_scatter(sl, dump_row + sl * tg)
            cp.start()
            cp.wait()

        def body(r, carry):
            gather(srcc_ref, r, 0).start()
            return carry
        lax.fori_loop(0, tg, body, 0)

    @pl.when(t < nu)
    def _():
        tile_gather(slot).wait()

        @pl.when(t >= 1)
        def _():
            tile_scatter(slot, 0).wait()

        @pl.when(jnp.logical_or(t == 0, te_ref[jnp.maximum(t - 1, 0)] != te_ref[t]))
        def _():
            for r in range(0, D_MODEL, LANES):
                wgu_b[r:r + LANES, :] = wgu_ref[r:r + LANES, :].astype(BF16)
            for r in range(0, D_FF, LANES):
                wdn_b[r:r + LANES, :] = wdn_ref[r:r + LANES, :].astype(BF16)

        xb = xbuf[slot].astype(BF16)
        n1 = 2 * D_FF // MOE_COLS
        for c in range(n1):
            cols = slice(c * MOE_COLS, (c + 1) * MOE_COLS)
            h_ref[:, cols] = _dot(xb, wgu_b[:, cols]) + bgu_ref[:, cols]
            for r in range(c * tg // n1, (c + 1) * tg // n1):
                gather(srcn_ref, r, other).start()

        h = h_ref[...]
        glu = jnp.minimum(h[:, :D_FF], SWIGLU_LIMIT)
        lin = jnp.clip(h[:, D_FF:], -SWIGLU_LIMIT, SWIGLU_LIMIT)
        act = (lin + 1.0) * (glu * (1.0 / (1.0 + jnp.exp(-SWIGLU_ALPHA * glu))))
        act_ref[...] = act.astype(BF16)

        first = t == 0
        n2 = D_MODEL // MOE_COLS
        for c in range(n2):
            cols = slice(c * MOE_COLS, (c + 1) * MOE_COLS)
            ybuf[slot, :, cols] = _dot(act_ref[...], wdn_b[:, cols]) + bdn_ref[:, cols]
            for r in range(c * tg // n2, (c + 1) * tg // n2):
                scatter(jnp.where(first, dump_row + tg + r, dstp_ref[r]), r, other).start()

        @pl.when(t == nu - 1)
        def _():
            def body(r, carry):
                scatter(dstc_ref[r], r, slot).start()
                return carry
            lax.fori_loop(0, tg, body, 0)
            tile_scatter(other, 0).wait()
            tile_scatter(slot, 0).wait()
            tile_gather(other).wait()


def _moe_experts(tile_e, n_used, src, dst, x1, wgu, bgu, wdn, bdn, layer):
    n = x1.shape[0]
    tg = EXPERT_TG
    n_tiles = src.shape[0] // tg
    dump_row = TOP_K * n
    wspec = lambda r, c: pl.BlockSpec((None, None, r, c), lambda t, te, nu: (layer, te[t], 0, 0))
    idx = lambda f: pl.BlockSpec((tg,), lambda t, te, nu: (f(t),), memory_space=pltpu.SMEM)
    grid_spec = pltpu.PrefetchScalarGridSpec(
        num_scalar_prefetch=2,
        grid=(n_tiles,),
        in_specs=[idx(lambda t: t), idx(lambda t: jnp.minimum(t + 1, n_tiles - 1)),
                  idx(lambda t: jnp.maximum(t - 1, 0)), idx(lambda t: t),
                  pl.BlockSpec(memory_space=pl.ANY),
                  wspec(D_MODEL, 2 * D_FF), wspec(1, 2 * D_FF), wspec(D_FF, D_MODEL), wspec(1, D_MODEL)],
        out_specs=pl.BlockSpec(memory_space=pl.ANY),
        scratch_shapes=[pltpu.VMEM((D_MODEL, 2 * D_FF), BF16), pltpu.VMEM((D_FF, D_MODEL), BF16),
                        pltpu.VMEM((2, tg, D_MODEL), F32), pltpu.VMEM((2, tg, D_MODEL), F32),
                        pltpu.VMEM((tg, 2 * D_FF), F32), pltpu.VMEM((tg, D_FF), BF16),
                        pltpu.SemaphoreType.DMA((2,)), pltpu.SemaphoreType.DMA((2,))],
    )
    return pl.pallas_call(
        functools.partial(_moe_kernel, dump_row=dump_row),
        out_shape=jax.ShapeDtypeStruct((dump_row + 2 * tg, D_MODEL), F32),
        grid_spec=grid_spec,
        compiler_params=pltpu.CompilerParams(dimension_semantics=("arbitrary",),
                                             vmem_limit_bytes=VMEM_LIMIT),
        name="experts",
    )(tile_e, n_used, src, src, dst, dst, x1, wgu, bgu, wdn, bdn)


def _moe_combine_kernel(x1_ref, gt_ref, g2_ref, b2_ref, y0_ref, y1_ref, y2_ref, y3_ref, o_ref, *, alpha):
    f = jnp.zeros(x1_ref.shape, F32)
    for j, y_ref in enumerate((y0_ref, y1_ref, y2_ref, y3_ref)):
        f = f + gt_ref[:, j:j + 1] * y_ref[...]
    o_ref[...] = _layer_norm(alpha * x1_ref[...] + f, g2_ref[...], b2_ref[...])


def _moe_combine(x1, gates, g2, b2, y, alpha):
    n = x1.shape[0]
    tm = min(MERGE_TM, n)
    nblk = n // tm
    row = lambda w: pl.BlockSpec((tm, w), lambda i: (i, 0))
    yspec = lambda j: pl.BlockSpec((tm, D_MODEL), lambda i: (j * nblk + i, 0))
    return pl.pallas_call(
        functools.partial(_moe_combine_kernel, alpha=alpha),
        out_shape=jax.ShapeDtypeStruct((n, D_MODEL), F32),
        grid=(nblk,),
        in_specs=[row(D_MODEL), row(LANES), _const_spec((1, D_MODEL)), _const_spec((1, D_MODEL)),
                  yspec(0), yspec(1), yspec(2), yspec(3)],
        out_specs=row(D_MODEL),
        compiler_params=pltpu.CompilerParams(dimension_semantics=("arbitrary",),
                                             vmem_limit_bytes=VMEM_LIMIT),
        name="combine",
    )(x1, gates, g2, b2, y, y, y, y)


def _sorted_row_maps(offsets, ids, ranks, n_rows):
    n = ids.shape[0]
    tg = EXPERT_TG
    pos = (offsets[ids[:, :TOP_K]] + ranks[:, :TOP_K]).reshape(-1)
    tok = jnp.repeat(jnp.arange(n, dtype=I32), TOP_K)
    slot = jnp.tile(jnp.arange(TOP_K, dtype=I32), n)
    rows = jnp.arange(n_rows, dtype=I32)
    src = jnp.zeros((n_rows,), I32).at[pos].set(tok, unique_indices=True)
    pad_dst = TOP_K * n + ((rows // tg) % 2) * tg + rows % tg
    dst = pad_dst.at[pos].set(slot * n + tok, unique_indices=True)
    return src, dst


def _slab_to_rows(x):
    t = x.shape[0]
    y = jnp.swapaxes(x.reshape(t // 8, 8, 8, LANES), 1, 2)
    return jnp.concatenate([y[:, c].reshape(t, LANES) for c in range(8)], axis=1)


def _rows_to_slab(v):
    t = v.shape[0]
    y = jnp.stack([v[:, c * LANES:(c + 1) * LANES].reshape(t // 8, 8, LANES) for c in range(8)], axis=1)
    return jnp.swapaxes(y, 1, 2).reshape(t, 8, LANES)


def _issue_rows(n_tok, copy):
    def body(g, carry):
        for u in range(8):
            for j in range(TOP_K):
                copy(g * 8 + u, j).start()
        return carry
    lax.fori_loop(0, n_tok // 8, body, 0)


def _scatter_kernel(zs_ref, pos_ref, x_ref, xs_ref, zbuf, sem):
    tm = x_ref.shape[0]
    tg = zbuf.shape[0]

    @pl.when(pl.program_id(0) == 0)
    def _():
        zbuf[...] = jnp.zeros(zbuf.shape, F32)
        for e in range(N_EXPERTS):
            cp = pltpu.make_async_copy(zbuf, xs_ref.at[pl.ds(zs_ref[e], tg)], sem)
            cp.start()
            cp.wait()
        n_tiles = xs_ref.shape[0] // tg
        for k in range(N_EXPERTS):
            @pl.when(n_tiles - 1 - k >= zs_ref[N_EXPERTS])
            def _():
                cp = pltpu.make_async_copy(zbuf, xs_ref.at[pl.ds((n_tiles - 1 - k) * tg, tg)], sem)
                cp.start()
                cp.wait()

    _issue_rows(tm, lambda tok, j: pltpu.make_async_copy(
        x_ref.at[tok], xs_ref.at[pos_ref[tok * TOP_K + j]], sem))
    for j in range(TOP_K):
        pltpu.make_async_copy(x_ref, xs_ref.at[pl.ds(0, tm)], sem).wait()


def _scatter_rows(zstart, pos_flat, x1r, n_rows):
    n = x1r.shape[0]
    tm = min(DISPATCH_TM, n)
    assert n % tm == 0
    grid_spec = pltpu.PrefetchScalarGridSpec(
        num_scalar_prefetch=1,
        grid=(n // tm,),
        in_specs=[pl.BlockSpec((tm * TOP_K,), lambda i, zs: (i,), memory_space=pltpu.SMEM),
                  pl.BlockSpec((tm, 8, LANES), lambda i, zs: (i, 0, 0))],
        out_specs=pl.BlockSpec(memory_space=pl.ANY),
        scratch_shapes=[pltpu.VMEM((EXPERT_TG, 8, LANES), F32), pltpu.SemaphoreType.DMA],
    )
    return pl.pallas_call(
        _scatter_kernel,
        out_shape=jax.ShapeDtypeStruct((n_rows, 8, LANES), F32),
        grid_spec=grid_spec,
        compiler_params=pltpu.CompilerParams(dimension_semantics=("arbitrary",),
                                             vmem_limit_bytes=VMEM_LIMIT),
        name="dispatch",
    )(zstart, pos_flat, x1r)


def _ffn_kernel(te_ref, nu_ref, xs_ref, wgu_ref, bgu_ref, wdn_ref, bdn_ref, y_ref, wgu_b, wdn_b):
    t = pl.program_id(0)
    valid = t < nu_ref[0]
    fresh = jnp.logical_or(t == 0, te_ref[jnp.maximum(t - 1, 0)] != te_ref[t])

    @pl.when(jnp.logical_and(valid, fresh))
    def _():
        for r in range(0, D_MODEL, LANES):
            wgu_b[r:r + LANES, :] = wgu_ref[r:r + LANES, :].astype(BF16)
        for r in range(0, D_FF, LANES):
            wdn_b[r:r + LANES, :] = wdn_ref[r:r + LANES, :].astype(BF16)

    @pl.when(valid)
    def _():
        h = _dot(_slab_to_rows(xs_ref[...]).astype(BF16), wgu_b[...]) + bgu_ref[...]
        glu = jnp.minimum(h[:, :D_FF], SWIGLU_LIMIT)
        lin = jnp.clip(h[:, D_FF:], -SWIGLU_LIMIT, SWIGLU_LIMIT)
        act = (lin + 1.0) * (glu * (1.0 / (1.0 + jnp.exp(-SWIGLU_ALPHA * glu))))
        y_ref[...] = _rows_to_slab(_dot(act.astype(BF16), wdn_b[...]) + bdn_ref[...])

    @pl.when(jnp.logical_not(valid))
    def _():
        y_ref[...] = jnp.zeros(y_ref.shape, F32)


def _ffn(tile_e, n_used, xs, wgu, bgu, wdn, bdn, layer):
    tg = EXPERT_TG
    n_tiles = xs.shape[0] // tg
    wspec = lambda r, c: pl.BlockSpec((None, None, r, c), lambda t, te, nu: (layer, te[t], 0, 0))
    slab = pl.BlockSpec((tg, 8, LANES), lambda t, te, nu: (t, 0, 0))
    grid_spec = pltpu.PrefetchScalarGridSpec(
        num_scalar_prefetch=2,
        grid=(n_tiles,),
        in_specs=[slab, wspec(D_MODEL, 2 * D_FF), wspec(1, 2 * D_FF), wspec(D_FF, D_MODEL),
                  wspec(1, D_MODEL)],
        out_specs=slab,
        scratch_shapes=[pltpu.VMEM((D_MODEL, 2 * D_FF), BF16), pltpu.VMEM((D_FF, D_MODEL), BF16)],
    )
    return pl.pallas_call(
        _ffn_kernel,
        out_shape=jax.ShapeDtypeStruct(xs.shape, F32),
        grid_spec=grid_spec,
        compiler_params=pltpu.CompilerParams(dimension_semantics=("arbitrary",),
                                             vmem_limit_bytes=VMEM_LIMIT),
        name="experts",
    )(tile_e, n_used, xs, wgu, bgu, wdn, bdn)


def _gather_kernel(posc_ref, posn_ref, x1_ref, gt_ref, g2_ref, b2_ref, y_ref, o_ref, buf, sem, *, alpha):
    tm = x1_ref.shape[0]
    i = pl.program_id(0)
    slot = lax.rem(i, 2)

    def issue(pos_ref, sl):
        _issue_rows(tm, lambda tok, j: pltpu.make_async_copy(
            y_ref.at[pos_ref[tok * TOP_K + j]], buf.at[sl, j, tok], sem.at[sl]))

    @pl.when(i == 0)
    def _():
        issue(posc_ref, 0)

    @pl.when(i + 1 < pl.num_programs(0))
    def _():
        issue(posn_ref, 1 - slot)

    for j in range(TOP_K):
        pltpu.make_async_copy(y_ref.at[pl.ds(0, tm)], buf.at[slot, j], sem.at[slot]).wait()

    f = jnp.zeros((tm, D_MODEL), F32)
    for j in range(TOP_K):
        f = f + gt_ref[:, j:j + 1] * _slab_to_rows(buf[slot, j])
    o_ref[...] = _layer_norm(alpha * x1_ref[...] + f, g2_ref[...], b2_ref[...])


def _gather_rows(pos_flat, x1, gates, g2, b2, y, alpha):
    n = x1.shape[0]
    tm = min(COMBINE_TM, n)
    assert n % tm == 0
    nblk = n // tm
    smem = lambda f: pl.BlockSpec((tm * TOP_K,), lambda i: (f(i),), memory_space=pltpu.SMEM)
    return pl.pallas_call(
        functools.partial(_gather_kernel, alpha=alpha),
        out_shape=jax.ShapeDtypeStruct((n, D_MODEL), F32),
        grid=(nblk,),
        in_specs=[smem(lambda i: i), smem(lambda i: jnp.minimum(i + 1, nblk - 1)),
                  pl.BlockSpec((tm, D_MODEL), lambda i: (i, 0)),
                  pl.BlockSpec((tm, LANES), lambda i: (i, 0)),
                  _const_spec((1, D_MODEL)), _const_spec((1, D_MODEL)),
                  pl.BlockSpec(memory_space=pl.ANY)],
        out_specs=pl.BlockSpec((tm, D_MODEL), lambda i: (i, 0)),
        scratch_shapes=[pltpu.VMEM((2, TOP_K, tm, 8, LANES), F32), pltpu.SemaphoreType.DMA((2,))],
        compiler_params=pltpu.CompilerParams(dimension_semantics=("arbitrary",),
                                             vmem_limit_bytes=VMEM_LIMIT),
        name="combine",
    )(pos_flat, pos_flat, x1, gates, g2, b2, y)


def _sorted_positions(offsets, counts, n_used, ids, ranks, n_rows):
    e = ids[:, :TOP_K]
    onehot = e[..., None] == jnp.arange(N_EXPERTS, dtype=I32)
    pos = ranks[:, :TOP_K] + jnp.sum(jnp.where(onehot, offsets[:N_EXPERTS], 0), axis=-1)
    cnt = counts[0, :N_EXPERTS].astype(I32)
    zstart = jnp.minimum(offsets[:N_EXPERTS] + cnt, n_rows - EXPERT_TG)
    return pos.reshape(-1).astype(I32), jnp.concatenate([zstart.astype(I32), n_used])


def _rope_tables(positions):
    pos = positions.reshape(-1).astype(F32)[:, None]
    n = pos.shape[0]

    def cs(rot_dim):
        half = rot_dim // 2
        inv_freq = ROPE_THETA ** (-jnp.arange(half, dtype=F32) / half)
        ang = pos * inv_freq
        return jnp.cos(ang), jnp.sin(ang)

    cos_p, sin_p = cs(A_ROT_DIM)
    cos_c, sin_c = cs(C_ROPE_DIM)
    hp, hc = A_ROT_DIM // 2, C_ROPE_DIM // 2
    one = lambda w: jnp.ones((n, w), F32)
    zero = lambda w: jnp.zeros((n, w), F32)

    a64 = jnp.concatenate([cos_p, cos_p, one(64 - 2 * hp)] * 2, axis=1)
    bm64 = jnp.concatenate([-sin_p, zero(64 - hp)] * 2, axis=1)
    bp64 = jnp.concatenate([zero(hp), sin_p, zero(64 - 2 * hp)] * 2, axis=1)
    a128 = jnp.concatenate([one(64), cos_c, cos_c, one(32)], axis=1)
    bm128 = jnp.concatenate([zero(64), -sin_c, zero(64 - hc)], axis=1)
    bp128 = jnp.concatenate([zero(64 + hc), sin_c, zero(32)], axis=1)
    iw_scale = jnp.full((n, IDX_HEADS), IDX_HEADS ** -0.5, F32)
    as2 = jnp.concatenate([iw_scale, zero(64 - IDX_HEADS), cos_c, cos_c, zero(32)], axis=1)
    return {"t64": jnp.stack([a64, bm64, bp64]),
            "t128": jnp.stack([a128, bm128, bp128]),
            "ts2": jnp.stack([as2, bm128, bp128])}


def _layer_weights(l, w_in, gm_norm_g, gm_norm_b, gm_w_s, gm_b_s, mla_q_norm, mla_kv_norm,
                   mla_w_uq, mla_w_ukv, w_branch, w_out, ln1_g, ln1_b, router_w, router_b):
    w = w_in[l]
    o = 0
    cols = {}
    for name, width in (("aq", 512), ("ak", 512), ("av", 512), ("iq", 512), ("ik", 64), ("iw", 8),
                        ("buv", 1024), ("cdq", 256), ("cdkv", 128), ("ckr", 32), ("gate", 3072)):
        cols[name] = w[:, o:o + width]
        o += width
    zc = lambda width: jnp.zeros((D_MODEL, width), F32)
    s2 = jnp.concatenate([cols["iw"], zc(64 - IDX_HEADS), cols["ckr"], zc(32)], axis=1)
    wp = jnp.concatenate([cols["aq"] * (A_HEAD_DIM ** -0.5 * LOG2E), cols["ak"],
                          cols["iq"] * (IDX_HEAD_DIM ** -0.5), cols["av"],
                          cols["ik"], cols["ik"], s2, cols["buv"], cols["cdq"], cols["cdkv"]], axis=1)

    uq = mla_w_uq[l].reshape(C_Q_RANK, C_HEADS, C_NOPE_DIM + C_ROPE_DIM)
    wuq = jnp.pad(uq, ((0, 0), (0, 0), (0, LANES - C_NOPE_DIM - C_ROPE_DIM))).reshape(C_Q_RANK, -1)
    wuq = wuq * ((C_NOPE_DIM + C_ROPE_DIM) ** -0.5 * LOG2E)
    ukv = mla_w_ukv[l].reshape(C_KV_RANK, C_HEADS, C_NOPE_DIM + C_V_DIM)
    wk = jnp.pad(ukv[:, :, :C_NOPE_DIM], ((0, 0), (0, 0), (0, LANES - C_NOPE_DIM))).reshape(C_KV_RANK, -1)
    wv = ukv[:, :, C_NOPE_DIM:].reshape(C_KV_RANK, -1)
    src = jnp.arange(LANES)[:, None]
    dst = jnp.arange(C_HEADS * LANES)[None, :]
    place = ((dst % LANES == src) & (src >= C_NOPE_DIM) & (src < C_NOPE_DIM + C_ROPE_DIM)).astype(F32)
    wke = jnp.concatenate([wk, place], axis=0)

    tril = jnp.tril(jnp.ones((B_CHUNK, B_CHUNK), dtype=bool))
    ws = jnp.where(tril[None], gm_w_s[l], 0)
    bsf = jnp.repeat(gm_b_s[l].T, B_GROUP_DIM, axis=1)
    rw = jnp.pad(router_w[l], ((0, 0), (0, LANES - N_EXPERTS)))
    rb = jnp.pad(router_b[l], (0, LANES - N_EXPERTS))[None, :]
    return {
        "wp": wp.astype(BF16), "wuq": wuq.astype(BF16), "wke": wke.astype(BF16), "wv": wv.astype(BF16),
        "ws": ws.astype(BF16), "gmg": gm_norm_g[l][None, :], "gmb": gm_norm_b[l][None, :], "bsf": bsf,
        "qn": mla_q_norm[l][None, :], "kvn": mla_kv_norm[l][None, :],
        "wg": cols["gate"].astype(BF16), "wbr": w_branch[l].astype(BF16), "wo": w_out[l].astype(BF16),
        "g1": ln1_g[l][None, :], "b1": ln1_b[l][None, :], "rw": rw, "rb": rb,
    }


def _expert_tiles(counts, n_tiles):
    tg = EXPERT_TG
    cnt = counts[0, :N_EXPERTS].astype(I32)
    tiles_per = (cnt + tg - 1) // tg
    tile_end = jnp.cumsum(tiles_per)
    offsets = (tile_end - tiles_per) * tg
    n_used = tile_end[-1]
    tid = jnp.minimum(jnp.arange(n_tiles, dtype=I32), n_used - 1)
    tile_e = jnp.sum(tid[:, None] >= tile_end[None, :], axis=1).astype(I32)
    off128 = jnp.pad(offsets, (0, LANES - N_EXPERTS)).astype(I32)
    return off128, tile_e, n_used.reshape(1).astype(I32)


def _layer(l, x2, tabs, batch, seq, depth, p):
    alpha = (2 * depth) ** 0.25
    n = x2.shape[0]
    wts = _layer_weights(l, p["w_in"], p["gm_norm_g"], p["gm_norm_b"], p["gm_w_s"], p["gm_b_s"],
                         p["mla_q_norm"], p["mla_kv_norm"], p["mla_w_uq"], p["mla_w_ukv"],
                         p["w_branch"], p["w_out"], p["ln1_g"], p["ln1_b"], p["router_w"], p["router_b"])
    aq, ak, iq, avt, ik2, iwt, o_b, q, k, vt = _proj(x2, wts, tabs)
    o_a = _dsa(iq, iwt, aq, ik2, ak, avt, batch, seq)
    o_c = _mla(q, k, vt, batch, seq)
    x1, x1r, ids, ranks, gates, counts = _merge(x2, o_a, o_b, o_c, wts, alpha)

    n_tiles = n * TOP_K // EXPERT_TG + N_EXPERTS
    offsets, tile_e, n_used = _expert_tiles(counts, n_tiles)
    n_rows = n_tiles * EXPERT_TG
    pos_flat, zstart = _sorted_positions(offsets, counts, n_used, ids, ranks, n_rows)
    xs = _scatter_rows(zstart, pos_flat, x1r, n_rows)
    y = _ffn(tile_e, n_used, xs, p["exp_w_gu"], p["exp_b_gu"][:, :, None, :],
             p["exp_w_dn"], p["exp_b_dn"][:, :, None, :], l)
    return _gather_rows(pos_flat, x1, gates, p["ln2_g"][l][None, :], p["ln2_b"][l][None, :], y, alpha)


def kernel(x, positions, w_in, gm_norm_g, gm_norm_b, gm_w_s, gm_b_s, mla_q_norm, mla_kv_norm, mla_w_uq, mla_w_ukv, w_branch, w_out, ln1_g, ln1_b, router_w, router_b, exp_w_gu, exp_b_gu, exp_w_dn, exp_b_dn, ln2_g, ln2_b):
    batch, seq, _ = x.shape
    depth = w_in.shape[0]
    p = dict(w_in=w_in, gm_norm_g=gm_norm_g, gm_norm_b=gm_norm_b, gm_w_s=gm_w_s, gm_b_s=gm_b_s,
             mla_q_norm=mla_q_norm, mla_kv_norm=mla_kv_norm, mla_w_uq=mla_w_uq, mla_w_ukv=mla_w_ukv,
             w_branch=w_branch, w_out=w_out, ln1_g=ln1_g, ln1_b=ln1_b, router_w=router_w,
             router_b=router_b, exp_w_gu=exp_w_gu, exp_b_gu=exp_b_gu, exp_w_dn=exp_w_dn,
             exp_b_dn=exp_b_dn, ln2_g=ln2_g, ln2_b=ln2_b)
    tabs = _rope_tables(positions)
    x2 = x.reshape(batch * seq, D_MODEL)
    for l in range(depth):
        x2 = _layer(l, x2, tabs, batch, seq, depth, p)
    return x2.reshape(batch, seq, D_MODEL)
```

```python
import functools
import math

import jax
import jax.numpy as jnp
from jax import lax
from jax.experimental import pallas as pl
from jax.experimental.pallas import tpu as pltpu

F32 = jnp.float32
BF16 = jnp.bfloat16
I32 = jnp.int32

D_MODEL = 1024
BRANCH_WIDTH = 512
N_BRANCH = 3
ROPE_THETA = 500000.0
LN_EPS = 1e-5
RMS_EPS = 1e-6

A_HEADS = 8
A_HEAD_DIM = 64
A_ROT_DIM = 16
IDX_HEADS = 8
IDX_HEAD_DIM = 64
IDX_TOPK_MAX = 256

B_GROUPS = 8
B_GROUP_DIM = 64
B_CHUNK = 128

C_HEADS = 8
C_NOPE_DIM = 64
C_ROPE_DIM = 32
C_V_DIM = 64
C_Q_RANK = 256
C_KV_RANK = 128

N_EXPERTS = 32
TOP_K = 4
D_FF = 1024
SWIGLU_LIMIT = 7.0
SWIGLU_ALPHA = 1.702

LANES = 128
ATT_T = 512
PROJ_TM = 512
MERGE_TM = 256
DISPATCH_TM = 256
COMBINE_TM = 256
EXPERT_TG = 256
VMEM_LIMIT = 56 * 1024 * 1024

FLASH_LOOKAHEAD = 2
NEG = -1e30
LOG2E = math.log2(math.e)
INT_MIN = -(2 ** 31)

_C_AQ, _C_AK, _C_IQ, _C_AV = 0, 512, 1024, 1536
_C_S1, _C_S2, _C_B, _C_C, _C_END = 2048, 2176, 2304, 3328, 3712


def _dot(a, b):
    return jnp.dot(a, b, preferred_element_type=F32)


def _dot_nt(a, b):
    return lax.dot_general(a, b, (((1,), (1,)), ((), ())), preferred_element_type=F32)


def _rot(y, a, bm, bp, r):
    w = y.shape[1]
    return y * a + pltpu.roll(y, w - r, 1) * bm + pltpu.roll(y, r, 1) * bp


def _layer_norm(x, g, b):
    mu = jnp.mean(x, axis=-1, keepdims=True)
    xc = x - mu
    var = jnp.mean(xc * xc, axis=-1, keepdims=True)
    return xc * lax.rsqrt(var + LN_EPS) * g + b


def _rms_norm(x, g):
    return x * lax.rsqrt(jnp.mean(x * x, axis=-1, keepdims=True) + RMS_EPS) * g


def _const_spec(shape):
    nd = len(shape)
    return pl.BlockSpec(shape, lambda *_: (0,) * nd)


def _proj_kernel(x_ref, wp_ref, wuq_ref, wke_ref, wv_ref, ws_ref, gmg_ref, gmb_ref, bsf_ref,
                 qn_ref, kvn_ref, t64_ref, t128_ref, ts2_ref,
                 aq_o, ak_o, iq_o, avt_o, ik2_o, iwt_o, ob_o, q_o, k_o, vt_o):
    tm = x_ref.shape[0]
    xb = x_ref[...].astype(BF16)

    def mm(c0, c1):
        return _dot(xb, wp_ref[:, c0:c1])

    a64, bm64, bp64 = t64_ref[0], t64_ref[1], t64_ref[2]
    a4 = jnp.concatenate([a64] * 4, axis=1)
    bm4 = jnp.concatenate([bm64] * 4, axis=1)
    bp4 = jnp.concatenate([bp64] * 4, axis=1)
    for c0, out in ((_C_AQ, aq_o), (_C_AK, ak_o), (_C_IQ, iq_o)):
        out[...] = _rot(mm(c0, c0 + 512), a4, bm4, bp4, A_ROT_DIM // 2).astype(BF16)
    avt_o[...] = mm(_C_AV, _C_AV + 512).T.astype(BF16)
    ik2_o[...] = _rot(mm(_C_S1, _C_S1 + LANES), a64, bm64, bp64, A_ROT_DIM // 2).astype(BF16)
    sm2 = _rot(mm(_C_S2, _C_S2 + LANES), ts2_ref[0], ts2_ref[1], ts2_ref[2], C_ROPE_DIM // 2)
    iwt_o[...] = sm2.T[:IDX_HEADS, :]

    buv = mm(_C_B, _C_B + 2 * BRANCH_WIDTH)
    buv = 0.5 * buv * (1.0 + lax.erf(buv * (1.0 / math.sqrt(2.0))))
    u = buv[:, :BRANCH_WIDTH]
    v = _layer_norm(buv[:, BRANCH_WIDTH:], gmg_ref[...], gmb_ref[...]).astype(BF16)
    lane = lax.broadcasted_iota(I32, (B_CHUNK, LANES), 1)
    for c in range(tm // B_CHUNK):
        rows = slice(c * B_CHUNK, (c + 1) * B_CHUNK)
        for pr in range(B_GROUPS // 2):
            cols = slice(pr * LANES, (pr + 1) * LANES)
            vp = v[rows, cols]
            s = jnp.where(lane < B_GROUP_DIM, _dot(ws_ref[2 * pr], vp), _dot(ws_ref[2 * pr + 1], vp))
            ob_o[rows, cols] = (u[rows, cols] * (s + bsf_ref[:, cols])).astype(BF16)

    cc = mm(_C_C, _C_END)
    cq = _rms_norm(cc[:, :C_Q_RANK], qn_ref[...]).astype(BF16)
    a8 = jnp.concatenate([t128_ref[0]] * C_HEADS, axis=1)
    bm8 = jnp.concatenate([t128_ref[1]] * C_HEADS, axis=1)
    bp8 = jnp.concatenate([t128_ref[2]] * C_HEADS, axis=1)
    q_o[...] = _rot(_dot(cq, wuq_ref[...]), a8, bm8, bp8, C_ROPE_DIM // 2).astype(BF16)
    ckv = _rms_norm(cc[:, C_Q_RANK:], kvn_ref[...]).astype(BF16)
    kin = jnp.concatenate([ckv, sm2.astype(BF16)], axis=1)
    k_o[...] = _dot(kin, wke_ref[...]).astype(BF16)
    vt_o[...] = _dot(ckv, wv_ref[...]).T.astype(BF16)


def _proj(x2, wts, tabs):
    n = x2.shape[0]
    tm = min(PROJ_TM, n)
    assert n % tm == 0 and tm % B_CHUNK == 0
    row = lambda w: pl.BlockSpec((tm, w), lambda i: (i, 0))
    col = lambda h: pl.BlockSpec((h, tm), lambda i: (0, i))
    tab = pl.BlockSpec((3, tm, LANES), lambda i: (0, i, 0))
    in_specs = [
        row(D_MODEL),
        _const_spec((D_MODEL, _C_END)),
        _const_spec((C_Q_RANK, C_HEADS * LANES)),
        _const_spec((C_KV_RANK + LANES, C_HEADS * LANES)),
        _const_spec((C_KV_RANK, C_HEADS * C_V_DIM)),
        _const_spec((B_GROUPS, B_CHUNK, B_CHUNK)),
        _const_spec((1, BRANCH_WIDTH)),
        _const_spec((1, BRANCH_WIDTH)),
        _const_spec((B_CHUNK, BRANCH_WIDTH)),
        _const_spec((1, C_Q_RANK)),
        _const_spec((1, C_KV_RANK)),
        tab, tab, tab,
    ]
    out_shape = [
        jax.ShapeDtypeStruct((n, 512), BF16),
        jax.ShapeDtypeStruct((n, 512), BF16),
        jax.ShapeDtypeStruct((n, 512), BF16),
        jax.ShapeDtypeStruct((512, n), BF16),
        jax.ShapeDtypeStruct((n, LANES), BF16),
        jax.ShapeDtypeStruct((IDX_HEADS, n), F32),
        jax.ShapeDtypeStruct((n, 512), BF16),
        jax.ShapeDtypeStruct((n, C_HEADS * LANES), BF16),
        jax.ShapeDtypeStruct((n, C_HEADS * LANES), BF16),
        jax.ShapeDtypeStruct((C_HEADS * C_V_DIM, n), BF16),
    ]
    out_specs = [row(512), row(512), row(512), col(512), row(LANES), col(IDX_HEADS), row(512),
                 row(C_HEADS * LANES), row(C_HEADS * LANES), col(C_HEADS * C_V_DIM)]
    return pl.pallas_call(
        _proj_kernel,
        out_shape=out_shape,
        grid=(n // tm,),
        in_specs=in_specs,
        out_specs=out_specs,
        compiler_params=pltpu.CompilerParams(dimension_semantics=("arbitrary",),
                                             vmem_limit_bytes=VMEM_LIMIT),
        name="proj",
    )(x2, wts["wp"], wts["wuq"], wts["wke"], wts["wv"], wts["ws"], wts["gmg"], wts["gmb"],
      wts["bsf"], wts["qn"], wts["kvn"], tabs["t64"], tabs["t128"], tabs["ts2"])


def _rep8(x):
    return jnp.broadcast_to(x, (8, x.shape[-1]))


def _fold8(x3, op):
    return _rep8(op(op(x3, axis=0), axis=0, keepdims=True))


def _split8(x):
    return x.reshape(x.shape[0] // 8, 8, x.shape[1])


def _flash_init(m_ref, l_ref, acc_ref):
    m_ref[...] = jnp.full(m_ref.shape, NEG, F32)
    l_ref[...] = jnp.zeros(l_ref.shape, F32)
    acc_ref[...] = jnp.zeros(acc_ref.shape, F32)


def _flash_update(s3, vt, h, m_ref, l_ref, acc_ref):
    n, _, tq = s3.shape
    m_old = m_ref[h]
    m_new = jnp.maximum(m_old, _fold8(s3, jnp.max))
    alpha = jnp.exp2(m_old - m_new)
    p = jnp.exp2(s3 - m_new[None])
    m_ref[h] = m_new
    l_ref[h] = alpha * l_ref[h] + _fold8(p, jnp.sum)
    pv = _dot(vt, p.reshape(n * 8, tq).astype(BF16))
    acc_ref[h] = (_split8(acc_ref[h]) * alpha[None]).reshape(pv.shape) + pv


def _flash_chunk(logits, values, heads, m_ref, l_ref, acc_ref):
    s = {h: logits(h) for h in range(min(FLASH_LOOKAHEAD, heads))}
    for h in range(heads):
        if h + FLASH_LOOKAHEAD < heads:
            s[h + FLASH_LOOKAHEAD] = logits(h + FLASH_LOOKAHEAD)
        _flash_update(s.pop(h), values(h), h, m_ref, l_ref, acc_ref)


def _flash_finish(o_ref, l_ref, acc_ref, heads):
    outs = [(_split8(acc_ref[h]) / l_ref[h][None]).reshape(acc_ref.shape[1:]) for h in range(heads)]
    o_ref[...] = jnp.concatenate(outs, axis=0).T.astype(BF16)


def _dsa_kernel(iq_ref, iwt_ref, aq_ref, ik2_ref, ak_ref, avt_ref, o_ref,
                key_ref, iqm_ref, aqm_ref, thr_ref, m_ref, l_ref, acc_ref, *, n_sel):
    t = iq_ref.shape[0]
    i = pl.program_id(1)
    nchunk = i + 1
    lane = lax.broadcasted_iota(I32, (t, LANES), 1)
    lo = lane < IDX_HEAD_DIM
    krow = lax.broadcasted_iota(I32, (t, t), 0)
    qcol = lax.broadcasted_iota(I32, (t, t), 1)

    for h in range(IDX_HEADS):
        cols = slice((h // 2) * LANES, (h // 2 + 1) * LANES)
        keep = lo if h % 2 == 0 else jnp.logical_not(lo)
        iqm_ref[h] = jnp.where(keep, iq_ref[:, cols], jnp.zeros((t, LANES), BF16))
        aqm_ref[h] = jnp.where(keep, aq_ref[:, cols], jnp.zeros((t, LANES), BF16))

    def rows(c):
        return pl.ds(pl.multiple_of(c * t, t), t)

    def causal_at(c):
        return (krow + c * t) <= (qcol + i * t)

    def idx_body(c, carry):
        kc = ik2_ref[rows(c), :]
        acc = jnp.zeros((t // 8, 8, t), F32)
        for h in range(IDX_HEADS):
            w8 = _rep8(iwt_ref[h:h + 1, :])
            acc = acc + _split8(jnp.maximum(_dot_nt(kc, iqm_ref[h]), 0.0)) * w8[None]
        bits = lax.bitcast_convert_type(acc.reshape(t, t), I32)
        key = bits ^ ((bits >> 31) & 0x7FFFFFFF)
        key_ref[rows(c), :] = jnp.where(causal_at(c), key, INT_MIN)
        return carry

    lax.fori_loop(0, nchunk, idx_body, 0)

    def count(indicator):
        def body(c, acc):
            return acc + jnp.sum(indicator(_split8(key_ref[rows(c), :])), axis=0)
        part = lax.fori_loop(0, nchunk, body, jnp.zeros((8, t), I32))
        return _rep8(jnp.sum(part, axis=0, keepdims=True))

    def bit_body(it, prefix):
        trial_u = prefix | jnp.left_shift(jnp.int32(1), 31 - it)
        trial_s = trial_u ^ INT_MIN
        cnt = count(lambda kc: jnp.where(kc >= trial_s[None], 1, 0))
        return jnp.where(cnt >= n_sel, trial_u, prefix)

    thr = lax.fori_loop(0, 32, bit_body, jnp.zeros((8, t), I32)) ^ INT_MIN
    cnt_ge = count(lambda kc: jnp.where(kc >= thr[None], 1, 0))
    tied = jnp.where(cnt_ge > n_sel, jnp.where(thr != INT_MIN, 1, 0), 0)

    @pl.when(jnp.max(tied) > 0)
    def _():
        need = (n_sel - count(lambda kc: jnp.where(kc > thr[None], 1, 0))).astype(F32)
        lower = jnp.where(qcol < krow, 1.0, 0.0).astype(BF16)

        def body(c, seen):
            kc = _split8(key_ref[rows(c), :])
            eq = jnp.where(kc == thr[None], 1.0, 0.0)
            rank = _split8(_dot(lower, eq.reshape(t, t).astype(BF16))) + seen[None]
            drop = jnp.where(rank >= need[None], eq, 0.0)
            key_ref[rows(c), :] = jnp.where(drop > 0.0, INT_MIN, kc).reshape(t, t)
            return seen + _fold8(eq, jnp.sum)

        lax.fori_loop(0, nchunk, body, jnp.zeros((8, t), F32))

    thr_ref[...] = jnp.maximum(thr, INT_MIN + 1)

    _flash_init(m_ref, l_ref, acc_ref)

    def att_body(c, carry):
        sel = _split8(key_ref[rows(c), :]) >= thr_ref[...][None]

        def logits(h):
            cols = slice((h // 2) * LANES, (h // 2 + 1) * LANES)
            return jnp.where(sel, _split8(_dot_nt(ak_ref[rows(c), cols], aqm_ref[h])), NEG)

        def values(h):
            return avt_ref[h * A_HEAD_DIM:(h + 1) * A_HEAD_DIM, rows(c)]

        _flash_chunk(logits, values, A_HEADS, m_ref, l_ref, acc_ref)
        return carry

    lax.fori_loop(0, nchunk, att_body, 0)
    _flash_finish(o_ref, l_ref, acc_ref, A_HEADS)


def _dsa(iq, iwt, aq, ik2, ak, avt, batch, seq):
    t = min(ATT_T, seq)
    nq = seq // t
    n_sel = min(IDX_TOPK_MAX, seq // 4)
    assert seq % t == 0 and t % LANES == 0
    qspec = lambda w: pl.BlockSpec((t, w), lambda b, i: (b * nq + i, 0))
    kspec = lambda w: pl.BlockSpec((seq, w), lambda b, i: (b, 0))
    return pl.pallas_call(
        functools.partial(_dsa_kernel, n_sel=n_sel),
        out_shape=jax.ShapeDtypeStruct((batch * seq, 512), BF16),
        grid=(batch, nq),
        in_specs=[qspec(512),
                  pl.BlockSpec((IDX_HEADS, t), lambda b, i: (0, b * nq + i)),
                  qspec(512), kspec(LANES), kspec(512),
                  pl.BlockSpec((512, seq), lambda b, i: (0, b))],
        out_specs=qspec(512),
        scratch_shapes=[
            pltpu.VMEM((seq, t), I32),
            pltpu.VMEM((IDX_HEADS, t, LANES), BF16),
            pltpu.VMEM((A_HEADS, t, LANES), BF16),
            pltpu.VMEM((8, t), I32),
            pltpu.VMEM((A_HEADS, 8, t), F32),
            pltpu.VMEM((A_HEADS, 8, t), F32),
            pltpu.VMEM((A_HEADS, A_HEAD_DIM, t), F32),
        ],
        compiler_params=pltpu.CompilerParams(dimension_semantics=("arbitrary", "arbitrary"),
                                             vmem_limit_bytes=VMEM_LIMIT),
        name="dsa",
    )(iq, iwt, aq, ik2, ak, avt)


def _mla_kernel(q_ref, k_ref, vt_ref, o_ref, m_ref, l_ref, acc_ref):
    t = q_ref.shape[0]
    i = pl.program_id(1)
    krow = lax.broadcasted_iota(I32, (t // 8, 8, t), 0) * 8 + lax.broadcasted_iota(I32, (t // 8, 8, t), 1)
    qcol = lax.broadcasted_iota(I32, (t // 8, 8, t), 2)
    _flash_init(m_ref, l_ref, acc_ref)

    def chunk(c, masked):
        rows = pl.ds(pl.multiple_of(c * t, t), t)

        def logits(h):
            cols = slice(h * LANES, (h + 1) * LANES)
            s3 = _split8(_dot_nt(k_ref[rows, cols], q_ref[:, cols]))
            return jnp.where(krow <= qcol, s3, NEG) if masked else s3

        def values(h):
            return vt_ref[h * C_V_DIM:(h + 1) * C_V_DIM, rows]

        _flash_chunk(logits, values, C_HEADS, m_ref, l_ref, acc_ref)

    def body(c, carry):
        chunk(c, False)
        return carry

    lax.fori_loop(0, i, body, 0)
    chunk(i, True)
    _flash_finish(o_ref, l_ref, acc_ref, C_HEADS)


def _mla(q, k, vt, batch, seq):
    t = min(ATT_T, seq)
    nq = seq // t
    return pl.pallas_call(
        _mla_kernel,
        out_shape=jax.ShapeDtypeStruct((batch * seq, C_HEADS * C_V_DIM), BF16),
        grid=(batch, nq),
        in_specs=[pl.BlockSpec((t, C_HEADS * LANES), lambda b, i: (b * nq + i, 0)),
                  pl.BlockSpec((seq, C_HEADS * LANES), lambda b, i: (b, 0)),
                  pl.BlockSpec((C_HEADS * C_V_DIM, seq), lambda b, i: (0, b))],
        out_specs=pl.BlockSpec((t, C_HEADS * C_V_DIM), lambda b, i: (b * nq + i, 0)),
        scratch_shapes=[pltpu.VMEM((C_HEADS, 8, t), F32),
                        pltpu.VMEM((C_HEADS, 8, t), F32),
                        pltpu.VMEM((C_HEADS, C_V_DIM, t), F32)],
        compiler_params=pltpu.CompilerParams(dimension_semantics=("arbitrary", "arbitrary"),
                                             vmem_limit_bytes=VMEM_LIMIT),
        name="mla",
    )(q, k, vt)


def _slab_to_rows(x):
    t = x.shape[0]
    y = jnp.swapaxes(x.reshape(t // 8, 8, 8, LANES), 1, 2)
    return jnp.concatenate([y[:, c].reshape(t, LANES) for c in range(8)], axis=1)


def _rows_to_slab(v):
    t = v.shape[0]
    y = jnp.stack([v[:, c * LANES:(c + 1) * LANES].reshape(t // 8, 8, LANES) for c in range(8)], axis=1)
    return jnp.swapaxes(y, 1, 2).reshape(t, 8, LANES)


def _issue_rows(n_tok, copy):
    def body(g, carry):
        for u in range(8):
            for j in range(TOP_K):
                copy(g * 8 + u, j).start(priority=(u * TOP_K + j) % 2)
        return carry
    lax.fori_loop(0, n_tok // 8, body, 0)


def _merge_kernel(x_ref, oa_ref, ob_ref, oc_ref, wg_ref, wbr_ref, wo_ref, g1_ref, b1_ref,
                  rw_ref, rb_ref, x1_o, x1r_o, ids_o, rk_o, gt_o, cnt_o, run_ref, *, alpha):
    tm = x_ref.shape[0]

    @pl.when(pl.program_id(0) == 0)
    def _():
        run_ref[...] = jnp.zeros_like(run_ref)

    x = x_ref[...]
    xb = x.astype(BF16)
    merged = jnp.zeros((tm, D_MODEL), F32)
    for n, o_ref in enumerate((oa_ref, ob_ref, oc_ref)):
        z = _dot(xb, wg_ref[:, n * D_MODEL:(n + 1) * D_MODEL])
        merged = merged + (1.0 / (1.0 + jnp.exp(-z))) * _dot(o_ref[...], wbr_ref[n])
    y = _dot(merged.astype(BF16), wo_ref[...])
    x1 = _layer_norm(alpha * x + y, g1_ref[...], b1_ref[...])
    x1_o[...] = x1
    x1r_o[...] = _rows_to_slab(x1)

    lane = lax.broadcasted_iota(I32, (tm, LANES), 1)
    logits = jnp.dot(x1, rw_ref[...], precision=lax.Precision.HIGHEST,
                     preferred_element_type=F32) + rb_ref[...]
    lg = jnp.where(lane < N_EXPERTS, logits, -jnp.inf)
    ids, vals = [], []
    for _ in range(TOP_K):
        mx = jnp.max(lg, axis=1, keepdims=True)
        idx = jnp.min(jnp.where(lg == mx, lane, LANES), axis=1, keepdims=True)
        ids.append(idx)
        vals.append(mx)
        lg = jnp.where(lane == idx, -jnp.inf, lg)
    es = [jnp.exp(v - vals[0]) for v in vals]
    den = es[0] + es[1] + es[2] + es[3]

    hot = jnp.zeros((tm, LANES), F32)
    for idx in ids:
        hot = hot + jnp.where(lane == idx, 1.0, 0.0)
    r2 = lax.broadcasted_iota(I32, (tm, tm), 0)
    c2 = lax.broadcasted_iota(I32, (tm, tm), 1)
    lower = jnp.where(c2 < r2, 1.0, 0.0).astype(BF16)
    base = run_ref[0:1, :] + _dot(lower, hot.astype(BF16))
    ids_v = jnp.zeros((tm, LANES), I32)
    rk_v = jnp.zeros((tm, LANES), I32)
    gt_v = jnp.zeros((tm, LANES), F32)
    for j in range(TOP_K):
        rank = jnp.sum(jnp.where(lane == ids[j], base, 0.0), axis=1, keepdims=True)
        ids_v = jnp.where(lane == j, ids[j], ids_v)
        rk_v = jnp.where(lane == j, rank.astype(I32), rk_v)
        gt_v = jnp.where(lane == j, es[j] / den, gt_v)
    ids_o[...] = ids_v
    rk_o[...] = rk_v
    gt_o[...] = gt_v
    run = run_ref[0:1, :] + jnp.sum(hot, axis=0, keepdims=True)
    run_ref[...] = jnp.broadcast_to(run, run_ref.shape)
    cnt_o[...] = jnp.broadcast_to(run, cnt_o.shape)


def _merge(x2, oa, ob, oc, wts, alpha):
    n = x2.shape[0]
    tm = min(MERGE_TM, n)
    assert n % tm == 0
    row = lambda w: pl.BlockSpec((tm, w), lambda i: (i, 0))
    in_specs = [row(D_MODEL), row(512), row(512), row(512),
                _const_spec((D_MODEL, N_BRANCH * D_MODEL)),
                _const_spec((N_BRANCH, BRANCH_WIDTH, D_MODEL)),
                _const_spec((D_MODEL, D_MODEL)),
                _const_spec((1, D_MODEL)), _const_spec((1, D_MODEL)),
                _const_spec((D_MODEL, LANES)), _const_spec((1, LANES))]
    out_shape = [jax.ShapeDtypeStruct((n, D_MODEL), F32),
                 jax.ShapeDtypeStruct((n, 8, LANES), F32),
                 jax.ShapeDtypeStruct((n, LANES), I32),
                 jax.ShapeDtypeStruct((n, LANES), I32),
                 jax.ShapeDtypeStruct((n, LANES), F32),
                 jax.ShapeDtypeStruct((8, LANES), F32)]
    out_specs = [row(D_MODEL), pl.BlockSpec((tm, 8, LANES), lambda i: (i, 0, 0)),
                 row(LANES), row(LANES), row(LANES), _const_spec((8, LANES))]
    return pl.pallas_call(
        functools.partial(_merge_kernel, alpha=alpha),
        out_shape=out_shape,
        grid=(n // tm,),
        in_specs=in_specs,
        out_specs=out_specs,
        scratch_shapes=[pltpu.VMEM((8, LANES), F32)],
        compiler_params=pltpu.CompilerParams(dimension_semantics=("arbitrary",),
                                             vmem_limit_bytes=VMEM_LIMIT),
        name="merge",
    )(x2, oa, ob, oc, wts["wg"], wts["wbr"], wts["wo"], wts["g1"], wts["b1"], wts["rw"], wts["rb"])


def _scatter_kernel(zs_ref, pos_ref, x_ref, xs_ref, zbuf, sem):
    tm = x_ref.shape[0]
    tg = zbuf.shape[0]

    @pl.when(pl.program_id(0) == 0)
    def _():
        zbuf[...] = jnp.zeros(zbuf.shape, F32)
        for e in range(N_EXPERTS):
            cp = pltpu.make_async_copy(zbuf, xs_ref.at[pl.ds(zs_ref[e], tg)], sem)
            cp.start()
            cp.wait()
        n_tiles = xs_ref.shape[0] // tg
        for k in range(N_EXPERTS):
            @pl.when(n_tiles - 1 - k >= zs_ref[N_EXPERTS])
            def _():
                cp = pltpu.make_async_copy(zbuf, xs_ref.at[pl.ds((n_tiles - 1 - k) * tg, tg)], sem)
                cp.start()
                cp.wait()

    _issue_rows(tm, lambda tok, j: pltpu.make_async_copy(
        x_ref.at[tok], xs_ref.at[pos_ref[tok * TOP_K + j]], sem))
    for j in range(TOP_K):
        pltpu.make_async_copy(x_ref, xs_ref.at[pl.ds(0, tm)], sem).wait()


def _scatter_rows(zstart, pos_flat, x1r, n_rows):
    n = x1r.shape[0]
    tm = min(DISPATCH_TM, n)
    assert n % tm == 0
    grid_spec = pltpu.PrefetchScalarGridSpec(
        num_scalar_prefetch=1,
        grid=(n // tm,),
        in_specs=[pl.BlockSpec((tm * TOP_K,), lambda i, zs: (i,), memory_space=pltpu.SMEM),
                  pl.BlockSpec((tm, 8, LANES), lambda i, zs: (i, 0, 0))],
        out_specs=pl.BlockSpec(memory_space=pl.ANY),
        scratch_shapes=[pltpu.VMEM((EXPERT_TG, 8, LANES), F32), pltpu.SemaphoreType.DMA],
    )
    return pl.pallas_call(
        _scatter_kernel,
        out_shape=jax.ShapeDtypeStruct((n_rows, 8, LANES), F32),
        grid_spec=grid_spec,
        compiler_params=pltpu.CompilerParams(dimension_semantics=("arbitrary",),
                                             vmem_limit_bytes=VMEM_LIMIT),
        name="dispatch",
    )(zstart, pos_flat, x1r)


def _ffn_kernel(te_ref, nu_ref, xs_ref, wgu_ref, bgu_ref, wdn_ref, bdn_ref, y_ref, wgu_b, wdn_b):
    t = pl.program_id(0)
    valid = t < nu_ref[0]
    fresh = jnp.logical_or(t == 0, te_ref[jnp.maximum(t - 1, 0)] != te_ref[t])

    @pl.when(jnp.logical_and(valid, fresh))
    def _():
        for r in range(0, D_MODEL, LANES):
            wgu_b[r:r + LANES, :] = wgu_ref[r:r + LANES, :].astype(BF16)
        for r in range(0, D_FF, LANES):
            wdn_b[r:r + LANES, :] = wdn_ref[r:r + LANES, :].astype(BF16)

    @pl.when(valid)
    def _():
        h = _dot(_slab_to_rows(xs_ref[...]).astype(BF16), wgu_b[...]) + bgu_ref[...]
        glu = jnp.minimum(h[:, :D_FF], SWIGLU_LIMIT)
        lin = jnp.clip(h[:, D_FF:], -SWIGLU_LIMIT, SWIGLU_LIMIT)
        act = (lin + 1.0) * (glu * (1.0 / (1.0 + jnp.exp(-SWIGLU_ALPHA * glu))))
        y_ref[...] = _rows_to_slab(_dot(act.astype(BF16), wdn_b[...]) + bdn_ref[...])

    @pl.when(jnp.logical_not(valid))
    def _():
        y_ref[...] = jnp.zeros(y_ref.shape, F32)


def _ffn(tile_e, n_used, xs, wgu, bgu, wdn, bdn, layer):
    tg = EXPERT_TG
    n_tiles = xs.shape[0] // tg
    wspec = lambda r, c: pl.BlockSpec((None, None, r, c), lambda t, te, nu: (layer, te[t], 0, 0))
    slab = pl.BlockSpec((tg, 8, LANES), lambda t, te, nu: (t, 0, 0))
    grid_spec = pltpu.PrefetchScalarGridSpec(
        num_scalar_prefetch=2,
        grid=(n_tiles,),
        in_specs=[slab, wspec(D_MODEL, 2 * D_FF), wspec(1, 2 * D_FF), wspec(D_FF, D_MODEL),
                  wspec(1, D_MODEL)],
        out_specs=slab,
        scratch_shapes=[pltpu.VMEM((D_MODEL, 2 * D_FF), BF16), pltpu.VMEM((D_FF, D_MODEL), BF16)],
    )
    return pl.pallas_call(
        _ffn_kernel,
        out_shape=jax.ShapeDtypeStruct(xs.shape, F32),
        grid_spec=grid_spec,
        compiler_params=pltpu.CompilerParams(dimension_semantics=("arbitrary",),
                                             vmem_limit_bytes=VMEM_LIMIT),
        name="experts",
    )(tile_e, n_used, xs, wgu, bgu, wdn, bdn)


def _gather_kernel(posc_ref, posn_ref, x1_ref, gt_ref, g2_ref, b2_ref, y_ref, o_ref, buf, sem, *, alpha):
    tm = x1_ref.shape[0]
    i = pl.program_id(0)
    slot = lax.rem(i, 2)

    def issue(pos_ref, sl):
        _issue_rows(tm, lambda tok, j: pltpu.make_async_copy(
            y_ref.at[pos_ref[tok * TOP_K + j]], buf.at[sl, j, tok], sem.at[sl]))

    @pl.when(i == 0)
    def _():
        issue(posc_ref, 0)

    @pl.when(i + 1 < pl.num_programs(0))
    def _():
        issue(posn_ref, 1 - slot)

    for j in range(TOP_K):
        pltpu.make_async_copy(y_ref.at[pl.ds(0, tm)], buf.at[slot, j], sem.at[slot]).wait()

    f = jnp.zeros((tm, D_MODEL), F32)
    for j in range(TOP_K):
        f = f + gt_ref[:, j:j + 1] * _slab_to_rows(buf[slot, j])
    o_ref[...] = _layer_norm(alpha * x1_ref[...] + f, g2_ref[...], b2_ref[...])


def _gather_rows(pos_flat, x1, gates, g2, b2, y, alpha):
    n = x1.shape[0]
    tm = min(COMBINE_TM, n)
    assert n % tm == 0
    nblk = n // tm
    smem = lambda f: pl.BlockSpec((tm * TOP_K,), lambda i: (f(i),), memory_space=pltpu.SMEM)
    return pl.pallas_call(
        functools.partial(_gather_kernel, alpha=alpha),
        out_shape=jax.ShapeDtypeStruct((n, D_MODEL), F32),
        grid=(nblk,),
        in_specs=[smem(lambda i: i), smem(lambda i: jnp.minimum(i + 1, nblk - 1)),
                  pl.BlockSpec((tm, D_MODEL), lambda i: (i, 0)),
                  pl.BlockSpec((tm, LANES), lambda i: (i, 0)),
                  _const_spec((1, D_MODEL)), _const_spec((1, D_MODEL)),
                  pl.BlockSpec(memory_space=pl.ANY)],
        out_specs=pl.BlockSpec((tm, D_MODEL), lambda i: (i, 0)),
        scratch_shapes=[pltpu.VMEM((2, TOP_K, tm, 8, LANES), F32), pltpu.SemaphoreType.DMA((2,))],
        compiler_params=pltpu.CompilerParams(dimension_semantics=("arbitrary",),
                                             vmem_limit_bytes=VMEM_LIMIT),
        name="combine",
    )(pos_flat, pos_flat, x1, gates, g2, b2, y)


def _rope_tables(positions):
    pos = positions.reshape(-1).astype(F32)[:, None]
    n = pos.shape[0]

    def cs(rot_dim):
        half = rot_dim // 2
        inv_freq = ROPE_THETA ** (-jnp.arange(half, dtype=F32) / half)
        ang = pos * inv_freq
        return jnp.cos(ang), jnp.sin(ang)

    cos_p, sin_p = cs(A_ROT_DIM)
    cos_c, sin_c = cs(C_ROPE_DIM)
    hp, hc = A_ROT_DIM // 2, C_ROPE_DIM // 2
    one = lambda w: jnp.ones((n, w), F32)
    zero = lambda w: jnp.zeros((n, w), F32)

    a64 = jnp.concatenate([cos_p, cos_p, one(64 - 2 * hp)] * 2, axis=1)
    bm64 = jnp.concatenate([-sin_p, zero(64 - hp)] * 2, axis=1)
    bp64 = jnp.concatenate([zero(hp), sin_p, zero(64 - 2 * hp)] * 2, axis=1)
    a128 = jnp.concatenate([one(64), cos_c, cos_c, one(32)], axis=1)
    bm128 = jnp.concatenate([zero(64), -sin_c, zero(64 - hc)], axis=1)
    bp128 = jnp.concatenate([zero(64 + hc), sin_c, zero(32)], axis=1)
    iw_scale = jnp.full((n, IDX_HEADS), IDX_HEADS ** -0.5, F32)
    as2 = jnp.concatenate([iw_scale, zero(64 - IDX_HEADS), cos_c, cos_c, zero(32)], axis=1)
    return {"t64": jnp.stack([a64, bm64, bp64]),
            "t128": jnp.stack([a128, bm128, bp128]),
            "ts2": jnp.stack([as2, bm128, bp128])}


def _layer_weights(l, w_in, gm_norm_g, gm_norm_b, gm_w_s, gm_b_s, mla_q_norm, mla_kv_norm,
                   mla_w_uq, mla_w_ukv, w_branch, w_out, ln1_g, ln1_b, router_w, router_b):
    w = w_in[l]
    o = 0
    cols = {}
    for name, width in (("aq", 512), ("ak", 512), ("av", 512), ("iq", 512), ("ik", 64), ("iw", 8),
                        ("buv", 1024), ("cdq", 256), ("cdkv", 128), ("ckr", 32), ("gate", 3072)):
        cols[name] = w[:, o:o + width]
        o += width
    zc = lambda width: jnp.zeros((D_MODEL, width), F32)
    s2 = jnp.concatenate([cols["iw"], zc(64 - IDX_HEADS), cols["ckr"], zc(32)], axis=1)
    wp = jnp.concatenate([cols["aq"] * (A_HEAD_DIM ** -0.5 * LOG2E), cols["ak"],
                          cols["iq"] * (IDX_HEAD_DIM ** -0.5), cols["av"],
                          cols["ik"], cols["ik"], s2, cols["buv"], cols["cdq"], cols["cdkv"]], axis=1)

    uq = mla_w_uq[l].reshape(C_Q_RANK, C_HEADS, C_NOPE_DIM + C_ROPE_DIM)
    wuq = jnp.pad(uq, ((0, 0), (0, 0), (0, LANES - C_NOPE_DIM - C_ROPE_DIM))).reshape(C_Q_RANK, -1)
    wuq = wuq * ((C_NOPE_DIM + C_ROPE_DIM) ** -0.5 * LOG2E)
    ukv = mla_w_ukv[l].reshape(C_KV_RANK, C_HEADS, C_NOPE_DIM + C_V_DIM)
    wk = jnp.pad(ukv[:, :, :C_NOPE_DIM], ((0, 0), (0, 0), (0, LANES - C_NOPE_DIM))).reshape(C_KV_RANK, -1)
    wv = ukv[:, :, C_NOPE_DIM:].reshape(C_KV_RANK, -1)
    src = jnp.arange(LANES)[:, None]
    dst = jnp.arange(C_HEADS * LANES)[None, :]
    place = ((dst % LANES == src) & (src >= C_NOPE_DIM) & (src < C_NOPE_DIM + C_ROPE_DIM)).astype(F32)
    wke = jnp.concatenate([wk, place], axis=0)

    tril = jnp.tril(jnp.ones((B_CHUNK, B_CHUNK), dtype=bool))
    ws = jnp.where(tril[None], gm_w_s[l], 0)
    bsf = jnp.repeat(gm_b_s[l].T, B_GROUP_DIM, axis=1)
    rw = jnp.pad(router_w[l], ((0, 0), (0, LANES - N_EXPERTS)))
    rb = jnp.pad(router_b[l], (0, LANES - N_EXPERTS))[None, :]
    return {
        "wp": wp.astype(BF16), "wuq": wuq.astype(BF16), "wke": wke.astype(BF16), "wv": wv.astype(BF16),
        "ws": ws.astype(BF16), "gmg": gm_norm_g[l][None, :], "gmb": gm_norm_b[l][None, :], "bsf": bsf,
        "qn": mla_q_norm[l][None, :], "kvn": mla_kv_norm[l][None, :],
        "wg": cols["gate"].astype(BF16), "wbr": w_branch[l].astype(BF16), "wo": w_out[l].astype(BF16),
        "g1": ln1_g[l][None, :], "b1": ln1_b[l][None, :], "rw": rw, "rb": rb,
    }


def _expert_tiles(counts, n_tiles):
    tg = EXPERT_TG
    cnt = counts[0, :N_EXPERTS].astype(I32)
    tiles_per = (cnt + tg - 1) // tg
    tile_end = jnp.cumsum(tiles_per)
    offsets = (tile_end - tiles_per) * tg
    n_used = tile_end[-1]
    tid = jnp.minimum(jnp.arange(n_tiles, dtype=I32), n_used - 1)
    tile_e = jnp.sum(tid[:, None] >= tile_end[None, :], axis=1).astype(I32)
    off128 = jnp.pad(offsets, (0, LANES - N_EXPERTS)).astype(I32)
    return off128, tile_e, n_used.reshape(1).astype(I32)


def _sorted_positions(offsets, counts, n_used, ids, ranks, n_rows):
    e = ids[:, :TOP_K]
    onehot = e[..., None] == jnp.arange(N_EXPERTS, dtype=I32)
    pos = ranks[:, :TOP_K] + jnp.sum(jnp.where(onehot, offsets[:N_EXPERTS], 0), axis=-1)
    cnt = counts[0, :N_EXPERTS].astype(I32)
    zstart = jnp.minimum(offsets[:N_EXPERTS] + cnt, n_rows - EXPERT_TG)
    return pos.reshape(-1).astype(I32), jnp.concatenate([zstart.astype(I32), n_used])


def _layer(l, x2, tabs, batch, seq, depth, p):
    alpha = (2 * depth) ** 0.25
    n = x2.shape[0]
    wts = _layer_weights(l, p["w_in"], p["gm_norm_g"], p["gm_norm_b"], p["gm_w_s"], p["gm_b_s"],
                         p["mla_q_norm"], p["mla_kv_norm"], p["mla_w_uq"], p["mla_w_ukv"],
                         p["w_branch"], p["w_out"], p["ln1_g"], p["ln1_b"], p["router_w"], p["router_b"])
    aq, ak, iq, avt, ik2, iwt, o_b, q, k, vt = _proj(x2, wts, tabs)
    o_a = _dsa(iq, iwt, aq, ik2, ak, avt, batch, seq)
    o_c = _mla(q, k, vt, batch, seq)
    x1, x1r, ids, ranks, gates, counts = _merge(x2, o_a, o_b, o_c, wts, alpha)

    n_tiles = n * TOP_K // EXPERT_TG + N_EXPERTS
    offsets, tile_e, n_used = _expert_tiles(counts, n_tiles)
    n_rows = n_tiles * EXPERT_TG
    pos_flat, zstart = _sorted_positions(offsets, counts, n_used, ids, ranks, n_rows)
    xs = _scatter_rows(zstart, pos_flat, x1r, n_rows)
    y = _ffn(tile_e, n_used, xs, p["exp_w_gu"], p["exp_b_gu"][:, :, None, :],
             p["exp_w_dn"], p["exp_b_dn"][:, :, None, :], l)
    return _gather_rows(pos_flat, x1, gates, p["ln2_g"][l][None, :], p["ln2_b"][l][None, :], y, alpha)


def kernel(x, positions, w_in, gm_norm_g, gm_norm_b, gm_w_s, gm_b_s, mla_q_norm, mla_kv_norm, mla_w_uq, mla_w_ukv, w_branch, w_out, ln1_g, ln1_b, router_w, router_b, exp_w_gu, exp_b_gu, exp_w_dn, exp_b_dn, ln2_g, ln2_b):
    batch, seq, _ = x.shape
    depth = w_in.shape[0]
    p = dict(w_in=w_in, gm_norm_g=gm_norm_g, gm_norm_b=gm_norm_b, gm_w_s=gm_w_s, gm_b_s=gm_b_s,
             mla_q_norm=mla_q_norm, mla_kv_norm=mla_kv_norm, mla_w_uq=mla_w_uq, mla_w_ukv=mla_w_ukv,
             w_branch=w_branch, w_out=w_out, ln1_g=ln1_g, ln1_b=ln1_b, router_w=router_w,
             router_b=router_b, exp_w_gu=exp_w_gu, exp_b_gu=exp_b_gu, exp_w_dn=exp_w_dn,
             exp_b_dn=exp_b_dn, ln2_g=ln2_g, ln2_b=ln2_b)
    tabs = _rope_tables(positions)
    x2 = x.reshape(batch * seq, D_MODEL)
    for l in range(depth):
        x2 = _layer(l, x2, tabs, batch, seq, depth, p)
    return x2.reshape(batch, seq, D_MODEL)
```

```python
import functools
import math

import jax
import jax.numpy as jnp
from jax import lax
from jax.experimental import pallas as pl
from jax.experimental.pallas import tpu as pltpu

F32 = jnp.float32
BF16 = jnp.bfloat16
I32 = jnp.int32

D_MODEL = 1024
BRANCH_WIDTH = 512
N_BRANCH = 3
ROPE_THETA = 500000.0
LN_EPS = 1e-5
RMS_EPS = 1e-6

A_HEADS = 8
A_HEAD_DIM = 64
A_ROT_DIM = 16
IDX_HEADS = 8
IDX_HEAD_DIM = 64
IDX_TOPK_MAX = 256

B_GROUPS = 8
B_GROUP_DIM = 64
B_CHUNK = 128

C_HEADS = 8
C_NOPE_DIM = 64
C_ROPE_DIM = 32
C_V_DIM = 64
C_Q_RANK = 256
C_KV_RANK = 128

N_EXPERTS = 32
TOP_K = 4
D_FF = 1024
SWIGLU_LIMIT = 7.0
SWIGLU_ALPHA = 1.702

LANES = 128
ATT_T = 512
PROJ_TM = 512
MERGE_TM = 512
DISPATCH_TM = 256
COMBINE_TM = 256
EXPERT_TG = 256
VMEM_LIMIT = 56 * 1024 * 1024

FLASH_LOOKAHEAD = 2
NEG = -1e30
LOG2E = math.log2(math.e)
INT_MIN = -(2 ** 31)

_C_AQ, _C_AK, _C_IQ, _C_AV = 0, 512, 1024, 1536
_C_S1, _C_S2, _C_B, _C_C, _C_END = 2048, 2176, 2304, 3328, 3712


def _dot(a, b):
    return jnp.dot(a, b, preferred_element_type=F32)


def _dot_nt(a, b):
    return lax.dot_general(a, b, (((1,), (1,)), ((), ())), preferred_element_type=F32)


def _rot(y, a, bm, bp, r):
    w = y.shape[1]
    return y * a + pltpu.roll(y, w - r, 1) * bm + pltpu.roll(y, r, 1) * bp


def _layer_norm(x, g, b):
    mu = jnp.mean(x, axis=-1, keepdims=True)
    xc = x - mu
    var = jnp.mean(xc * xc, axis=-1, keepdims=True)
    return xc * lax.rsqrt(var + LN_EPS) * g + b


def _rms_norm(x, g):
    return x * lax.rsqrt(jnp.mean(x * x, axis=-1, keepdims=True) + RMS_EPS) * g


def _const_spec(shape):
    nd = len(shape)
    return pl.BlockSpec(shape, lambda *_: (0,) * nd)


def _proj_kernel(x_ref, wp_ref, wuq_ref, wke_ref, wv_ref, ws_ref, gmg_ref, gmb_ref, bsf_ref,
                 qn_ref, kvn_ref, t64_ref, t128_ref, ts2_ref,
                 aq_o, ak_o, iq_o, avt_o, ik2_o, iwt_o, ob_o, q_o, k_o, vt_o):
    tm = x_ref.shape[0]
    xb = x_ref[...].astype(BF16)

    def mm(c0, c1):
        return _dot(xb, wp_ref[:, c0:c1])

    a64, bm64, bp64 = t64_ref[0], t64_ref[1], t64_ref[2]
    a4 = jnp.concatenate([a64] * 4, axis=1)
    bm4 = jnp.concatenate([bm64] * 4, axis=1)
    bp4 = jnp.concatenate([bp64] * 4, axis=1)
    for c0, out in ((_C_AQ, aq_o), (_C_AK, ak_o), (_C_IQ, iq_o)):
        out[...] = _rot(mm(c0, c0 + 512), a4, bm4, bp4, A_ROT_DIM // 2).astype(BF16)
    avt_o[...] = mm(_C_AV, _C_AV + 512).T.astype(BF16)
    ik2_o[...] = _rot(mm(_C_S1, _C_S1 + LANES), a64, bm64, bp64, A_ROT_DIM // 2).astype(BF16)
    sm2 = _rot(mm(_C_S2, _C_S2 + LANES), ts2_ref[0], ts2_ref[1], ts2_ref[2], C_ROPE_DIM // 2)
    iwt_o[...] = sm2.T[:IDX_HEADS, :]

    buv = mm(_C_B, _C_B + 2 * BRANCH_WIDTH)
    buv = 0.5 * buv * (1.0 + lax.erf(buv * (1.0 / math.sqrt(2.0))))
    u = buv[:, :BRANCH_WIDTH]
    v = _layer_norm(buv[:, BRANCH_WIDTH:], gmg_ref[...], gmb_ref[...]).astype(BF16)
    lane = lax.broadcasted_iota(I32, (B_CHUNK, LANES), 1)
    for c in range(tm // B_CHUNK):
        rows = slice(c * B_CHUNK, (c + 1) * B_CHUNK)
        for pr in range(B_GROUPS // 2):
            cols = slice(pr * LANES, (pr + 1) * LANES)
            vp = v[rows, cols]
            s = jnp.where(lane < B_GROUP_DIM, _dot(ws_ref[2 * pr], vp), _dot(ws_ref[2 * pr + 1], vp))
            ob_o[rows, cols] = (u[rows, cols] * (s + bsf_ref[:, cols])).astype(BF16)

    cc = mm(_C_C, _C_END)
    cq = _rms_norm(cc[:, :C_Q_RANK], qn_ref[...]).astype(BF16)
    a8 = jnp.concatenate([t128_ref[0]] * C_HEADS, axis=1)
    bm8 = jnp.concatenate([t128_ref[1]] * C_HEADS, axis=1)
    bp8 = jnp.concatenate([t128_ref[2]] * C_HEADS, axis=1)
    q_o[...] = _rot(_dot(cq, wuq_ref[...]), a8, bm8, bp8, C_ROPE_DIM // 2).astype(BF16)
    ckv = _rms_norm(cc[:, C_Q_RANK:], kvn_ref[...]).astype(BF16)
    kin = jnp.concatenate([ckv, sm2.astype(BF16)], axis=1)
    k_o[...] = _dot(kin, wke_ref[...]).astype(BF16)
    vt_o[...] = _dot(ckv, wv_ref[...]).T.astype(BF16)


def _proj(x2, wts, tabs):
    n = x2.shape[0]
    tm = min(PROJ_TM, n)
    assert n % tm == 0 and tm % B_CHUNK == 0
    row = lambda w: pl.BlockSpec((tm, w), lambda i: (i, 0))
    col = lambda h: pl.BlockSpec((h, tm), lambda i: (0, i))
    tab = pl.BlockSpec((3, tm, LANES), lambda i: (0, i, 0))
    in_specs = [
        row(D_MODEL),
        _const_spec((D_MODEL, _C_END)),
        _const_spec((C_Q_RANK, C_HEADS * LANES)),
        _const_spec((C_KV_RANK + LANES, C_HEADS * LANES)),
        _const_spec((C_KV_RANK, C_HEADS * C_V_DIM)),
        _const_spec((B_GROUPS, B_CHUNK, B_CHUNK)),
        _const_spec((1, BRANCH_WIDTH)),
        _const_spec((1, BRANCH_WIDTH)),
        _const_spec((B_CHUNK, BRANCH_WIDTH)),
        _const_spec((1, C_Q_RANK)),
        _const_spec((1, C_KV_RANK)),
        tab, tab, tab,
    ]
    out_shape = [
        jax.ShapeDtypeStruct((n, 512), BF16),
        jax.ShapeDtypeStruct((n, 512), BF16),
        jax.ShapeDtypeStruct((n, 512), BF16),
        jax.ShapeDtypeStruct((512, n), BF16),
        jax.ShapeDtypeStruct((n, LANES), BF16),
        jax.ShapeDtypeStruct((IDX_HEADS, n), F32),
        jax.ShapeDtypeStruct((n, 512), BF16),
        jax.ShapeDtypeStruct((n, C_HEADS * LANES), BF16),
        jax.ShapeDtypeStruct((n, C_HEADS * LANES), BF16),
        jax.ShapeDtypeStruct((C_HEADS * C_V_DIM, n), BF16),
    ]
    out_specs = [row(512), row(512), row(512), col(512), row(LANES), col(IDX_HEADS), row(512),
                 row(C_HEADS * LANES), row(C_HEADS * LANES), col(C_HEADS * C_V_DIM)]
    return pl.pallas_call(
        _proj_kernel,
        out_shape=out_shape,
        grid=(n // tm,),
        in_specs=in_specs,
        out_specs=out_specs,
        compiler_params=pltpu.CompilerParams(dimension_semantics=("arbitrary",),
                                             vmem_limit_bytes=VMEM_LIMIT),
        name="proj",
    )(x2, wts["wp"], wts["wuq"], wts["wke"], wts["wv"], wts["ws"], wts["gmg"], wts["gmb"],
      wts["bsf"], wts["qn"], wts["kvn"], tabs["t64"], tabs["t128"], tabs["ts2"])


def _rep8(x):
    return jnp.broadcast_to(x, (8, x.shape[-1]))


def _fold8(x3, op):
    return _rep8(op(op(x3, axis=0), axis=0, keepdims=True))


def _split8(x):
    return x.reshape(x.shape[0] // 8, 8, x.shape[1])


def _flash_init(m_ref, l_ref, acc_ref):
    m_ref[...] = jnp.full(m_ref.shape, NEG, F32)
    l_ref[...] = jnp.zeros(l_ref.shape, F32)
    acc_ref[...] = jnp.zeros(acc_ref.shape, F32)


def _flash_update(s3, vt, h, m_ref, l_ref, acc_ref):
    n, _, tq = s3.shape
    m_old = m_ref[h]
    m_new = jnp.maximum(m_old, _fold8(s3, jnp.max))
    alpha = jnp.exp2(m_old - m_new)
    p = jnp.exp2(s3 - m_new[None])
    m_ref[h] = m_new
    l_ref[h] = alpha * l_ref[h] + _fold8(p, jnp.sum)
    pv = _dot(vt, p.reshape(n * 8, tq).astype(BF16))
    acc_ref[h] = (_split8(acc_ref[h]) * alpha[None]).reshape(pv.shape) + pv


def _flash_chunk(logits, values, heads, m_ref, l_ref, acc_ref):
    s = {h: logits(h) for h in range(min(FLASH_LOOKAHEAD, heads))}
    for h in range(heads):
        if h + FLASH_LOOKAHEAD < heads:
            s[h + FLASH_LOOKAHEAD] = logits(h + FLASH_LOOKAHEAD)
        _flash_update(s.pop(h), values(h), h, m_ref, l_ref, acc_ref)


def _flash_finish(o_ref, l_ref, acc_ref, heads):
    outs = [(_split8(acc_ref[h]) / l_ref[h][None]).reshape(acc_ref.shape[1:]) for h in range(heads)]
    o_ref[...] = jnp.concatenate(outs, axis=0).T.astype(BF16)


def _dsa_kernel(iq_ref, iwt_ref, aq_ref, ik2_ref, ak_ref, avt_ref, o_ref,
                key_ref, iqm_ref, aqm_ref, thr_ref, m_ref, l_ref, acc_ref, *, n_sel):
    t = iq_ref.shape[0]
    i = pl.program_id(1)
    nchunk = i + 1
    lane = lax.broadcasted_iota(I32, (t, LANES), 1)
    lo = lane < IDX_HEAD_DIM
    krow = lax.broadcasted_iota(I32, (t, t), 0)
    qcol = lax.broadcasted_iota(I32, (t, t), 1)

    for h in range(IDX_HEADS):
        cols = slice((h // 2) * LANES, (h // 2 + 1) * LANES)
        keep = lo if h % 2 == 0 else jnp.logical_not(lo)
        iqm_ref[h] = jnp.where(keep, iq_ref[:, cols], jnp.zeros((t, LANES), BF16))
        aqm_ref[h] = jnp.where(keep, aq_ref[:, cols], jnp.zeros((t, LANES), BF16))

    def rows(c):
        return pl.ds(pl.multiple_of(c * t, t), t)

    def causal_at(c):
        return (krow + c * t) <= (qcol + i * t)

    def idx_body(c, carry):
        kc = ik2_ref[rows(c), :]
        acc = jnp.zeros((t // 8, 8, t), F32)
        for h in range(IDX_HEADS):
            w8 = _rep8(iwt_ref[h:h + 1, :])
            acc = acc + _split8(jnp.maximum(_dot_nt(kc, iqm_ref[h]), 0.0)) * w8[None]
        bits = lax.bitcast_convert_type(acc.reshape(t, t), I32)
        key = bits ^ ((bits >> 31) & 0x7FFFFFFF)
        key_ref[rows(c), :] = jnp.where(causal_at(c), key, INT_MIN)
        return carry

    lax.fori_loop(0, nchunk, idx_body, 0)

    def count(indicator):
        def body(c, acc):
            return acc + jnp.sum(indicator(_split8(key_ref[rows(c), :])), axis=0)
        part = lax.fori_loop(0, nchunk, body, jnp.zeros((8, t), I32))
        return _rep8(jnp.sum(part, axis=0, keepdims=True))

    def bit_body(it, prefix):
        trial_u = prefix | jnp.left_shift(jnp.int32(1), 31 - it)
        trial_s = trial_u ^ INT_MIN
        cnt = count(lambda kc: jnp.where(kc >= trial_s[None], 1, 0))
        return jnp.where(cnt >= n_sel, trial_u, prefix)

    thr = lax.fori_loop(0, 32, bit_body, jnp.zeros((8, t), I32)) ^ INT_MIN
    cnt_ge = count(lambda kc: jnp.where(kc >= thr[None], 1, 0))
    tied = jnp.where(cnt_ge > n_sel, jnp.where(thr != INT_MIN, 1, 0), 0)

    @pl.when(jnp.max(tied) > 0)
    def _():
        need = (n_sel - count(lambda kc: jnp.where(kc > thr[None], 1, 0))).astype(F32)
        lower = jnp.where(qcol < krow, 1.0, 0.0).astype(BF16)

        def body(c, seen):
            kc = _split8(key_ref[rows(c), :])
            eq = jnp.where(kc == thr[None], 1.0, 0.0)
            rank = _split8(_dot(lower, eq.reshape(t, t).astype(BF16))) + seen[None]
            drop = jnp.where(rank >= need[None], eq, 0.0)
            key_ref[rows(c), :] = jnp.where(drop > 0.0, INT_MIN, kc).reshape(t, t)
            return seen + _fold8(eq, jnp.sum)

        lax.fori_loop(0, nchunk, body, jnp.zeros((8, t), F32))

    thr_ref[...] = jnp.maximum(thr, INT_MIN + 1)

    _flash_init(m_ref, l_ref, acc_ref)

    def att_body(c, carry):
        sel = _split8(key_ref[rows(c), :]) >= thr_ref[...][None]

        def logits(h):
            cols = slice((h // 2) * LANES, (h // 2 + 1) * LANES)
            return jnp.where(sel, _split8(_dot_nt(ak_ref[rows(c), cols], aqm_ref[h])), NEG)

        def values(h):
            return avt_ref[h * A_HEAD_DIM:(h + 1) * A_HEAD_DIM, rows(c)]

        _flash_chunk(logits, values, A_HEADS, m_ref, l_ref, acc_ref)
        return carry

    lax.fori_loop(0, nchunk, att_body, 0)
    _flash_finish(o_ref, l_ref, acc_ref, A_HEADS)


def _dsa(iq, iwt, aq, ik2, ak, avt, batch, seq):
    t = min(ATT_T, seq)
    nq = seq // t
    n_sel = min(IDX_TOPK_MAX, seq // 4)
    assert seq % t == 0 and t % LANES == 0
    qspec = lambda w: pl.BlockSpec((t, w), lambda b, i: (b * nq + i, 0))
    kspec = lambda w: pl.BlockSpec((seq, w), lambda b, i: (b, 0))
    return pl.pallas_call(
        functools.partial(_dsa_kernel, n_sel=n_sel),
        out_shape=jax.ShapeDtypeStruct((batch * seq, 512), BF16),
        grid=(batch, nq),
        in_specs=[qspec(512),
                  pl.BlockSpec((IDX_HEADS, t), lambda b, i: (0, b * nq + i)),
                  qspec(512), kspec(LANES), kspec(512),
                  pl.BlockSpec((512, seq), lambda b, i: (0, b))],
        out_specs=qspec(512),
        scratch_shapes=[
            pltpu.VMEM((seq, t), I32),
            pltpu.VMEM((IDX_HEADS, t, LANES), BF16),
            pltpu.VMEM((A_HEADS, t, LANES), BF16),
            pltpu.VMEM((8, t), I32),
            pltpu.VMEM((A_HEADS, 8, t), F32),
            pltpu.VMEM((A_HEADS, 8, t), F32),
            pltpu.VMEM((A_HEADS, A_HEAD_DIM, t), F32),
        ],
        compiler_params=pltpu.CompilerParams(dimension_semantics=("arbitrary", "arbitrary"),
                                             vmem_limit_bytes=VMEM_LIMIT),
        name="dsa",
    )(iq, iwt, aq, ik2, ak, avt)


def _mla_kernel(q_ref, k_ref, vt_ref, o_ref, m_ref, l_ref, acc_ref):
    t = q_ref.shape[0]
    i = pl.program_id(1)
    krow = lax.broadcasted_iota(I32, (t // 8, 8, t), 0) * 8 + lax.broadcasted_iota(I32, (t // 8, 8, t), 1)
    qcol = lax.broadcasted_iota(I32, (t // 8, 8, t), 2)
    _flash_init(m_ref, l_ref, acc_ref)

    def chunk(c, masked):
        rows = pl.ds(pl.multiple_of(c * t, t), t)

        def logits(h):
            cols = slice(h * LANES, (h + 1) * LANES)
            s3 = _split8(_dot_nt(k_ref[rows, cols], q_ref[:, cols]))
            return jnp.where(krow <= qcol, s3, NEG) if masked else s3

        def values(h):
            return vt_ref[h * C_V_DIM:(h + 1) * C_V_DIM, rows]

        _flash_chunk(logits, values, C_HEADS, m_ref, l_ref, acc_ref)

    def body(c, carry):
        chunk(c, False)
        return carry

    lax.fori_loop(0, i, body, 0)
    chunk(i, True)
    _flash_finish(o_ref, l_ref, acc_ref, C_HEADS)


def _mla(q, k, vt, batch, seq):
    t = min(ATT_T, seq)
    nq = seq // t
    return pl.pallas_call(
        _mla_kernel,
        out_shape=jax.ShapeDtypeStruct((batch * seq, C_HEADS * C_V_DIM), BF16),
        grid=(batch, nq),
        in_specs=[pl.BlockSpec((t, C_HEADS * LANES), lambda b, i: (b * nq + i, 0)),
                  pl.BlockSpec((seq, C_HEADS * LANES), lambda b, i: (b, 0)),
                  pl.BlockSpec((C_HEADS * C_V_DIM, seq), lambda b, i: (0, b))],
        out_specs=pl.BlockSpec((t, C_HEADS * C_V_DIM), lambda b, i: (b * nq + i, 0)),
        scratch_shapes=[pltpu.VMEM((C_HEADS, 8, t), F32),
                        pltpu.VMEM((C_HEADS, 8, t), F32),
                        pltpu.VMEM((C_HEADS, C_V_DIM, t), F32)],
        compiler_params=pltpu.CompilerParams(dimension_semantics=("arbitrary", "arbitrary"),
                                             vmem_limit_bytes=VMEM_LIMIT),
        name="mla",
    )(q, k, vt)


def _slab_to_rows(x):
    t = x.shape[0]
    y = jnp.swapaxes(x.reshape(t // 8, 8, 8, LANES), 1, 2)
    return jnp.concatenate([y[:, c].reshape(t, LANES) for c in range(8)], axis=1)


def _rows_to_slab(v):
    t = v.shape[0]
    y = jnp.stack([v[:, c * LANES:(c + 1) * LANES].reshape(t // 8, 8, LANES) for c in range(8)], axis=1)
    return jnp.swapaxes(y, 1, 2).reshape(t, 8, LANES)


def _issue_rows(n_tok, copy):
    def body(g, carry):
        for u in range(8):
            for j in range(TOP_K):
                copy(g * 8 + u, j).start(priority=(u * TOP_K + j) % 2)
        return carry
    lax.fori_loop(0, n_tok // 8, body, 0)


def _merge_kernel(x_ref, oa_ref, ob_ref, oc_ref, wg_ref, wbr_ref, wo_ref, g1_ref, b1_ref,
                  rw_ref, rb_ref, x1_o, x1r_o, ids_o, rk_o, gt_o, cnt_o, run_ref, *, alpha):
    tm = x_ref.shape[0]

    @pl.when(pl.program_id(0) == 0)
    def _():
        run_ref[...] = jnp.zeros_like(run_ref)

    x = x_ref[...]
    xb = x.astype(BF16)
    merged = jnp.zeros((tm, D_MODEL), F32)
    for n, o_ref in enumerate((oa_ref, ob_ref, oc_ref)):
        z = _dot(xb, wg_ref[:, n * D_MODEL:(n + 1) * D_MODEL])
        merged = merged + (1.0 / (1.0 + jnp.exp(-z))) * _dot(o_ref[...], wbr_ref[n])
    y = _dot(merged.astype(BF16), wo_ref[...])
    x1 = _layer_norm(alpha * x + y, g1_ref[...], b1_ref[...])
    x1_o[...] = x1
    x1r_o[...] = _rows_to_slab(x1)

    lane = lax.broadcasted_iota(I32, (tm, LANES), 1)
    logits = jnp.dot(x1, rw_ref[...], precision=lax.Precision.HIGHEST,
                     preferred_element_type=F32) + rb_ref[...]
    lg = jnp.where(lane < N_EXPERTS, logits, -jnp.inf)
    ids, vals = [], []
    for _ in range(TOP_K):
        mx = jnp.max(lg, axis=1, keepdims=True)
        idx = jnp.min(jnp.where(lg == mx, lane, LANES), axis=1, keepdims=True)
        ids.append(idx)
        vals.append(mx)
        lg = jnp.where(lane == idx, -jnp.inf, lg)
    es = [jnp.exp(v - vals[0]) for v in vals]
    den = es[0] + es[1] + es[2] + es[3]

    hot = jnp.zeros((tm, LANES), F32)
    for idx in ids:
        hot = hot + jnp.where(lane == idx, 1.0, 0.0)
    r2 = lax.broadcasted_iota(I32, (tm, tm), 0)
    c2 = lax.broadcasted_iota(I32, (tm, tm), 1)
    lower = jnp.where(c2 < r2, 1.0, 0.0).astype(BF16)
    base = run_ref[0:1, :] + _dot(lower, hot.astype(BF16))
    ids_v = jnp.zeros((tm, LANES), I32)
    rk_v = jnp.zeros((tm, LANES), I32)
    gt_v = jnp.zeros((tm, LANES), F32)
    for j in range(TOP_K):
        rank = jnp.sum(jnp.where(lane == ids[j], base, 0.0), axis=1, keepdims=True)
        ids_v = jnp.where(lane == j, ids[j], ids_v)
        rk_v = jnp.where(lane == j, rank.astype(I32), rk_v)
        gt_v = jnp.where(lane == j, es[j] / den, gt_v)
    ids_o[...] = ids_v
    rk_o[...] = rk_v
    gt_o[...] = gt_v
    run = run_ref[0:1, :] + jnp.sum(hot, axis=0, keepdims=True)
    run_ref[...] = jnp.broadcast_to(run, run_ref.shape)
    cnt_o[...] = jnp.broadcast_to(run, cnt_o.shape)


def _merge(x2, oa, ob, oc, wts, alpha):
    n = x2.shape[0]
    tm = min(MERGE_TM, n)
    assert n % tm == 0
    row = lambda w: pl.BlockSpec((tm, w), lambda i: (i, 0))
    in_specs = [row(D_MODEL), row(512), row(512), row(512),
                _const_spec((D_MODEL, N_BRANCH * D_MODEL)),
                _const_spec((N_BRANCH, BRANCH_WIDTH, D_MODEL)),
                _const_spec((D_MODEL, D_MODEL)),
                _const_spec((1, D_MODEL)), _const_spec((1, D_MODEL)),
                _const_spec((D_MODEL, LANES)), _const_spec((1, LANES))]
    out_shape = [jax.ShapeDtypeStruct((n, D_MODEL), F32),
                 jax.ShapeDtypeStruct((n, 8, LANES), F32),
                 jax.ShapeDtypeStruct((n, LANES), I32),
                 jax.ShapeDtypeStruct((n, LANES), I32),
                 jax.ShapeDtypeStruct((n, LANES), F32),
                 jax.ShapeDtypeStruct((8, LANES), F32)]
    out_specs = [row(D_MODEL), pl.BlockSpec((tm, 8, LANES), lambda i: (i, 0, 0)),
                 row(LANES), row(LANES), row(LANES), _const_spec((8, LANES))]
    return pl.pallas_call(
        functools.partial(_merge_kernel, alpha=alpha),
        out_shape=out_shape,
        grid=(n // tm,),
        in_specs=in_specs,
        out_specs=out_specs,
        scratch_shapes=[pltpu.VMEM((8, LANES), F32)],
        compiler_params=pltpu.CompilerParams(dimension_semantics=("arbitrary",),
                                             vmem_limit_bytes=VMEM_LIMIT),
        name="merge",
    )(x2, oa, ob, oc, wts["wg"], wts["wbr"], wts["wo"], wts["g1"], wts["b1"], wts["rw"], wts["rb"])


def _scatter_kernel(zs_ref, pos_ref, x_ref, xs_ref, zbuf, sem):
    tm = x_ref.shape[0]
    tg = zbuf.shape[0]

    @pl.when(pl.program_id(0) == 0)
    def _():
        zbuf[...] = jnp.zeros(zbuf.shape, F32)
        for e in range(N_EXPERTS):
            cp = pltpu.make_async_copy(zbuf, xs_ref.at[pl.ds(zs_ref[e], tg)], sem)
            cp.start()
            cp.wait()
        n_tiles = xs_ref.shape[0] // tg
        for k in range(N_EXPERTS):
            @pl.when(n_tiles - 1 - k >= zs_ref[N_EXPERTS])
            def _():
                cp = pltpu.make_async_copy(zbuf, xs_ref.at[pl.ds((n_tiles - 1 - k) * tg, tg)], sem)
                cp.start()
                cp.wait()

    _issue_rows(tm, lambda tok, j: pltpu.make_async_copy(
        x_ref.at[tok], xs_ref.at[pos_ref[tok * TOP_K + j]], sem))
    for j in range(TOP_K):
        pltpu.make_async_copy(x_ref, xs_ref.at[pl.ds(0, tm)], sem).wait()


def _scatter_rows(zstart, pos_flat, x1r, n_rows):
    n = x1r.shape[0]
    tm = min(DISPATCH_TM, n)
    assert n % tm == 0
    grid_spec = pltpu.PrefetchScalarGridSpec(
        num_scalar_prefetch=1,
        grid=(n // tm,),
        in_specs=[pl.BlockSpec((tm * TOP_K,), lambda i, zs: (i,), memory_space=pltpu.SMEM),
                  pl.BlockSpec((tm, 8, LANES), lambda i, zs: (i, 0, 0))],
        out_specs=pl.BlockSpec(memory_space=pl.ANY),
        scratch_shapes=[pltpu.VMEM((EXPERT_TG, 8, LANES), F32), pltpu.SemaphoreType.DMA],
    )
    return pl.pallas_call(
        _scatter_kernel,
        out_shape=jax.ShapeDtypeStruct((n_rows, 8, LANES), F32),
        grid_spec=grid_spec,
        compiler_params=pltpu.CompilerParams(dimension_semantics=("arbitrary",),
                                             vmem_limit_bytes=VMEM_LIMIT),
        name="dispatch",
    )(zstart, pos_flat, x1r)


def _ffn_kernel(te_ref, nu_ref, xs_ref, wgu_ref, bgu_ref, wdn_ref, bdn_ref, y_ref, wgu_b, wdn_b):
    t = pl.program_id(0)
    valid = t < nu_ref[0]
    fresh = jnp.logical_or(t == 0, te_ref[jnp.maximum(t - 1, 0)] != te_ref[t])

    @pl.when(jnp.logical_and(valid, fresh))
    def _():
        for r in range(0, D_MODEL, LANES):
            wgu_b[r:r + LANES, :] = wgu_ref[r:r + LANES, :].astype(BF16)
        for r in range(0, D_FF, LANES):
            wdn_b[r:r + LANES, :] = wdn_ref[r:r + LANES, :].astype(BF16)

    @pl.when(valid)
    def _():
        h = _dot(_slab_to_rows(xs_ref[...]).astype(BF16), wgu_b[...]) + bgu_ref[...]
        glu = jnp.minimum(h[:, :D_FF], SWIGLU_LIMIT)
        lin = jnp.clip(h[:, D_FF:], -SWIGLU_LIMIT, SWIGLU_LIMIT)
        act = (lin + 1.0) * (glu * (1.0 / (1.0 + jnp.exp(-SWIGLU_ALPHA * glu))))
        y_ref[...] = _rows_to_slab(_dot(act.astype(BF16), wdn_b[...]) + bdn_ref[...])

    @pl.when(jnp.logical_not(valid))
    def _():
        y_ref[...] = jnp.zeros(y_ref.shape, F32)


def _ffn(tile_e, n_used, xs, wgu, bgu, wdn, bdn, layer):
    tg = EXPERT_TG
    n_tiles = xs.shape[0] // tg
    wspec = lambda r, c: pl.BlockSpec((None, None, r, c), lambda t, te, nu: (layer, te[t], 0, 0))
    slab = pl.BlockSpec((tg, 8, LANES), lambda t, te, nu: (t, 0, 0))
    grid_spec = pltpu.PrefetchScalarGridSpec(
        num_scalar_prefetch=2,
        grid=(n_tiles,),
        in_specs=[slab, wspec(D_MODEL, 2 * D_FF), wspec(1, 2 * D_FF), wspec(D_FF, D_MODEL),
                  wspec(1, D_MODEL)],
        out_specs=slab,
        scratch_shapes=[pltpu.VMEM((D_MODEL, 2 * D_FF), BF16), pltpu.VMEM((D_FF, D_MODEL), BF16)],
    )
    return pl.pallas_call(
        _ffn_kernel,
        out_shape=jax.ShapeDtypeStruct(xs.shape, F32),
        grid_spec=grid_spec,
        compiler_params=pltpu.CompilerParams(dimension_semantics=("arbitrary",),
                                             vmem_limit_bytes=VMEM_LIMIT),
        name="experts",
    )(tile_e, n_used, xs, wgu, bgu, wdn, bdn)


def _gather_kernel(posc_ref, posn_ref, x1_ref, gt_ref, g2_ref, b2_ref, y_ref, o_ref, buf, sem, *, alpha):
    tm = x1_ref.shape[0]
    i = pl.program_id(0)
    slot = lax.rem(i, 2)

    def issue(pos_ref, sl):
        _issue_rows(tm, lambda tok, j: pltpu.make_async_copy(
            y_ref.at[pos_ref[tok * TOP_K + j]], buf.at[sl, j, tok], sem.at[sl]))

    @pl.when(i == 0)
    def _():
        issue(posc_ref, 0)

    @pl.when(i + 1 < pl.num_programs(0))
    def _():
        issue(posn_ref, 1 - slot)

    for j in range(TOP_K):
        pltpu.make_async_copy(y_ref.at[pl.ds(0, tm)], buf.at[slot, j], sem.at[slot]).wait()

    f = jnp.zeros((tm, D_MODEL), F32)
    for j in range(TOP_K):
        f = f + gt_ref[:, j:j + 1] * _slab_to_rows(buf[slot, j])
    o_ref[...] = _layer_norm(alpha * x1_ref[...] + f, g2_ref[...], b2_ref[...])


def _gather_rows(pos_flat, x1, gates, g2, b2, y, alpha):
    n = x1.shape[0]
    tm = min(COMBINE_TM, n)
    assert n % tm == 0
    nblk = n // tm
    smem = lambda f: pl.BlockSpec((tm * TOP_K,), lambda i: (f(i),), memory_space=pltpu.SMEM)
    return pl.pallas_call(
        functools.partial(_gather_kernel, alpha=alpha),
        out_shape=jax.ShapeDtypeStruct((n, D_MODEL), F32),
        grid=(nblk,),
        in_specs=[smem(lambda i: i), smem(lambda i: jnp.minimum(i + 1, nblk - 1)),
                  pl.BlockSpec((tm, D_MODEL), lambda i: (i, 0)),
                  pl.BlockSpec((tm, LANES), lambda i: (i, 0)),
                  _const_spec((1, D_MODEL)), _const_spec((1, D_MODEL)),
                  pl.BlockSpec(memory_space=pl.ANY)],
        out_specs=pl.BlockSpec((tm, D_MODEL), lambda i: (i, 0)),
        scratch_shapes=[pltpu.VMEM((2, TOP_K, tm, 8, LANES), F32), pltpu.SemaphoreType.DMA((2,))],
        compiler_params=pltpu.CompilerParams(dimension_semantics=("arbitrary",),
                                             vmem_limit_bytes=VMEM_LIMIT),
        name="combine",
    )(pos_flat, pos_flat, x1, gates, g2, b2, y)


def _rope_tables(positions):
    pos = positions.reshape(-1).astype(F32)[:, None]
    n = pos.shape[0]

    def cs(rot_dim):
        half = rot_dim // 2
        inv_freq = ROPE_THETA ** (-jnp.arange(half, dtype=F32) / half)
        ang = pos * inv_freq
        return jnp.cos(ang), jnp.sin(ang)

    cos_p, sin_p = cs(A_ROT_DIM)
    cos_c, sin_c = cs(C_ROPE_DIM)
    hp, hc = A_ROT_DIM // 2, C_ROPE_DIM // 2
    one = lambda w: jnp.ones((n, w), F32)
    zero = lambda w: jnp.zeros((n, w), F32)

    a64 = jnp.concatenate([cos_p, cos_p, one(64 - 2 * hp)] * 2, axis=1)
    bm64 = jnp.concatenate([-sin_p, zero(64 - hp)] * 2, axis=1)
    bp64 = jnp.concatenate([zero(hp), sin_p, zero(64 - 2 * hp)] * 2, axis=1)
    a128 = jnp.concatenate([one(64), cos_c, cos_c, one(32)], axis=1)
    bm128 = jnp.concatenate([zero(64), -sin_c, zero(64 - hc)], axis=1)
    bp128 = jnp.concatenate([zero(64 + hc), sin_c, zero(32)], axis=1)
    iw_scale = jnp.full((n, IDX_HEADS), IDX_HEADS ** -0.5, F32)
    as2 = jnp.concatenate([iw_scale, zero(64 - IDX_HEADS), cos_c, cos_c, zero(32)], axis=1)
    return {"t64": jnp.stack([a64, bm64, bp64]),
            "t128": jnp.stack([a128, bm128, bp128]),
            "ts2": jnp.stack([as2, bm128, bp128])}


def _layer_weights(l, w_in, gm_norm_g, gm_norm_b, gm_w_s, gm_b_s, mla_q_norm, mla_kv_norm,
                   mla_w_uq, mla_w_ukv, w_branch, w_out, ln1_g, ln1_b, router_w, router_b):
    w = w_in[l]
    o = 0
    cols = {}
    for name, width in (("aq", 512), ("ak", 512), ("av", 512), ("iq", 512), ("ik", 64), ("iw", 8),
                        ("buv", 1024), ("cdq", 256), ("cdkv", 128), ("ckr", 32), ("gate", 3072)):
        cols[name] = w[:, o:o + width]
        o += width
    zc = lambda width: jnp.zeros((D_MODEL, width), F32)
    s2 = jnp.concatenate([cols["iw"], zc(64 - IDX_HEADS), cols["ckr"], zc(32)], axis=1)
    wp = jnp.concatenate([cols["aq"] * (A_HEAD_DIM ** -0.5 * LOG2E), cols["ak"],
                          cols["iq"] * (IDX_HEAD_DIM ** -0.5), cols["av"],
                          cols["ik"], cols["ik"], s2, cols["buv"], cols["cdq"], cols["cdkv"]], axis=1)

    uq = mla_w_uq[l].reshape(C_Q_RANK, C_HEADS, C_NOPE_DIM + C_ROPE_DIM)
    wuq = jnp.pad(uq, ((0, 0), (0, 0), (0, LANES - C_NOPE_DIM - C_ROPE_DIM))).reshape(C_Q_RANK, -1)
    wuq = wuq * ((C_NOPE_DIM + C_ROPE_DIM) ** -0.5 * LOG2E)
    ukv = mla_w_ukv[l].reshape(C_KV_RANK, C_HEADS, C_NOPE_DIM + C_V_DIM)
    wk = jnp.pad(ukv[:, :, :C_NOPE_DIM], ((0, 0), (0, 0), (0, LANES - C_NOPE_DIM))).reshape(C_KV_RANK, -1)
    wv = ukv[:, :, C_NOPE_DIM:].reshape(C_KV_RANK, -1)
    src = jnp.arange(LANES)[:, None]
    dst = jnp.arange(C_HEADS * LANES)[None, :]
    place = ((dst % LANES == src) & (src >= C_NOPE_DIM) & (src < C_NOPE_DIM + C_ROPE_DIM)).astype(F32)
    wke = jnp.concatenate([wk, place], axis=0)

    tril = jnp.tril(jnp.ones((B_CHUNK, B_CHUNK), dtype=bool))
    ws = jnp.where(tril[None], gm_w_s[l], 0)
    bsf = jnp.repeat(gm_b_s[l].T, B_GROUP_DIM, axis=1)
    rw = jnp.pad(router_w[l], ((0, 0), (0, LANES - N_EXPERTS)))
    rb = jnp.pad(router_b[l], (0, LANES - N_EXPERTS))[None, :]
    return {
        "wp": wp.astype(BF16), "wuq": wuq.astype(BF16), "wke": wke.astype(BF16), "wv": wv.astype(BF16),
        "ws": ws.astype(BF16), "gmg": gm_norm_g[l][None, :], "gmb": gm_norm_b[l][None, :], "bsf": bsf,
        "qn": mla_q_norm[l][None, :], "kvn": mla_kv_norm[l][None, :],
        "wg": cols["gate"].astype(BF16), "wbr": w_branch[l].astype(BF16), "wo": w_out[l].astype(BF16),
        "g1": ln1_g[l][None, :], "b1": ln1_b[l][None, :], "rw": rw, "rb": rb,
    }


def _expert_tiles(counts, n_tiles):
    tg = EXPERT_TG
    cnt = counts[0, :N_EXPERTS].astype(I32)
    tiles_per = (cnt + tg - 1) // tg
    tile_end = jnp.cumsum(tiles_per)
    offsets = (tile_end - tiles_per) * tg
    n_used = tile_end[-1]
    tid = jnp.minimum(jnp.arange(n_tiles, dtype=I32), n_used - 1)
    tile_e = jnp.sum(tid[:, None] >= tile_end[None, :], axis=1).astype(I32)
    off128 = jnp.pad(offsets, (0, LANES - N_EXPERTS)).astype(I32)
    return off128, tile_e, n_used.reshape(1).astype(I32)


def _sorted_positions(offsets, counts, n_used, ids, ranks, n_rows):
    e = ids[:, :TOP_K]
    onehot = e[..., None] == jnp.arange(N_EXPERTS, dtype=I32)
    pos = ranks[:, :TOP_K] + jnp.sum(jnp.where(onehot, offsets[:N_EXPERTS], 0), axis=-1)
    cnt = counts[0, :N_EXPERTS].astype(I32)
    zstart = jnp.minimum(offsets[:N_EXPERTS] + cnt, n_rows - EXPERT_TG)
    return pos.reshape(-1).astype(I32), jnp.concatenate([zstart.astype(I32), n_used])


def _layer(l, x2, tabs, batch, seq, depth, p):
    alpha = (2 * depth) ** 0.25
    n = x2.shape[0]
    wts = _layer_weights(l, p["w_in"], p["gm_norm_g"], p["gm_norm_b"], p["gm_w_s"], p["gm_b_s"],
                         p["mla_q_norm"], p["mla_kv_norm"], p["mla_w_uq"], p["mla_w_ukv"],
                         p["w_branch"], p["w_out"], p["ln1_g"], p["ln1_b"], p["router_w"], p["router_b"])
    aq, ak, iq, avt, ik2, iwt, o_b, q, k, vt = _proj(x2, wts, tabs)
    o_a = _dsa(iq, iwt, aq, ik2, ak, avt, batch, seq)
    o_c = _mla(q, k, vt, batch, seq)
    x1, x1r, ids, ranks, gates, counts = _merge(x2, o_a, o_b, o_c, wts, alpha)

    n_tiles = n * TOP_K // EXPERT_TG + N_EXPERTS
    offsets, tile_e, n_used = _expert_tiles(counts, n_tiles)
    n_rows = n_tiles * EXPERT_TG
    pos_flat, zstart = _sorted_positions(offsets, counts, n_used, ids, ranks, n_rows)
    xs = _scatter_rows(zstart, pos_flat, x1r, n_rows)
    y = _ffn(tile_e, n_used, xs, p["exp_w_gu"], p["exp_b_gu"][:, :, None, :],
             p["exp_w_dn"], p["exp_b_dn"][:, :, None, :], l)
    return _gather_rows(pos_flat, x1, gates, p["ln2_g"][l][None, :], p["ln2_b"][l][None, :], y, alpha)


def kernel(x, positions, w_in, gm_norm_g, gm_norm_b, gm_w_s, gm_b_s, mla_q_norm, mla_kv_norm, mla_w_uq, mla_w_ukv, w_branch, w_out, ln1_g, ln1_b, router_w, router_b, exp_w_gu, exp_b_gu, exp_w_dn, exp_b_dn, ln2_g, ln2_b):
    batch, seq, _ = x.shape
    depth = w_in.shape[0]
    p = dict(w_in=w_in, gm_norm_g=gm_norm_g, gm_norm_b=gm_norm_b, gm_w_s=gm_w_s, gm_b_s=gm_b_s,
             mla_q_norm=mla_q_norm, mla_kv_norm=mla_kv_norm, mla_w_uq=mla_w_uq, mla_w_ukv=mla_w_ukv,
             w_branch=w_branch, w_out=w_out, ln1_g=ln1_g, ln1_b=ln1_b, router_w=router_w,
             router_b=router_b, exp_w_gu=exp_w_gu, exp_b_gu=exp_b_gu, exp_w_dn=exp_w_dn,
             exp_b_dn=exp_b_dn, ln2_g=ln2_g, ln2_b=ln2_b)
    tabs = _rope_tables(positions)
    x2 = x.reshape(batch * seq, D_MODEL)
    for l in range(depth):
        x2 = _layer(l, x2, tabs, batch, seq, depth, p)
    return x2.reshape(batch, seq, D_MODEL)
```
